```python
import jax
import jax.numpy as jnp
from jax import lax
import numpy as np

D_MODEL = 2048
BATCH = 2
SEQ = 4096
DEPTH = 4

GRID_W = 64
CTX_LEN = 256
N_MIXERS = 3
N_SUB = 3
D_FF = 5632
RMS_EPS = 1e-6
MOD_INIT = 0.5

HG_HEAD_DIM = 128
HG_HEADS = D_MODEL // HG_HEAD_DIM
HG_QF = HG_HEADS * HG_HEAD_DIM
HG_IV = HG_HEADS * HG_HEAD_DIM
HG_CHUNK = 16

MLA_HEADS = 16
MLA_Q_RANK = 512
MLA_KV_RANK = 512
MLA_NOPE = 128
MLA_ROPE = 64
MLA_V = 128
MLA_SCALE = (MLA_NOPE + MLA_ROPE) ** -0.5
ATTN_BLOCK = 128
ROPE_THETA = 10000.0
ROPE_FREQS = MLA_ROPE // 4

FOURIER_GROUPS = 8

kernel_name = 'hybrid_hgrn2_mla_fnet_macaron_dit'


def n_layers_of_kind(kind):
    return len(range(kind, DEPTH, N_MIXERS))


def rmsnorm(x, g):
    xf = x.astype(jnp.float32)
    y = xf * lax.rsqrt(jnp.mean(xf * xf, axis=-1, keepdims=True) + RMS_EPS)
    return (y * g.astype(jnp.float32)).astype(x.dtype)


def adanorm(z, g, shift, scale):
    return rmsnorm(z, g) * (1 + scale) + shift


def swiglu(h, w_gu, w_down):
    gate, up = jnp.split(h @ w_gu, 2, axis=-1)
    return (jax.nn.silu(gate) * up) @ w_down


def ffn_sublayer(z, g, mm, s, w_gu, w_down):
    return z + 0.5 * mm[:, :, s, 2] * swiglu(adanorm(z, g, mm[:, :, s, 0], mm[:, :, s, 1]), w_gu, w_down)


def axial_rope_tables(n_tokens):
    rows = n_tokens // GRID_W
    r = jnp.broadcast_to(jnp.arange(rows, dtype=jnp.float32)[:, None], (rows, GRID_W)).reshape(-1)
    col = jnp.broadcast_to(jnp.arange(GRID_W, dtype=jnp.float32)[None, :], (rows, GRID_W)).reshape(-1)
    inv_freq = ROPE_THETA ** (-jnp.arange(ROPE_FREQS, dtype=jnp.float32) / ROPE_FREQS)
    ang = jnp.stack([r, col], axis=-1)[..., None] * inv_freq
    return jnp.cos(ang), jnp.sin(ang)


def axial_rope(x, cos, sin):
    xs = x.reshape(x.shape[:-1] + (2, 2, ROPE_FREQS))
    xa, xb = xs[..., 0, :], xs[..., 1, :]
    cos, sin = cos.astype(x.dtype), sin.astype(x.dtype)
    return jnp.stack([xa * cos - xb * sin, xb * cos + xa * sin], axis=-2).reshape(x.shape)


def hgrn2_chunk_scan(q, k, v, log_f, s0):
    B, T, H, _ = q.shape
    V = v.shape[-1]
    n = T // HG_CHUNK

    def chunks(a):
        return a.reshape(B, n, HG_CHUNK, H, a.shape[-1]).transpose(1, 0, 3, 2, 4)

    q, k, v, log_f = chunks(q), chunks(k), chunks(v), chunks(log_f)
    b = jnp.cumsum(log_f, axis=-2)
    b_end = b[..., -1:, :]
    q_dec = q * jnp.exp(b)
    k_to_end = k * jnp.exp(b_end - b)
    lower_tri = jnp.tril(jnp.ones((HG_CHUNK, HG_CHUNK), dtype=bool))
    a = jnp.einsum('nbhck,nbhsk->nbhcs', q_dec, k * jnp.exp(-b))
    o_intra = jnp.einsum('nbhcs,nbhsv->nbhcv', jnp.where(lower_tri, a, 0.0), v)
    decay = jnp.exp(b_end[..., 0, :])

    def step(s, inp):
        qd, kd, vn, dn = inp
        o_inter = jnp.einsum('bhck,bhkv->bhcv', qd, s)
        s = s * dn[..., None] + jnp.einsum('bhck,bhcv->bhkv', kd, vn)
        return s, o_inter

    s_final, o_inter = lax.scan(step, s0, (q_dec, k_to_end, v, decay))
    o = (o_intra + o_inter).transpose(1, 0, 3, 2, 4).reshape(B, T, H, V)
    return o, s_final


def hgrn2_mixer(h_lat, h_ctx, w_in, lb_fwd, lb_bwd, g_norm, w_out, with_ctx_out):
    splits = [HG_QF, 2 * HG_QF, 3 * HG_QF, 3 * HG_QF + HG_IV]

    def branch_inputs(h):
        B, T, _ = h.shape
        q, zf, zb, i, gate = jnp.split(h @ w_in, splits, axis=-1)
        heads = lambda a: a.astype(jnp.float32).reshape(B, T, HG_HEADS, HG_HEAD_DIM)
        q, i = heads(jax.nn.silu(q)), heads(i)
        dirs = []
        for z, lb in ((zf, lb_fwd), (zb, lb_bwd)):
            z = heads(z)
            lb = lb.reshape(HG_HEADS, HG_HEAD_DIM)
            log_f = jnp.logaddexp(jnp.log(lb), jnp.log1p(-lb) + jax.nn.log_sigmoid(z))
            k = (1.0 - lb) * jax.nn.sigmoid(-z)
            dirs.append((log_f, k))
        return q, i, gate, dirs

    def readout(o, gate, dtype):
        B, T = o.shape[:2]
        o = rmsnorm(o, g_norm.reshape(HG_HEADS, HG_HEAD_DIM)).reshape(B, T, HG_IV).astype(dtype)
        return (o * jax.nn.silu(gate)) @ w_out

    flip = lambda a: a[:, ::-1]
    qc, ic, gc, (fc, bc) = branch_inputs(h_ctx)
    ql, il, gl, (fl, bl) = branch_inputs(h_lat)
    s_zero = jnp.zeros((h_lat.shape[0], HG_HEADS, HG_HEAD_DIM, HG_HEAD_DIM), jnp.float32)
    o_cf, s_f = hgrn2_chunk_scan(qc, fc[1], ic, fc[0], s_zero)
    o_cb, s_b = hgrn2_chunk_scan(flip(qc), flip(bc[1]), flip(ic), flip(bc[0]), s_zero)
    o_lf, _ = hgrn2_chunk_scan(ql, fl[1], il, fl[0], s_f)
    o_lb, _ = hgrn2_chunk_scan(flip(ql), flip(bl[1]), flip(il), flip(bl[0]), s_b)
    y_lat = readout(o_lf + flip(o_lb), gl, h_lat.dtype)
    y_ctx = readout(o_cf + flip(o_cb), gc, h_ctx.dtype) if with_ctx_out else None
    return y_lat, y_ctx


def mla_mixer(h_lat, h_ctx, w_dqkv, q_norm, kv_norm, w_uq, w_ukv, w_o, cos, sin, with_ctx_out):
    def project(h, rotate):
        B, T, _ = h.shape
        cq, ckv, k_rope = jnp.split(h @ w_dqkv, [MLA_Q_RANK, MLA_Q_RANK + MLA_KV_RANK], axis=-1)
        q = (rmsnorm(cq, q_norm) @ w_uq).reshape(B, T, MLA_HEADS, MLA_NOPE + MLA_ROPE)
        kv = (rmsnorm(ckv, kv_norm) @ w_ukv).reshape(B, T, MLA_HEADS, MLA_NOPE + MLA_V)
        q_nope, q_rope = q[..., :MLA_NOPE], q[..., MLA_NOPE:]
        k_nope, v = kv[..., :MLA_NOPE], kv[..., MLA_NOPE:]
        if rotate:
            q_rope = axial_rope(q_rope, cos[:, None], sin[:, None])
            k_rope = axial_rope(k_rope, cos, sin)
        return q_nope, q_rope, k_nope, k_rope, v

    def attend(q_nope, q_rope, k_nope, k_rope, v):
        s = (jnp.einsum('bqhd,bkhd->bhqk', q_nope, k_nope, preferred_element_type=jnp.float32)
             + jnp.einsum('bqhr,bkr->bhqk', q_rope, k_rope, preferred_element_type=jnp.float32)) * MLA_SCALE
        p = jax.nn.softmax(s, axis=-1).astype(v.dtype)
        return jnp.einsum('bhqk,bkhd->bqhd', p, v)

    qn_c, qr_c, kn_c, kr_c, v_c = project(h_ctx, False)
    qn_l, qr_l, kn_l, kr_l, v_l = project(h_lat, True)
    kn = jnp.concatenate([kn_c, kn_l], axis=1)
    kr = jnp.concatenate([kr_c, kr_l], axis=1)
    vv = jnp.concatenate([v_c, v_l], axis=1)
    B, T = h_lat.shape[:2]
    nb = T // ATTN_BLOCK
    to_blocks = lambda a: jnp.moveaxis(a.reshape((B, nb, ATTN_BLOCK) + a.shape[2:]), 1, 0)
    o_blk = lax.map(lambda qb: attend(qb[0], qb[1], kn, kr, vv), (to_blocks(qn_l), to_blocks(qr_l)))
    y_lat = jnp.moveaxis(o_blk, 0, 1).reshape(B, T, MLA_HEADS * MLA_V) @ w_o
    y_ctx = None
    if with_ctx_out:
        y_ctx = attend(qn_c, qr_c, kn_c, kr_c, v_c).reshape(B, h_ctx.shape[1], MLA_HEADS * MLA_V) @ w_o
    return y_lat, y_ctx


def fourier_mixer(h, w_out):
    B, T, D = h.shape
    hg = h.astype(jnp.float32).reshape(B, T, FOURIER_GROUPS, D // FOURIER_GROUPS)
    y = jnp.fft.fft2(hg, axes=(1, 3), norm='ortho').real
    return y.reshape(B, T, D).astype(h.dtype) @ w_out


def setup_inputs(seed: int = 0) -> dict:
    key = jax.random.key(seed)
    keys = iter(jax.random.split(key, 40))

    def normal(shape, scale=1.0):
        return scale * jax.random.normal(next(keys), shape, jnp.float32)

    def dense(shape, scale=1.0):
        return normal(shape, scale * shape[-2] ** -0.5)

    def gain(shape):
        return 1.0 + normal(shape, 0.05)

    n_a, n_b, n_c = (n_layers_of_kind(k) for k in range(N_MIXERS))
    D = D_MODEL
    return {
        'x': normal((BATCH, SEQ, D)),
        'c': normal((BATCH, D)),
        'ctx': normal((BATCH, CTX_LEN, D)),
        'c_ctx': normal((D,)),
        'mod_w': dense((DEPTH, D, N_SUB * 3 * D), MOD_INIT),
        'mod_b': normal((DEPTH, N_SUB * 3 * D), 0.02),
        'norm_g': gain((DEPTH, N_SUB, D)),
        'ffn1_w_gu': dense((DEPTH, D, 2 * D_FF)),
        'ffn1_w_down': dense((DEPTH, D_FF, D)),
        'ffn2_w_gu': dense((DEPTH, D, 2 * D_FF)),
        'ffn2_w_down': dense((DEPTH, D_FF, D)),
        'hgrn_w_in': dense((n_a, D, 3 * HG_QF + 2 * HG_IV)),
        'hgrn_lb_logits': normal((2, n_a, HG_QF), 0.5),
        'hgrn_g_norm': gain((n_a, HG_IV)),
        'hgrn_w_out': dense((n_a, HG_IV, D)),
        'mla_w_dqkv': dense((n_b, D, MLA_Q_RANK + MLA_KV_RANK + MLA_ROPE)),
        'mla_q_norm': gain((n_b, MLA_Q_RANK)),
        'mla_kv_norm': gain((n_b, MLA_KV_RANK)),
        'mla_w_uq': dense((n_b, MLA_Q_RANK, MLA_HEADS * (MLA_NOPE + MLA_ROPE))),
        'mla_w_ukv': dense((n_b, MLA_KV_RANK, MLA_HEADS * (MLA_NOPE + MLA_V))),
        'mla_w_o': dense((n_b, MLA_HEADS * MLA_V, D)),
        'fnet_w_out': dense((n_c, D, D)),
        'final_g': gain((D,)),
    }


def reference(x, c, ctx, c_ctx, mod_w, mod_b, norm_g, ffn1_w_gu, ffn1_w_down, ffn2_w_gu, ffn2_w_down,
              hgrn_w_in, hgrn_lb_logits, hgrn_g_norm, hgrn_w_out,
              mla_w_dqkv, mla_q_norm, mla_kv_norm, mla_w_uq, mla_w_ukv, mla_w_o,
              fnet_w_out, final_g):
    B, T, D = x.shape
    cos, sin = axial_rope_tables(T)
    lb = jnp.cumsum(jax.nn.softmax(hgrn_lb_logits.astype(jnp.float32), axis=1), axis=1)
    lb = lb - lb[:, :1]
    h, hc = x, ctx
    for i in range(DEPTH):
        kind, j = i % N_MIXERS, i // N_MIXERS
        last = i == DEPTH - 1
        ctx_in = not (last and kind == 2)
        m = (jax.nn.silu(c) @ mod_w[i] + mod_b[i]).reshape(B, 1, N_SUB, 3, D)
        mc = (jax.nn.silu(c_ctx) @ mod_w[i] + mod_b[i]).reshape(1, 1, N_SUB, 3, D)
        h = ffn_sublayer(h, norm_g[i, 0], m, 0, ffn1_w_gu[i], ffn1_w_down[i])
        if ctx_in:
            hc = ffn_sublayer(hc, norm_g[i, 0], mc, 0, ffn1_w_gu[i], ffn1_w_down[i])
        n = adanorm(h, norm_g[i, 1], m[:, :, 1, 0], m[:, :, 1, 1])
        nc = adanorm(hc, norm_g[i, 1], mc[:, :, 1, 0], mc[:, :, 1, 1]) if ctx_in else None
        if kind == 0:
            y, yc = hgrn2_mixer(n, nc, hgrn_w_in[j], lb[0, j], lb[1, j], hgrn_g_norm[j], hgrn_w_out[j], not last)
        elif kind == 1:
            y, yc = mla_mixer(n, nc, mla_w_dqkv[j], mla_q_norm[j], mla_kv_norm[j], mla_w_uq[j], mla_w_ukv[j],
                              mla_w_o[j], cos, sin, not last)
        else:
            y = fourier_mixer(n, fnet_w_out[j])
            yc = None if last else fourier_mixer(nc, fnet_w_out[j])
        h = h + m[:, :, 1, 2] * y
        if not last:
            hc = hc + mc[:, :, 1, 2] * yc
        h = ffn_sublayer(h, norm_g[i, 2], m, 2, ffn2_w_gu[i], ffn2_w_down[i])
        if not last:
            hc = ffn_sublayer(hc, norm_g[i, 2], mc, 2, ffn2_w_gu[i], ffn2_w_down[i])
    return rmsnorm(h, final_g)
```

```python
import functools

import jax
import jax.numpy as jnp
from jax import lax
from jax.experimental import pallas as pl
from jax.experimental.pallas import tpu as pltpu

D_MODEL = 2048
SEQ = 4096
DEPTH = 4
GRID_W = 64
CTX_LEN = 256
N_MIXERS = 3
N_SUB = 3
D_FF = 5632
RMS_EPS = 1e-6

HG_HEAD_DIM = 128
HG_HEADS = D_MODEL // HG_HEAD_DIM
HG_QF = HG_HEADS * HG_HEAD_DIM
HG_IV = HG_HEADS * HG_HEAD_DIM
HG_CHUNK = 16

MLA_HEADS = 16
MLA_Q_RANK = 512
MLA_KV_RANK = 512
MLA_NOPE = 128
MLA_ROPE = 64
MLA_V = 128
MLA_SCALE = (MLA_NOPE + MLA_ROPE) ** -0.5
ATTN_BLOCK = 128
ROPE_THETA = 10000.0
ROPE_FREQS = MLA_ROPE // 4

FOURIER_GROUPS = 8

BF16 = jnp.bfloat16
F32 = jnp.float32

VMEM_LIMIT_BYTES = 56 * 1024 * 1024
MOD_ROWS = 8
N_MOD = N_SUB * 3


def _mod_row(t, tm):
    return jnp.minimum(t // (SEQ // tm), 2)


def _mod_kernel(c_ref, w_ref, b_ref, o_ref):
    c = c_ref[...]
    a = (c * jax.nn.sigmoid(c)).astype(BF16)
    o_ref[...] = jnp.dot(a, w_ref[...].astype(BF16), preferred_element_type=F32) + b_ref[...]


def modulation(cc, mod_w, mod_b, *, tn=1024):
    depth, d, n = mod_w.shape
    return pl.pallas_call(
        _mod_kernel,
        grid=(depth, n // tn),
        in_specs=[
            pl.BlockSpec((MOD_ROWS, d), lambda i, j: (0, 0)),
            pl.BlockSpec((None, d, tn), lambda i, j: (i, 0, j)),
            pl.BlockSpec((None, 1, tn), lambda i, j: (i, 0, j)),
        ],
        out_specs=pl.BlockSpec((None, MOD_ROWS, tn), lambda i, j: (i, 0, j)),
        out_shape=jax.ShapeDtypeStruct((depth, MOD_ROWS, n), F32),
        compiler_params=pltpu.CompilerParams(
            dimension_semantics=("arbitrary", "arbitrary"), vmem_limit_bytes=VMEM_LIMIT_BYTES),
        name="modulation",
    )(cc, mod_w, mod_b.reshape(depth, 1, n))


def _adanorm(x, g, shift, scale):
    y = x * lax.rsqrt(jnp.mean(x * x, axis=-1, keepdims=True) + RMS_EPS)
    return (y * g) * (1.0 + scale) + shift


def _ffn_kernel(x_ref, mod_ref, g_ref, wg_ref, wu_ref, wd_ref, fg_ref, o_ref, n_ref, *, s, nf, final):
    f = pl.program_id(1)

    @pl.when(f == 0)
    def _():
        n = _adanorm(x_ref[...], g_ref[...], mod_ref[3 * s:3 * s + 1, :], mod_ref[3 * s + 1:3 * s + 2, :])
        n_ref[...] = n.astype(BF16)

    n = n_ref[...]
    gate = jnp.dot(n, wg_ref[...], preferred_element_type=F32)
    up = jnp.dot(n, wu_ref[...], preferred_element_type=F32)
    act = (gate * jax.nn.sigmoid(gate) * up).astype(BF16)
    part = jnp.dot(act, wd_ref[...], preferred_element_type=F32)

    @pl.when(f == 0)
    def _():
        o_ref[...] = part

    @pl.when(f > 0)
    def _():
        o_ref[...] += part

    @pl.when(f == nf - 1)
    def _():
        h = x_ref[...] + (0.5 * mod_ref[3 * s + 2:3 * s + 3, :]) * o_ref[...]
        if final:
            h = h * lax.rsqrt(jnp.mean(h * h, axis=-1, keepdims=True) + RMS_EPS) * fg_ref[...]
        o_ref[...] = h


def ffn_sublayer(h, mods, g, w_gu, w_down, final_g, *, layer, s, rows, tm, tf, final=False):
    d = h.shape[1]
    nf = D_FF // tf
    kern = functools.partial(_ffn_kernel, s=s, nf=nf, final=final)
    return pl.pallas_call(
        kern,
        grid=(rows // tm, nf),
        in_specs=[
            pl.BlockSpec((tm, d), lambda t, f: (t, 0)),
            pl.BlockSpec((None, None, N_MOD, d), lambda t, f: (layer, _mod_row(t, tm), 0, 0)),
            pl.BlockSpec((None, 1, d), lambda t, f: (layer * N_SUB + s, 0, 0)),
            pl.BlockSpec((None, d, tf), lambda t, f: (layer, 0, f)),
            pl.BlockSpec((None, d, tf), lambda t, f: (layer, 0, f + nf)),
            pl.BlockSpec((None, tf, d), lambda t, f: (layer, f, 0)),
            pl.BlockSpec((1, d), lambda t, f: (0, 0)),
        ],
        out_specs=pl.BlockSpec((tm, d), lambda t, f: (t, 0)),
        out_shape=jax.ShapeDtypeStruct((rows if final else h.shape[0], d), F32),
        scratch_shapes=[pltpu.VMEM((tm, d), BF16)],
        compiler_params=pltpu.CompilerParams(
            dimension_semantics=("arbitrary", "arbitrary"), vmem_limit_bytes=VMEM_LIMIT_BYTES),
        name=f"ffn_l{layer}_s{s}",
    )(h, mods, g, w_gu, w_gu, w_down, final_g)


def rmsnorm(x, g):
    xf = x.astype(jnp.float32)
    y = xf * lax.rsqrt(jnp.mean(xf * xf, axis=-1, keepdims=True) + RMS_EPS)
    return (y * g.astype(jnp.float32)).astype(x.dtype)


def adanorm(z, g, shift, scale):
    return rmsnorm(z, g) * (1 + scale) + shift


def axial_rope_tables(n_tokens):
    rows = n_tokens // GRID_W
    r = jnp.broadcast_to(jnp.arange(rows, dtype=jnp.float32)[:, None], (rows, GRID_W)).reshape(-1)
    col = jnp.broadcast_to(jnp.arange(GRID_W, dtype=jnp.float32)[None, :], (rows, GRID_W)).reshape(-1)
    inv_freq = ROPE_THETA ** (-jnp.arange(ROPE_FREQS, dtype=jnp.float32) / ROPE_FREQS)
    ang = jnp.stack([r, col], axis=-1)[..., None] * inv_freq
    return jnp.cos(ang), jnp.sin(ang)


def axial_rope(x, cos, sin):
    xs = x.reshape(x.shape[:-1] + (2, 2, ROPE_FREQS))
    xa, xb = xs[..., 0, :], xs[..., 1, :]
    cos, sin = cos.astype(x.dtype), sin.astype(x.dtype)
    return jnp.stack([xa * cos - xb * sin, xb * cos + xa * sin], axis=-2).reshape(x.shape)


def hgrn2_chunk_scan(q, k, v, log_f, s0):
    B, T, H, _ = q.shape
    n = T // HG_CHUNK

    def chunks(a):
        return a.reshape(B, n, HG_CHUNK, H, a.shape[-1]).transpose(1, 0, 3, 2, 4)

    q, k, v, log_f = chunks(q), chunks(k), chunks(v), chunks(log_f)
    b = jnp.cumsum(log_f, axis=-2)
    b_end = b[..., -1:, :]
    q_dec = q * jnp.exp(b)
    k_to_end = k * jnp.exp(b_end - b)
    lower_tri = jnp.tril(jnp.ones((HG_CHUNK, HG_CHUNK), dtype=bool))
    a = jnp.einsum('nbhck,nbhsk->nbhcs', q_dec, k * jnp.exp(-b))
    o_intra = jnp.einsum('nbhcs,nbhsv->nbhcv', jnp.where(lower_tri, a, 0.0), v)
    decay = jnp.exp(b_end[..., 0, :])

    def step(s, inp):
        qd, kd, vn, dn = inp
        o_inter = jnp.einsum('bhck,bhkv->bhcv', qd, s)
        s = s * dn[..., None] + jnp.einsum('bhck,bhcv->bhkv', kd, vn)
        return s, o_inter

    s_final, o_inter = lax.scan(step, s0, (q_dec, k_to_end, v, decay))
    o = (o_intra + o_inter).transpose(1, 0, 3, 2, 4).reshape(B, T, H, v.shape[-1])
    return o, s_final


def hgrn2_mixer(h_lat, h_ctx, w_in, lb_fwd, lb_bwd, g_norm, w_out, with_ctx_out):
    splits = [HG_QF, 2 * HG_QF, 3 * HG_QF, 3 * HG_QF + HG_IV]

    def branch_inputs(h):
        B, T, _ = h.shape
        q, zf, zb, i, gate = jnp.split(h @ w_in, splits, axis=-1)
        heads = lambda a: a.astype(jnp.float32).reshape(B, T, HG_HEADS, HG_HEAD_DIM)
        q, i = heads(jax.nn.silu(q)), heads(i)
        dirs = []
        for z, lb in ((zf, lb_fwd), (zb, lb_bwd)):
            z = heads(z)
            lb = lb.reshape(HG_HEADS, HG_HEAD_DIM)
            log_f = jnp.logaddexp(jnp.log(lb), jnp.log1p(-lb) + jax.nn.log_sigmoid(z))
            k = (1.0 - lb) * jax.nn.sigmoid(-z)
            dirs.append((log_f, k))
        return q, i, gate, dirs

    def readout(o, gate, dtype):
        B, T = o.shape[:2]
        o = rmsnorm(o, g_norm.reshape(HG_HEADS, HG_HEAD_DIM)).reshape(B, T, HG_IV).astype(dtype)
        return (o * jax.nn.silu(gate)) @ w_out

    flip = lambda a: a[:, ::-1]
    qc, ic, gc, (fc, bc) = branch_inputs(h_ctx)
    ql, il, gl, (fl, bl) = branch_inputs(h_lat)
    s_zero = jnp.zeros((h_lat.shape[0], HG_HEADS, HG_HEAD_DIM, HG_HEAD_DIM), jnp.float32)
    o_cf, s_f = hgrn2_chunk_scan(qc, fc[1], ic, fc[0], s_zero)
    o_cb, s_b = hgrn2_chunk_scan(flip(qc), flip(bc[1]), flip(ic), flip(bc[0]), s_zero)
    o_lf, _ = hgrn2_chunk_scan(ql, fl[1], il, fl[0], s_f)
    o_lb, _ = hgrn2_chunk_scan(flip(ql), flip(bl[1]), flip(il), flip(bl[0]), s_b)
    y_lat = readout(o_lf + flip(o_lb), gl, h_lat.dtype)
    y_ctx = readout(o_cf + flip(o_cb), gc, h_ctx.dtype) if with_ctx_out else None
    return y_lat, y_ctx


def mla_mixer(h_lat, h_ctx, w_dqkv, q_norm, kv_norm, w_uq, w_ukv, w_o, cos, sin, with_ctx_out):
    def project(h, rotate):
        B, T, _ = h.shape
        cq, ckv, k_rope = jnp.split(h @ w_dqkv, [MLA_Q_RANK, MLA_Q_RANK + MLA_KV_RANK], axis=-1)
        q = (rmsnorm(cq, q_norm) @ w_uq).reshape(B, T, MLA_HEADS, MLA_NOPE + MLA_ROPE)
        kv = (rmsnorm(ckv, kv_norm) @ w_ukv).reshape(B, T, MLA_HEADS, MLA_NOPE + MLA_V)
        q_nope, q_rope = q[..., :MLA_NOPE], q[..., MLA_NOPE:]
        k_nope, v = kv[..., :MLA_NOPE], kv[..., MLA_NOPE:]
        if rotate:
            q_rope = axial_rope(q_rope, cos[:, None], sin[:, None])
            k_rope = axial_rope(k_rope, cos, sin)
        return q_nope, q_rope, k_nope, k_rope, v

    def attend(q_nope, q_rope, k_nope, k_rope, v):
        s = (jnp.einsum('bqhd,bkhd->bhqk', q_nope, k_nope, preferred_element_type=jnp.float32)
             + jnp.einsum('bqhr,bkr->bhqk', q_rope, k_rope, preferred_element_type=jnp.float32)) * MLA_SCALE
        p = jax.nn.softmax(s, axis=-1).astype(v.dtype)
        return jnp.einsum('bhqk,bkhd->bqhd', p, v)

    qn_c, qr_c, kn_c, kr_c, v_c = project(h_ctx, False)
    qn_l, qr_l, kn_l, kr_l, v_l = project(h_lat, True)
    kn = jnp.concatenate([kn_c, kn_l], axis=1)
    kr = jnp.concatenate([kr_c, kr_l], axis=1)
    vv = jnp.concatenate([v_c, v_l], axis=1)
    B, T = h_lat.shape[:2]
    nb = T // ATTN_BLOCK
    to_blocks = lambda a: jnp.moveaxis(a.reshape((B, nb, ATTN_BLOCK) + a.shape[2:]), 1, 0)
    o_blk = lax.map(lambda qb: attend(qb[0], qb[1], kn, kr, vv), (to_blocks(qn_l), to_blocks(qr_l)))
    y_lat = jnp.moveaxis(o_blk, 0, 1).reshape(B, T, MLA_HEADS * MLA_V) @ w_o
    y_ctx = None
    if with_ctx_out:
        y_ctx = attend(qn_c, qr_c, kn_c, kr_c, v_c).reshape(B, h_ctx.shape[1], MLA_HEADS * MLA_V) @ w_o
    return y_lat, y_ctx


def fourier_mixer(h, w_out):
    B, T, D = h.shape
    hg = h.astype(jnp.float32).reshape(B, T, FOURIER_GROUPS, D // FOURIER_GROUPS)
    y = jnp.fft.fft2(hg, axes=(1, 3), norm='ortho').real
    return y.reshape(B, T, D).astype(h.dtype) @ w_out


def kernel(x, c, ctx, c_ctx, mod_w, mod_b, norm_g, ffn1_w_gu, ffn1_w_down, ffn2_w_gu, ffn2_w_down,
           hgrn_w_in, hgrn_lb_logits, hgrn_g_norm, hgrn_w_out,
           mla_w_dqkv, mla_q_norm, mla_kv_norm, mla_w_uq, mla_w_ukv, mla_w_o,
           fnet_w_out, final_g):
    B, T, D = x.shape
    n_lat = B * T
    n_ctx = B * CTX_LEN
    tm, tf = 512, 512
    rows_all = n_lat + n_ctx

    cos, sin = axial_rope_tables(T)
    lb = jnp.cumsum(jax.nn.softmax(hgrn_lb_logits.astype(jnp.float32), axis=1), axis=1)
    lb = lb - lb[:, :1]

    cc = jnp.concatenate([c, c_ctx[None, :], jnp.zeros((MOD_ROWS - B - 1, D), F32)], axis=0)
    mods = modulation(cc, mod_w, mod_b).reshape(DEPTH, MOD_ROWS, N_MOD, D)

    w1_gu, w1_down = ffn1_w_gu.astype(BF16), ffn1_w_down.astype(BF16)
    w2_gu, w2_down = ffn2_w_gu.astype(BF16), ffn2_w_down.astype(BF16)
    norm_g3 = norm_g.reshape(DEPTH * N_SUB, 1, D)
    final_g2 = final_g.reshape(1, D)

    hs = jnp.concatenate([x.reshape(n_lat, D), ctx.reshape(n_ctx, D)], axis=0)
    for i in range(DEPTH):
        kind, j = i % N_MIXERS, i // N_MIXERS
        last = i == DEPTH - 1
        ctx_in = not (last and kind == 2)
        m = mods[i, :B].reshape(B, 1, N_SUB, 3, D)
        mc = mods[i, B].reshape(1, 1, N_SUB, 3, D)
        hs = ffn_sublayer(hs, mods, norm_g3, w1_gu, w1_down, final_g2, layer=i, s=0,
                          rows=rows_all if ctx_in else n_lat, tm=tm, tf=tf)
        h = hs[:n_lat].reshape(B, T, D)
        hc = hs[n_lat:].reshape(B, CTX_LEN, D)
        n = adanorm(h, norm_g[i, 1], m[:, :, 1, 0], m[:, :, 1, 1])
        nc = adanorm(hc, norm_g[i, 1], mc[:, :, 1, 0], mc[:, :, 1, 1]) if ctx_in else None
        if kind == 0:
            y, yc = hgrn2_mixer(n, nc, hgrn_w_in[j], lb[0, j], lb[1, j], hgrn_g_norm[j], hgrn_w_out[j], not last)
        elif kind == 1:
            y, yc = mla_mixer(n, nc, mla_w_dqkv[j], mla_q_norm[j], mla_kv_norm[j], mla_w_uq[j], mla_w_ukv[j],
                              mla_w_o[j], cos, sin, not last)
        else:
            y = fourier_mixer(n, fnet_w_out[j])
            yc = None if last else fourier_mixer(nc, fnet_w_out[j])
        h = h + m[:, :, 1, 2] * y
        if not last:
            hc = hc + mc[:, :, 1, 2] * yc
        hs = jnp.concatenate([h.reshape(n_lat, D), hc.reshape(n_ctx, D)], axis=0)
        hs = ffn_sublayer(hs, mods, norm_g3, w2_gu, w2_down, final_g2, layer=i, s=2,
                          rows=n_lat if last else rows_all, tm=tm, tf=tf, final=last)
    return hs.reshape(B, T, D)
```

```python
import functools

import jax
import jax.numpy as jnp
from jax import lax
from jax.experimental import pallas as pl
from jax.experimental.pallas import tpu as pltpu

D_MODEL = 2048
SEQ = 4096
DEPTH = 4
GRID_W = 64
CTX_LEN = 256
N_MIXERS = 3
N_SUB = 3
D_FF = 5632
RMS_EPS = 1e-6

HG_HEAD_DIM = 128
HG_HEADS = D_MODEL // HG_HEAD_DIM
HG_QF = HG_HEADS * HG_HEAD_DIM
HG_IV = HG_HEADS * HG_HEAD_DIM
HG_CHUNK = 16

MLA_HEADS = 16
MLA_Q_RANK = 512
MLA_KV_RANK = 512
MLA_NOPE = 128
MLA_ROPE = 64
MLA_V = 128
MLA_SCALE = (MLA_NOPE + MLA_ROPE) ** -0.5
ATTN_BLOCK = 128
ROPE_THETA = 10000.0
ROPE_FREQS = MLA_ROPE // 4

FOURIER_GROUPS = 8

BF16 = jnp.bfloat16
F32 = jnp.float32

VMEM_LIMIT_BYTES = 56 * 1024 * 1024
MOD_ROWS = 8
N_MOD = N_SUB * 3


def _mod_row(t, tm):
    return jnp.minimum(t // (SEQ // tm), 2)


def _mod_kernel(c_ref, w_ref, b_ref, o_ref):
    c = c_ref[...]
    a = (c * jax.nn.sigmoid(c)).astype(BF16)
    o_ref[...] = jnp.dot(a, w_ref[...].astype(BF16), preferred_element_type=F32) + b_ref[...]


def modulation(cc, mod_w, mod_b, *, tn=1024):
    depth, d, n = mod_w.shape
    return pl.pallas_call(
        _mod_kernel,
        grid=(depth, n // tn),
        in_specs=[
            pl.BlockSpec((MOD_ROWS, d), lambda i, j: (0, 0)),
            pl.BlockSpec((None, d, tn), lambda i, j: (i, 0, j)),
            pl.BlockSpec((None, 1, tn), lambda i, j: (i, 0, j)),
        ],
        out_specs=pl.BlockSpec((None, MOD_ROWS, tn), lambda i, j: (i, 0, j)),
        out_shape=jax.ShapeDtypeStruct((depth, MOD_ROWS, n), F32),
        compiler_params=pltpu.CompilerParams(
            dimension_semantics=("arbitrary", "arbitrary"), vmem_limit_bytes=VMEM_LIMIT_BYTES),
        name="modulation",
    )(cc, mod_w, mod_b.reshape(depth, 1, n))


def _adanorm(x, g, shift, scale):
    y = x * lax.rsqrt(jnp.mean(x * x, axis=-1, keepdims=True) + RMS_EPS)
    return (y * g) * (1.0 + scale) + shift


def _ffn_kernel(x_ref, mod_ref, g_ref, wg_ref, wu_ref, wd_ref, fg_ref, o_ref, n_ref, *, s, nf, final):
    f = pl.program_id(1)

    @pl.when(f == 0)
    def _():
        n = _adanorm(x_ref[...], g_ref[...], mod_ref[3 * s:3 * s + 1, :], mod_ref[3 * s + 1:3 * s + 2, :])
        n_ref[...] = n.astype(BF16)

    n = n_ref[...]
    gate = jnp.dot(n, wg_ref[...], preferred_element_type=F32)
    up = jnp.dot(n, wu_ref[...], preferred_element_type=F32)
    act = (gate * jax.nn.sigmoid(gate) * up).astype(BF16)
    part = jnp.dot(act, wd_ref[...], preferred_element_type=F32)

    @pl.when(f == 0)
    def _():
        o_ref[...] = part

    @pl.when(f > 0)
    def _():
        o_ref[...] += part

    @pl.when(f == nf - 1)
    def _():
        h = x_ref[...] + (0.5 * mod_ref[3 * s + 2:3 * s + 3, :]) * o_ref[...]
        if final:
            h = h * lax.rsqrt(jnp.mean(h * h, axis=-1, keepdims=True) + RMS_EPS) * fg_ref[...]
        o_ref[...] = h


def ffn_sublayer(h, mods, g, w_gu, w_down, final_g, *, layer, s, rows, tm, tf, final=False):
    d = h.shape[1]
    nf = D_FF // tf
    kern = functools.partial(_ffn_kernel, s=s, nf=nf, final=final)
    return pl.pallas_call(
        kern,
        grid=(rows // tm, nf),
        in_specs=[
            pl.BlockSpec((tm, d), lambda t, f: (t, 0)),
            pl.BlockSpec((None, None, N_MOD, d), lambda t, f: (layer, _mod_row(t, tm), 0, 0)),
            pl.BlockSpec((None, 1, d), lambda t, f: (layer * N_SUB + s, 0, 0)),
            pl.BlockSpec((None, d, tf), lambda t, f: (layer, 0, f)),
            pl.BlockSpec((None, d, tf), lambda t, f: (layer, 0, f + nf)),
            pl.BlockSpec((None, tf, d), lambda t, f: (layer, f, 0)),
            pl.BlockSpec((1, d), lambda t, f: (0, 0)),
        ],
        out_specs=pl.BlockSpec((tm, d), lambda t, f: (t, 0)),
        out_shape=jax.ShapeDtypeStruct((rows if final else h.shape[0], d), F32),
        scratch_shapes=[pltpu.VMEM((tm, d), BF16)],
        compiler_params=pltpu.CompilerParams(
            dimension_semantics=("arbitrary", "arbitrary"), vmem_limit_bytes=VMEM_LIMIT_BYTES),
        name=f"ffn_l{layer}_s{s}",
    )(h, mods, g, w_gu, w_gu, w_down, final_g)


TMX = 256
S_ALL = SEQ + CTX_LEN
LAT_TILES = SEQ // TMX
CTX_BLOCK = SEQ // TMX


def _resident(shape):
    return pl.BlockSpec(shape, lambda *_: (0,) * len(shape), pipeline_mode=pl.Buffered(1))


def _stream_tiles(batch):
    return batch * (LAT_TILES + CTX_LEN // TMX)


def _tile_batch(t, batch):
    lat = t < batch * LAT_TILES
    return jnp.where(lat, t // LAT_TILES, t - batch * LAT_TILES)


def _tile_block(t, batch):
    return jnp.where(t < batch * LAT_TILES, t % LAT_TILES, CTX_BLOCK)


def _tile_mod_row(t, batch):
    return jnp.where(t < batch * LAT_TILES, t // LAT_TILES, batch)


def _rms(x):
    return x * lax.rsqrt(jnp.mean(x * x, axis=-1, keepdims=True) + RMS_EPS)


MLA_QK = 2 * MLA_NOPE
N_DQ = MLA_Q_RANK + MLA_KV_RANK


def _mla_proj_kernel(x_ref, mod_ref, g_ref, wd_ref, qn_ref, kvn_ref, wuq_ref, wukv_ref, cos_ref, sin_ref,
                     q_ref, k_ref, v_ref):
    n = _adanorm(x_ref[...], g_ref[...], mod_ref[3:4, :], mod_ref[4:5, :]).astype(BF16)
    proj = jnp.dot(n, wd_ref[...], preferred_element_type=F32)
    cq = (_rms(proj[:, :MLA_Q_RANK]) * qn_ref[...]).astype(BF16)
    ckv = (_rms(proj[:, MLA_Q_RANK:N_DQ]) * kvn_ref[...]).astype(BF16)
    cos, sin = cos_ref[...], sin_ref[...]
    kr = proj[:, N_DQ:N_DQ + 128] * cos + proj[:, N_DQ + 128:N_DQ + 256] * sin
    kr = kr.astype(BF16)
    q = jnp.dot(cq, wuq_ref[...], preferred_element_type=F32) * MLA_SCALE
    kv = jnp.dot(ckv, wukv_ref[...], preferred_element_type=F32)
    hn = MLA_HEADS * MLA_NOPE
    for h in range(MLA_HEADS):
        lo = h * 128
        q_ref[h, :, 0:128] = q[:, lo:lo + 128].astype(BF16)
        qr = q[:, hn + lo:hn + lo + 128] * cos + q[:, 2 * hn + lo:2 * hn + lo + 128] * sin
        q_ref[h, :, 128:256] = qr.astype(BF16)
        k_ref[h, :, 0:128] = kv[:, lo:lo + 128].astype(BF16)
        k_ref[h, :, 128:256] = kr
        v_ref[h, :, :] = kv[:, hn + lo:hn + lo + 128].astype(BF16)


def _rope_partner(width):
    idx = jnp.arange(width)
    return idx ^ ROPE_FREQS


def mla_weights(w_dqkv, w_uq, w_ukv):
    d = w_dqkv.shape[0]
    z = jnp.zeros((d, 128 - MLA_ROPE), w_dqkv.dtype)
    kr = w_dqkv[:, N_DQ:]
    wd = jnp.concatenate([w_dqkv[:, :N_DQ], kr, z, kr[:, _rope_partner(MLA_ROPE)], z], axis=1)
    wq = w_uq.reshape(MLA_Q_RANK, MLA_HEADS, MLA_NOPE + MLA_ROPE)
    qr = wq[:, :, MLA_NOPE:]
    zq = jnp.zeros((MLA_Q_RANK, MLA_HEADS, 128 - MLA_ROPE), w_uq.dtype)
    wuq = jnp.concatenate([
        wq[:, :, :MLA_NOPE].reshape(MLA_Q_RANK, -1),
        jnp.concatenate([qr, zq], axis=2).reshape(MLA_Q_RANK, -1),
        jnp.concatenate([qr[:, :, _rope_partner(MLA_ROPE)], zq], axis=2).reshape(MLA_Q_RANK, -1)], axis=1)
    wkv = w_ukv.reshape(MLA_KV_RANK, MLA_HEADS, MLA_NOPE + MLA_V)
    wukv = jnp.concatenate([wkv[:, :, :MLA_NOPE].reshape(MLA_KV_RANK, -1),
                            wkv[:, :, MLA_NOPE:].reshape(MLA_KV_RANK, -1)], axis=1)
    return wd.astype(BF16), wuq.astype(BF16), wukv.astype(BF16)


def mla_rope_tables(n_tokens):
    rows = n_tokens // GRID_W
    r = jnp.broadcast_to(jnp.arange(rows, dtype=F32)[:, None], (rows, GRID_W)).reshape(-1)
    col = jnp.broadcast_to(jnp.arange(GRID_W, dtype=F32)[None, :], (rows, GRID_W)).reshape(-1)
    inv_freq = ROPE_THETA ** (-jnp.arange(ROPE_FREQS, dtype=F32) / ROPE_FREQS)
    ang = jnp.stack([r, col], axis=-1)[..., None] * inv_freq
    cos = jnp.broadcast_to(jnp.cos(ang)[:, :, None, :], (n_tokens, 2, 2, ROPE_FREQS)).reshape(n_tokens, MLA_ROPE)
    sin = jnp.sin(ang)
    sin = jnp.stack([-sin, sin], axis=2).reshape(n_tokens, MLA_ROPE)
    pad = jnp.zeros((n_tokens, 128 - MLA_ROPE), F32)
    cos = jnp.concatenate([cos, pad], axis=1)
    sin = jnp.concatenate([sin, pad], axis=1)
    ctx_cos = jnp.concatenate([jnp.ones((CTX_LEN, MLA_ROPE), F32), jnp.zeros((CTX_LEN, 128 - MLA_ROPE), F32)], axis=1)
    return (jnp.concatenate([cos, ctx_cos], axis=0),
            jnp.concatenate([sin, jnp.zeros((CTX_LEN, 128), F32)], axis=0))


def mla_project(hs, mods, norm_g3, wd, q_norm, kv_norm, wuq, wukv, cos, sin, *, layer, batch):
    d = hs.shape[1]
    bmap = lambda t: (_tile_batch(t, batch), 0, _tile_block(t, batch), 0)
    return pl.pallas_call(
        _mla_proj_kernel,
        grid=(_stream_tiles(batch),),
        in_specs=[
            pl.BlockSpec((TMX, d), lambda t: (t, 0)),
            pl.BlockSpec((None, None, N_MOD, d), lambda t: (layer, _tile_mod_row(t, batch), 0, 0)),
            pl.BlockSpec((None, 1, d), lambda t: (layer * N_SUB + 1, 0, 0)),
            _resident(wd.shape),
            _resident((1, MLA_Q_RANK)),
            _resident((1, MLA_KV_RANK)),
            _resident(wuq.shape),
            _resident(wukv.shape),
            pl.BlockSpec((TMX, 128), lambda t: (_tile_block(t, batch), 0)),
            pl.BlockSpec((TMX, 128), lambda t: (_tile_block(t, batch), 0)),
        ],
        out_specs=[
            pl.BlockSpec((None, MLA_HEADS, TMX, MLA_QK), bmap),
            pl.BlockSpec((None, MLA_HEADS, TMX, MLA_QK), bmap),
            pl.BlockSpec((None, MLA_HEADS, TMX, MLA_V), bmap),
        ],
        out_shape=[
            jax.ShapeDtypeStruct((batch, MLA_HEADS, S_ALL, MLA_QK), BF16),
            jax.ShapeDtypeStruct((batch, MLA_HEADS, S_ALL, MLA_QK), BF16),
            jax.ShapeDtypeStruct((batch, MLA_HEADS, S_ALL, MLA_V), BF16),
        ],
        compiler_params=pltpu.CompilerParams(
            dimension_semantics=("arbitrary",), vmem_limit_bytes=VMEM_LIMIT_BYTES),
        name=f"mla_proj_l{layer}",
    )(hs, mods, norm_g3, wd, q_norm, kv_norm, wuq, wukv, cos, sin)


def _attn_kernel(q_ref, k_ref, v_ref, *rest):
    o_ref = rest[-1]
    s = lax.dot_general(q_ref[...], k_ref[...], (((1,), (1,)), ((), ())), preferred_element_type=F32)
    p = jnp.exp(s - jnp.max(s, axis=-1, keepdims=True))
    l = jnp.sum(p, axis=-1, keepdims=True)
    o = jnp.dot(p.astype(BF16), v_ref[...], preferred_element_type=F32)
    o_ref[...] = (o / l).astype(BF16)


def mla_attention(q, k, v, *, tq, q_block0, n_q, k_rows, k_block, prev=None, name):
    batch, heads, s_all, _ = q.shape
    in_specs = [
        pl.BlockSpec((None, None, tq, MLA_QK), lambda b, h, i: (b, h, q_block0 + i, 0)),
        pl.BlockSpec((None, None, k_rows, MLA_QK), lambda b, h, i: (b, h, k_block, 0)),
        pl.BlockSpec((None, None, k_rows, MLA_V), lambda b, h, i: (b, h, k_block, 0)),
    ]
    args = [q, k, v]
    aliases = {}
    if prev is not None:
        in_specs.append(pl.BlockSpec(memory_space=pl.ANY))
        args.append(prev)
        aliases = {3: 0}
    return pl.pallas_call(
        _attn_kernel,
        grid=(batch, heads, n_q),
        in_specs=in_specs,
        out_specs=pl.BlockSpec((None, tq, MLA_V), lambda b, h, i: (b, q_block0 + i, h)),
        out_shape=jax.ShapeDtypeStruct((batch, s_all, heads * MLA_V), BF16),
        input_output_aliases=aliases,
        compiler_params=pltpu.CompilerParams(
            dimension_semantics=("arbitrary", "arbitrary", "arbitrary"), vmem_limit_bytes=VMEM_LIMIT_BYTES),
        name=name,
    )(*args)


def _out_proj_kernel(y_ref, w_ref, x_ref, mod_ref, o_ref):
    y = jnp.dot(y_ref[...], w_ref[...], preferred_element_type=F32)
    o_ref[...] = x_ref[...] + mod_ref[5:6, :] * y


def mixer_out_proj(y, w, hs, mods, *, layer, batch, n_tiles, name):
    d = hs.shape[1]
    kdim = y.shape[2]
    return pl.pallas_call(
        _out_proj_kernel,
        grid=(n_tiles,),
        in_specs=[
            pl.BlockSpec((None, TMX, kdim), lambda t: (_tile_batch(t, batch), _tile_block(t, batch), 0)),
            _resident(w.shape),
            pl.BlockSpec((TMX, d), lambda t: (t, 0)),
            pl.BlockSpec((None, None, N_MOD, d), lambda t: (layer, _tile_mod_row(t, batch), 0, 0)),
        ],
        out_specs=pl.BlockSpec((TMX, d), lambda t: (t, 0)),
        out_shape=jax.ShapeDtypeStruct(hs.shape, F32),
        input_output_aliases={2: 0},
        compiler_params=pltpu.CompilerParams(
            dimension_semantics=("arbitrary",), vmem_limit_bytes=VMEM_LIMIT_BYTES),
        name=name,
    )(y, w, hs, mods)


def mla_layer(hs, mods, norm_g3, weights, q_norm, kv_norm, w_o, cos, sin, *, layer, batch, with_ctx_out):
    wd, wuq, wukv = weights
    q, k, v = mla_project(hs, mods, norm_g3, wd, q_norm, kv_norm, wuq, wukv, cos, sin, layer=layer, batch=batch)
    tq = 512
    att = mla_attention(q, k, v, tq=tq, q_block0=0, n_q=SEQ // tq, k_rows=S_ALL, k_block=0,
                        name=f"mla_attn_lat_l{layer}")
    n_tiles = batch * LAT_TILES
    if with_ctx_out:
        att = mla_attention(q, k, v, tq=CTX_LEN, q_block0=SEQ // CTX_LEN, n_q=1, k_rows=CTX_LEN,
                            k_block=SEQ // CTX_LEN, prev=att, name=f"mla_attn_ctx_l{layer}")
        n_tiles = _stream_tiles(batch)
    return mixer_out_proj(att, w_o, hs, mods, layer=layer, batch=batch, n_tiles=n_tiles, name=f"mla_out_l{layer}")


def rmsnorm(x, g):
    xf = x.astype(jnp.float32)
    y = xf * lax.rsqrt(jnp.mean(xf * xf, axis=-1, keepdims=True) + RMS_EPS)
    return (y * g.astype(jnp.float32)).astype(x.dtype)


def adanorm(z, g, shift, scale):
    return rmsnorm(z, g) * (1 + scale) + shift


def axial_rope_tables(n_tokens):
    rows = n_tokens // GRID_W
    r = jnp.broadcast_to(jnp.arange(rows, dtype=jnp.float32)[:, None], (rows, GRID_W)).reshape(-1)
    col = jnp.broadcast_to(jnp.arange(GRID_W, dtype=jnp.float32)[None, :], (rows, GRID_W)).reshape(-1)
    inv_freq = ROPE_THETA ** (-jnp.arange(ROPE_FREQS, dtype=jnp.float32) / ROPE_FREQS)
    ang = jnp.stack([r, col], axis=-1)[..., None] * inv_freq
    return jnp.cos(ang), jnp.sin(ang)


def axial_rope(x, cos, sin):
    xs = x.reshape(x.shape[:-1] + (2, 2, ROPE_FREQS))
    xa, xb = xs[..., 0, :], xs[..., 1, :]
    cos, sin = cos.astype(x.dtype), sin.astype(x.dtype)
    return jnp.stack([xa * cos - xb * sin, xb * cos + xa * sin], axis=-2).reshape(x.shape)


def hgrn2_chunk_scan(q, k, v, log_f, s0):
    B, T, H, _ = q.shape
    n = T // HG_CHUNK

    def chunks(a):
        return a.reshape(B, n, HG_CHUNK, H, a.shape[-1]).transpose(1, 0, 3, 2, 4)

    q, k, v, log_f = chunks(q), chunks(k), chunks(v), chunks(log_f)
    b = jnp.cumsum(log_f, axis=-2)
    b_end = b[..., -1:, :]
    q_dec = q * jnp.exp(b)
    k_to_end = k * jnp.exp(b_end - b)
    lower_tri = jnp.tril(jnp.ones((HG_CHUNK, HG_CHUNK), dtype=bool))
    a = jnp.einsum('nbhck,nbhsk->nbhcs', q_dec, k * jnp.exp(-b))
    o_intra = jnp.einsum('nbhcs,nbhsv->nbhcv', jnp.where(lower_tri, a, 0.0), v)
    decay = jnp.exp(b_end[..., 0, :])

    def step(s, inp):
        qd, kd, vn, dn = inp
        o_inter = jnp.einsum('bhck,bhkv->bhcv', qd, s)
        s = s * dn[..., None] + jnp.einsum('bhck,bhcv->bhkv', kd, vn)
        return s, o_inter

    s_final, o_inter = lax.scan(step, s0, (q_dec, k_to_end, v, decay))
    o = (o_intra + o_inter).transpose(1, 0, 3, 2, 4).reshape(B, T, H, v.shape[-1])
    return o, s_final


def hgrn2_mixer(h_lat, h_ctx, w_in, lb_fwd, lb_bwd, g_norm, w_out, with_ctx_out):
    splits = [HG_QF, 2 * HG_QF, 3 * HG_QF, 3 * HG_QF + HG_IV]

    def branch_inputs(h):
        B, T, _ = h.shape
        q, zf, zb, i, gate = jnp.split(h @ w_in, splits, axis=-1)
        heads = lambda a: a.astype(jnp.float32).reshape(B, T, HG_HEADS, HG_HEAD_DIM)
        q, i = heads(jax.nn.silu(q)), heads(i)
        dirs = []
        for z, lb in ((zf, lb_fwd), (zb, lb_bwd)):
            z = heads(z)
            lb = lb.reshape(HG_HEADS, HG_HEAD_DIM)
            log_f = jnp.logaddexp(jnp.log(lb), jnp.log1p(-lb) + jax.nn.log_sigmoid(z))
            k = (1.0 - lb) * jax.nn.sigmoid(-z)
            dirs.append((log_f, k))
        return q, i, gate, dirs

    def readout(o, gate, dtype):
        B, T = o.shape[:2]
        o = rmsnorm(o, g_norm.reshape(HG_HEADS, HG_HEAD_DIM)).reshape(B, T, HG_IV).astype(dtype)
        return (o * jax.nn.silu(gate)) @ w_out

    flip = lambda a: a[:, ::-1]
    qc, ic, gc, (fc, bc) = branch_inputs(h_ctx)
    ql, il, gl, (fl, bl) = branch_inputs(h_lat)
    s_zero = jnp.zeros((h_lat.shape[0], HG_HEADS, HG_HEAD_DIM, HG_HEAD_DIM), jnp.float32)
    o_cf, s_f = hgrn2_chunk_scan(qc, fc[1], ic, fc[0], s_zero)
    o_cb, s_b = hgrn2_chunk_scan(flip(qc), flip(bc[1]), flip(ic), flip(bc[0]), s_zero)
    o_lf, _ = hgrn2_chunk_scan(ql, fl[1], il, fl[0], s_f)
    o_lb, _ = hgrn2_chunk_scan(flip(ql), flip(bl[1]), flip(il), flip(bl[0]), s_b)
    y_lat = readout(o_lf + flip(o_lb), gl, h_lat.dtype)
    y_ctx = readout(o_cf + flip(o_cb), gc, h_ctx.dtype) if with_ctx_out else None
    return y_lat, y_ctx


def mla_mixer(h_lat, h_ctx, w_dqkv, q_norm, kv_norm, w_uq, w_ukv, w_o, cos, sin, with_ctx_out):
    def project(h, rotate):
        B, T, _ = h.shape
        cq, ckv, k_rope = jnp.split(h @ w_dqkv, [MLA_Q_RANK, MLA_Q_RANK + MLA_KV_RANK], axis=-1)
        q = (rmsnorm(cq, q_norm) @ w_uq).reshape(B, T, MLA_HEADS, MLA_NOPE + MLA_ROPE)
        kv = (rmsnorm(ckv, kv_norm) @ w_ukv).reshape(B, T, MLA_HEADS, MLA_NOPE + MLA_V)
        q_nope, q_rope = q[..., :MLA_NOPE], q[..., MLA_NOPE:]
        k_nope, v = kv[..., :MLA_NOPE], kv[..., MLA_NOPE:]
        if rotate:
            q_rope = axial_rope(q_rope, cos[:, None], sin[:, None])
            k_rope = axial_rope(k_rope, cos, sin)
        return q_nope, q_rope, k_nope, k_rope, v

    def attend(q_nope, q_rope, k_nope, k_rope, v):
        s = (jnp.einsum('bqhd,bkhd->bhqk', q_nope, k_nope, preferred_element_type=jnp.float32)
             + jnp.einsum('bqhr,bkr->bhqk', q_rope, k_rope, preferred_element_type=jnp.float32)) * MLA_SCALE
        p = jax.nn.softmax(s, axis=-1).astype(v.dtype)
        return jnp.einsum('bhqk,bkhd->bqhd', p, v)

    qn_c, qr_c, kn_c, kr_c, v_c = project(h_ctx, False)
    qn_l, qr_l, kn_l, kr_l, v_l = project(h_lat, True)
    kn = jnp.concatenate([kn_c, kn_l], axis=1)
    kr = jnp.concatenate([kr_c, kr_l], axis=1)
    vv = jnp.concatenate([v_c, v_l], axis=1)
    B, T = h_lat.shape[:2]
    nb = T // ATTN_BLOCK
    to_blocks = lambda a: jnp.moveaxis(a.reshape((B, nb, ATTN_BLOCK) + a.shape[2:]), 1, 0)
    o_blk = lax.map(lambda qb: attend(qb[0], qb[1], kn, kr, vv), (to_blocks(qn_l), to_blocks(qr_l)))
    y_lat = jnp.moveaxis(o_blk, 0, 1).reshape(B, T, MLA_HEADS * MLA_V) @ w_o
    y_ctx = None
    if with_ctx_out:
        y_ctx = attend(qn_c, qr_c, kn_c, kr_c, v_c).reshape(B, h_ctx.shape[1], MLA_HEADS * MLA_V) @ w_o
    return y_lat, y_ctx


def fourier_mixer(h, w_out):
    B, T, D = h.shape
    hg = h.astype(jnp.float32).reshape(B, T, FOURIER_GROUPS, D // FOURIER_GROUPS)
    y = jnp.fft.fft2(hg, axes=(1, 3), norm='ortho').real
    return y.reshape(B, T, D).astype(h.dtype) @ w_out


def _jax_mixer_layer(hs, mods, norm_g, mixer, *, layer, batch, ctx_in, last):
    n_lat, n_ctx = batch * SEQ, batch * CTX_LEN
    d = hs.shape[1]
    m = mods[layer, :batch].reshape(batch, 1, N_SUB, 3, d)
    mc = mods[layer, batch].reshape(1, 1, N_SUB, 3, d)
    h = hs[:n_lat].reshape(batch, SEQ, d)
    hc = hs[n_lat:].reshape(batch, CTX_LEN, d)
    n = adanorm(h, norm_g[layer, 1], m[:, :, 1, 0], m[:, :, 1, 1])
    nc = adanorm(hc, norm_g[layer, 1], mc[:, :, 1, 0], mc[:, :, 1, 1]) if ctx_in else None
    y, yc = mixer(n, nc)
    h = h + m[:, :, 1, 2] * y
    if not last:
        hc = hc + mc[:, :, 1, 2] * yc
    return jnp.concatenate([h.reshape(n_lat, d), hc.reshape(n_ctx, d)], axis=0)


def kernel(x, c, ctx, c_ctx, mod_w, mod_b, norm_g, ffn1_w_gu, ffn1_w_down, ffn2_w_gu, ffn2_w_down,
           hgrn_w_in, hgrn_lb_logits, hgrn_g_norm, hgrn_w_out,
           mla_w_dqkv, mla_q_norm, mla_kv_norm, mla_w_uq, mla_w_ukv, mla_w_o,
           fnet_w_out, final_g):
    B, T, D = x.shape
    n_lat = B * T
    n_ctx = B * CTX_LEN
    tm, tf = 512, 512
    rows_all = n_lat + n_ctx

    lb = jnp.cumsum(jax.nn.softmax(hgrn_lb_logits.astype(jnp.float32), axis=1), axis=1)
    lb = lb - lb[:, :1]
    rope_cos, rope_sin = mla_rope_tables(T)

    cc = jnp.concatenate([c, c_ctx[None, :], jnp.zeros((MOD_ROWS - B - 1, D), F32)], axis=0)
    mods = modulation(cc, mod_w, mod_b).reshape(DEPTH, MOD_ROWS, N_MOD, D)

    w1_gu, w1_down = ffn1_w_gu.astype(BF16), ffn1_w_down.astype(BF16)
    w2_gu, w2_down = ffn2_w_gu.astype(BF16), ffn2_w_down.astype(BF16)
    norm_g3 = norm_g.reshape(DEPTH * N_SUB, 1, D)
    final_g2 = final_g.reshape(1, D)

    hs = jnp.concatenate([x.reshape(n_lat, D), ctx.reshape(n_ctx, D)], axis=0)
    for i in range(DEPTH):
        kind, j = i % N_MIXERS, i // N_MIXERS
        last = i == DEPTH - 1
        ctx_in = not (last and kind == 2)
        hs = ffn_sublayer(hs, mods, norm_g3, w1_gu, w1_down, final_g2, layer=i, s=0,
                          rows=rows_all if ctx_in else n_lat, tm=tm, tf=tf)
        if kind == 0:
            mixer = lambda n, nc: hgrn2_mixer(n, nc, hgrn_w_in[j], lb[0, j], lb[1, j], hgrn_g_norm[j],
                                              hgrn_w_out[j], not last)
            hs = _jax_mixer_layer(hs, mods, norm_g, mixer, layer=i, batch=B, ctx_in=ctx_in, last=last)
        elif kind == 1:
            hs = mla_layer(hs, mods, norm_g3, mla_weights(mla_w_dqkv[j], mla_w_uq[j], mla_w_ukv[j]),
                           mla_q_norm[j].reshape(1, -1), mla_kv_norm[j].reshape(1, -1),
                           mla_w_o[j].astype(BF16), rope_cos, rope_sin, layer=i, batch=B, with_ctx_out=not last)
        else:
            mixer = lambda n, nc: (fourier_mixer(n, fnet_w_out[j]),
                                   None if last else fourier_mixer(nc, fnet_w_out[j]))
            hs = _jax_mixer_layer(hs, mods, norm_g, mixer, layer=i, batch=B, ctx_in=ctx_in, last=last)
        hs = ffn_sublayer(hs, mods, norm_g3, w2_gu, w2_down, final_g2, layer=i, s=2,
                          rows=n_lat if last else rows_all, tm=tm, tf=tf, final=last)
    return hs.reshape(B, T, D)
```

```python
import functools

import jax
import jax.numpy as jnp
from jax import lax
from jax.experimental import pallas as pl
from jax.experimental.pallas import tpu as pltpu

D_MODEL = 2048
SEQ = 4096
DEPTH = 4
GRID_W = 64
CTX_LEN = 256
N_MIXERS = 3
N_SUB = 3
D_FF = 5632
RMS_EPS = 1e-6

HG_HEAD_DIM = 128
HG_HEADS = D_MODEL // HG_HEAD_DIM
HG_QF = HG_HEADS * HG_HEAD_DIM
HG_IV = HG_HEADS * HG_HEAD_DIM
HG_CHUNK = 16

MLA_HEADS = 16
MLA_Q_RANK = 512
MLA_KV_RANK = 512
MLA_NOPE = 128
MLA_ROPE = 64
MLA_V = 128
MLA_SCALE = (MLA_NOPE + MLA_ROPE) ** -0.5
ATTN_BLOCK = 128
ROPE_THETA = 10000.0
ROPE_FREQS = MLA_ROPE // 4

FOURIER_GROUPS = 8

BF16 = jnp.bfloat16
F32 = jnp.float32

VMEM_LIMIT_BYTES = 56 * 1024 * 1024
MOD_ROWS = 8
N_MOD = N_SUB * 3


def _mod_row(t, tm):
    return jnp.minimum(t // (SEQ // tm), 2)


def _mod_kernel(c_ref, w_ref, b_ref, o_ref):
    c = c_ref[...]
    a = (c * jax.nn.sigmoid(c)).astype(BF16)
    o_ref[...] = jnp.dot(a, w_ref[...].astype(BF16), preferred_element_type=F32) + b_ref[...]


def modulation(cc, mod_w, mod_b, *, tn=1024):
    depth, d, n = mod_w.shape
    return pl.pallas_call(
        _mod_kernel,
        grid=(depth, n // tn),
        in_specs=[
            pl.BlockSpec((MOD_ROWS, d), lambda i, j: (0, 0)),
            pl.BlockSpec((None, d, tn), lambda i, j: (i, 0, j)),
            pl.BlockSpec((None, 1, tn), lambda i, j: (i, 0, j)),
        ],
        out_specs=pl.BlockSpec((None, MOD_ROWS, tn), lambda i, j: (i, 0, j)),
        out_shape=jax.ShapeDtypeStruct((depth, MOD_ROWS, n), F32),
        compiler_params=pltpu.CompilerParams(
            dimension_semantics=("arbitrary", "arbitrary"), vmem_limit_bytes=VMEM_LIMIT_BYTES),
        name="modulation",
    )(cc, mod_w, mod_b.reshape(depth, 1, n))


def _adanorm(x, g, shift, scale):
    y = x * lax.rsqrt(jnp.mean(x * x, axis=-1, keepdims=True) + RMS_EPS)
    return (y * g) * (1.0 + scale) + shift


def _ffn_kernel(x_ref, mod_ref, g_ref, wg_ref, wu_ref, wd_ref, fg_ref, o_ref, n_ref, *, s, nf, final):
    f = pl.program_id(1)

    @pl.when(f == 0)
    def _():
        n = _adanorm(x_ref[...], g_ref[...], mod_ref[3 * s:3 * s + 1, :], mod_ref[3 * s + 1:3 * s + 2, :])
        n_ref[...] = n.astype(BF16)

    n = n_ref[...]
    gate = jnp.dot(n, wg_ref[...], preferred_element_type=F32)
    up = jnp.dot(n, wu_ref[...], preferred_element_type=F32)
    act = (gate * jax.nn.sigmoid(gate) * up).astype(BF16)
    part = jnp.dot(act, wd_ref[...], preferred_element_type=F32)

    @pl.when(f == 0)
    def _():
        o_ref[...] = part

    @pl.when(f > 0)
    def _():
        o_ref[...] += part

    @pl.when(f == nf - 1)
    def _():
        h = x_ref[...] + (0.5 * mod_ref[3 * s + 2:3 * s + 3, :]) * o_ref[...]
        if final:
            h = h * lax.rsqrt(jnp.mean(h * h, axis=-1, keepdims=True) + RMS_EPS) * fg_ref[...]
        o_ref[...] = h


def ffn_sublayer(h, mods, g, w_gu, w_down, final_g, *, layer, s, rows, tm, tf, final=False):
    d = h.shape[1]
    nf = D_FF // tf
    kern = functools.partial(_ffn_kernel, s=s, nf=nf, final=final)
    return pl.pallas_call(
        kern,
        grid=(rows // tm, nf),
        in_specs=[
            pl.BlockSpec((tm, d), lambda t, f: (t, 0)),
            pl.BlockSpec((None, None, N_MOD, d), lambda t, f: (layer, _mod_row(t, tm), 0, 0)),
            pl.BlockSpec((None, 1, d), lambda t, f: (layer * N_SUB + s, 0, 0)),
            pl.BlockSpec((None, d, tf), lambda t, f: (layer, 0, f)),
            pl.BlockSpec((None, d, tf), lambda t, f: (layer, 0, f + nf)),
            pl.BlockSpec((None, tf, d), lambda t, f: (layer, f, 0)),
            pl.BlockSpec((1, d), lambda t, f: (0, 0)),
        ],
        out_specs=pl.BlockSpec((tm, d), lambda t, f: (t, 0)),
        out_shape=jax.ShapeDtypeStruct((rows if final else h.shape[0], d), F32),
        scratch_shapes=[pltpu.VMEM((tm, d), BF16)],
        compiler_params=pltpu.CompilerParams(
            dimension_semantics=("arbitrary", "arbitrary"), vmem_limit_bytes=VMEM_LIMIT_BYTES),
        name=f"ffn_l{layer}_s{s}",
    )(h, mods, g, w_gu, w_gu, w_down, final_g)


TMX = 256
S_ALL = SEQ + CTX_LEN
LAT_TILES = SEQ // TMX
CTX_BLOCK = SEQ // TMX


def _resident(shape):
    return pl.BlockSpec(shape, lambda *_: (0,) * len(shape), pipeline_mode=pl.Buffered(1))


def _stream_tiles(batch):
    return batch * (LAT_TILES + CTX_LEN // TMX)


def _tile_batch(t, batch):
    lat = t < batch * LAT_TILES
    return jnp.where(lat, t // LAT_TILES, t - batch * LAT_TILES)


def _tile_block(t, batch):
    return jnp.where(t < batch * LAT_TILES, t % LAT_TILES, CTX_BLOCK)


def _tile_mod_row(t, batch):
    return jnp.where(t < batch * LAT_TILES, t // LAT_TILES, batch)


def _rms(x):
    return x * lax.rsqrt(jnp.mean(x * x, axis=-1, keepdims=True) + RMS_EPS)


MLA_QK = 2 * MLA_NOPE
N_DQ = MLA_Q_RANK + MLA_KV_RANK


def _mla_proj_kernel(x_ref, mod_ref, g_ref, wd_ref, qn_ref, kvn_ref, wuq_ref, wukv_ref, cos_ref, sin_ref,
                     q_ref, k_ref, v_ref):
    n = _adanorm(x_ref[...], g_ref[...], mod_ref[3:4, :], mod_ref[4:5, :]).astype(BF16)
    proj = jnp.dot(n, wd_ref[...], preferred_element_type=F32)
    cq = (_rms(proj[:, :MLA_Q_RANK]) * qn_ref[...]).astype(BF16)
    ckv = (_rms(proj[:, MLA_Q_RANK:N_DQ]) * kvn_ref[...]).astype(BF16)
    cos, sin = cos_ref[...], sin_ref[...]
    kr = proj[:, N_DQ:N_DQ + 128] * cos + proj[:, N_DQ + 128:N_DQ + 256] * sin
    kr = kr.astype(BF16)
    q = jnp.dot(cq, wuq_ref[...], preferred_element_type=F32) * MLA_SCALE
    kv = jnp.dot(ckv, wukv_ref[...], preferred_element_type=F32)
    hn = MLA_HEADS * MLA_NOPE
    for h in range(MLA_HEADS):
        lo = h * 128
        q_ref[h, :, 0:128] = q[:, lo:lo + 128].astype(BF16)
        qr = q[:, hn + lo:hn + lo + 128] * cos + q[:, 2 * hn + lo:2 * hn + lo + 128] * sin
        q_ref[h, :, 128:256] = qr.astype(BF16)
        k_ref[h, :, 0:128] = kv[:, lo:lo + 128].astype(BF16)
        k_ref[h, :, 128:256] = kr
        v_ref[h, :, :] = kv[:, hn + lo:hn + lo + 128].astype(BF16)


def _rope_partner(width):
    idx = jnp.arange(width)
    return idx ^ ROPE_FREQS


def mla_weights(w_dqkv, w_uq, w_ukv):
    d = w_dqkv.shape[0]
    z = jnp.zeros((d, 128 - MLA_ROPE), w_dqkv.dtype)
    kr = w_dqkv[:, N_DQ:]
    wd = jnp.concatenate([w_dqkv[:, :N_DQ], kr, z, kr[:, _rope_partner(MLA_ROPE)], z], axis=1)
    wq = w_uq.reshape(MLA_Q_RANK, MLA_HEADS, MLA_NOPE + MLA_ROPE)
    qr = wq[:, :, MLA_NOPE:]
    zq = jnp.zeros((MLA_Q_RANK, MLA_HEADS, 128 - MLA_ROPE), w_uq.dtype)
    wuq = jnp.concatenate([
        wq[:, :, :MLA_NOPE].reshape(MLA_Q_RANK, -1),
        jnp.concatenate([qr, zq], axis=2).reshape(MLA_Q_RANK, -1),
        jnp.concatenate([qr[:, :, _rope_partner(MLA_ROPE)], zq], axis=2).reshape(MLA_Q_RANK, -1)], axis=1)
    wkv = w_ukv.reshape(MLA_KV_RANK, MLA_HEADS, MLA_NOPE + MLA_V)
    wukv = jnp.concatenate([wkv[:, :, :MLA_NOPE].reshape(MLA_KV_RANK, -1),
                            wkv[:, :, MLA_NOPE:].reshape(MLA_KV_RANK, -1)], axis=1)
    return wd.astype(BF16), wuq.astype(BF16), wukv.astype(BF16)


def mla_rope_tables(n_tokens):
    rows = n_tokens // GRID_W
    r = jnp.broadcast_to(jnp.arange(rows, dtype=F32)[:, None], (rows, GRID_W)).reshape(-1)
    col = jnp.broadcast_to(jnp.arange(GRID_W, dtype=F32)[None, :], (rows, GRID_W)).reshape(-1)
    inv_freq = ROPE_THETA ** (-jnp.arange(ROPE_FREQS, dtype=F32) / ROPE_FREQS)
    ang = jnp.stack([r, col], axis=-1)[..., None] * inv_freq
    cos = jnp.broadcast_to(jnp.cos(ang)[:, :, None, :], (n_tokens, 2, 2, ROPE_FREQS)).reshape(n_tokens, MLA_ROPE)
    sin = jnp.sin(ang)
    sin = jnp.stack([-sin, sin], axis=2).reshape(n_tokens, MLA_ROPE)
    pad = jnp.zeros((n_tokens, 128 - MLA_ROPE), F32)
    cos = jnp.concatenate([cos, pad], axis=1)
    sin = jnp.concatenate([sin, pad], axis=1)
    ctx_cos = jnp.concatenate([jnp.ones((CTX_LEN, MLA_ROPE), F32), jnp.zeros((CTX_LEN, 128 - MLA_ROPE), F32)], axis=1)
    return (jnp.concatenate([cos, ctx_cos], axis=0),
            jnp.concatenate([sin, jnp.zeros((CTX_LEN, 128), F32)], axis=0))


def mla_project(hs, mods, norm_g3, wd, q_norm, kv_norm, wuq, wukv, cos, sin, *, layer, batch):
    d = hs.shape[1]
    bmap = lambda t: (_tile_batch(t, batch), 0, _tile_block(t, batch), 0)
    return pl.pallas_call(
        _mla_proj_kernel,
        grid=(_stream_tiles(batch),),
        in_specs=[
            pl.BlockSpec((TMX, d), lambda t: (t, 0)),
            pl.BlockSpec((None, None, N_MOD, d), lambda t: (layer, _tile_mod_row(t, batch), 0, 0)),
            pl.BlockSpec((None, 1, d), lambda t: (layer * N_SUB + 1, 0, 0)),
            _resident(wd.shape),
            _resident((1, MLA_Q_RANK)),
            _resident((1, MLA_KV_RANK)),
            _resident(wuq.shape),
            _resident(wukv.shape),
            pl.BlockSpec((TMX, 128), lambda t: (_tile_block(t, batch), 0)),
            pl.BlockSpec((TMX, 128), lambda t: (_tile_block(t, batch), 0)),
        ],
        out_specs=[
            pl.BlockSpec((None, MLA_HEADS, TMX, MLA_QK), bmap),
            pl.BlockSpec((None, MLA_HEADS, TMX, MLA_QK), bmap),
            pl.BlockSpec((None, MLA_HEADS, TMX, MLA_V), bmap),
        ],
        out_shape=[
            jax.ShapeDtypeStruct((batch, MLA_HEADS, S_ALL, MLA_QK), BF16),
            jax.ShapeDtypeStruct((batch, MLA_HEADS, S_ALL, MLA_QK), BF16),
            jax.ShapeDtypeStruct((batch, MLA_HEADS, S_ALL, MLA_V), BF16),
        ],
        compiler_params=pltpu.CompilerParams(
            dimension_semantics=("arbitrary",), vmem_limit_bytes=VMEM_LIMIT_BYTES),
        name=f"mla_proj_l{layer}",
    )(hs, mods, norm_g3, wd, q_norm, kv_norm, wuq, wukv, cos, sin)


def _attn_kernel(q_ref, k_ref, v_ref, *rest):
    o_ref = rest[-1]
    s = lax.dot_general(q_ref[...], k_ref[...], (((1,), (1,)), ((), ())), preferred_element_type=F32)
    p = jnp.exp(s - jnp.max(s, axis=-1, keepdims=True))
    l = jnp.sum(p, axis=-1, keepdims=True)
    o = jnp.dot(p.astype(BF16), v_ref[...], preferred_element_type=F32)
    o_ref[...] = (o / l).astype(BF16)


def mla_attention(q, k, v, *, tq, q_block0, n_q, k_rows, k_block, prev=None, name):
    batch, heads, s_all, _ = q.shape
    in_specs = [
        pl.BlockSpec((None, None, tq, MLA_QK), lambda b, h, i: (b, h, q_block0 + i, 0)),
        pl.BlockSpec((None, None, k_rows, MLA_QK), lambda b, h, i: (b, h, k_block, 0)),
        pl.BlockSpec((None, None, k_rows, MLA_V), lambda b, h, i: (b, h, k_block, 0)),
    ]
    args = [q, k, v]
    aliases = {}
    if prev is not None:
        in_specs.append(pl.BlockSpec(memory_space=pl.ANY))
        args.append(prev)
        aliases = {3: 0}
    return pl.pallas_call(
        _attn_kernel,
        grid=(batch, heads, n_q),
        in_specs=in_specs,
        out_specs=pl.BlockSpec((None, tq, MLA_V), lambda b, h, i: (b, q_block0 + i, h)),
        out_shape=jax.ShapeDtypeStruct((batch, s_all, heads * MLA_V), BF16),
        input_output_aliases=aliases,
        compiler_params=pltpu.CompilerParams(
            dimension_semantics=("arbitrary", "arbitrary", "arbitrary"), vmem_limit_bytes=VMEM_LIMIT_BYTES),
        name=name,
    )(*args)


def _out_proj_kernel(y_ref, w_ref, x_ref, mod_ref, o_ref):
    y = jnp.dot(y_ref[...], w_ref[...], preferred_element_type=F32)
    o_ref[...] = x_ref[...] + mod_ref[5:6, :] * y


def mixer_out_proj(y, w, hs, mods, *, layer, batch, n_tiles, name):
    d = hs.shape[1]
    kdim = y.shape[2]
    return pl.pallas_call(
        _out_proj_kernel,
        grid=(n_tiles,),
        in_specs=[
            pl.BlockSpec((None, TMX, kdim), lambda t: (_tile_batch(t, batch), _tile_block(t, batch), 0)),
            _resident(w.shape),
            pl.BlockSpec((TMX, d), lambda t: (t, 0)),
            pl.BlockSpec((None, None, N_MOD, d), lambda t: (layer, _tile_mod_row(t, batch), 0, 0)),
        ],
        out_specs=pl.BlockSpec((TMX, d), lambda t: (t, 0)),
        out_shape=jax.ShapeDtypeStruct(hs.shape, F32),
        input_output_aliases={2: 0},
        compiler_params=pltpu.CompilerParams(
            dimension_semantics=("arbitrary",), vmem_limit_bytes=VMEM_LIMIT_BYTES),
        name=name,
    )(y, w, hs, mods)


def mla_layer(hs, mods, norm_g3, weights, q_norm, kv_norm, w_o, cos, sin, *, layer, batch, with_ctx_out):
    wd, wuq, wukv = weights
    q, k, v = mla_project(hs, mods, norm_g3, wd, q_norm, kv_norm, wuq, wukv, cos, sin, layer=layer, batch=batch)
    tq = 512
    att = mla_attention(q, k, v, tq=tq, q_block0=0, n_q=SEQ // tq, k_rows=S_ALL, k_block=0,
                        name=f"mla_attn_lat_l{layer}")
    n_tiles = batch * LAT_TILES
    if with_ctx_out:
        att = mla_attention(q, k, v, tq=CTX_LEN, q_block0=SEQ // CTX_LEN, n_q=1, k_rows=CTX_LEN,
                            k_block=SEQ // CTX_LEN, prev=att, name=f"mla_attn_ctx_l{layer}")
        n_tiles = _stream_tiles(batch)
    return mixer_out_proj(att, w_o, hs, mods, layer=layer, batch=batch, n_tiles=n_tiles, name=f"mla_out_l{layer}")


HG_WIN = 128
HG_NH = 2
N_TBLK = S_ALL // TMX


def _hgrn_proj_kernel(x_ref, mod_ref, g_ref, w_ref, o_ref):
    n = _adanorm(x_ref[...], g_ref[...], mod_ref[3:4, :], mod_ref[4:5, :]).astype(BF16)
    o_ref[...] = jnp.dot(n, w_ref[...], preferred_element_type=F32)


def hgrn_project(hs, mods, norm_g3, w_in, *, layer, batch):
    d = hs.shape[1]
    n_out = w_in.shape[1]
    return pl.pallas_call(
        _hgrn_proj_kernel,
        grid=(n_out // d, _stream_tiles(batch)),
        in_specs=[
            pl.BlockSpec((TMX, d), lambda j, t: (t, 0)),
            pl.BlockSpec((None, None, N_MOD, d), lambda j, t: (layer, _tile_mod_row(t, batch), 0, 0)),
            pl.BlockSpec((None, 1, d), lambda j, t: (layer * N_SUB + 1, 0, 0)),
            pl.BlockSpec((d, d), lambda j, t: (0, j)),
        ],
        out_specs=pl.BlockSpec((None, TMX, d), lambda j, t: (_tile_batch(t, batch), _tile_block(t, batch), j)),
        out_shape=jax.ShapeDtypeStruct((batch, S_ALL, n_out), F32),
        compiler_params=pltpu.CompilerParams(
            dimension_semantics=("arbitrary", "arbitrary"), vmem_limit_bytes=VMEM_LIMIT_BYTES),
        name=f"hgrn_proj_l{layer}",
    )(hs, mods, norm_g3, w_in)


def _chunk_cumsum(x, row_in_chunk, reverse):
    n = x.shape[0]
    sh = 1
    while sh < HG_CHUNK:
        if reverse:
            x = x + jnp.where(row_in_chunk < HG_CHUNK - sh, pltpu.roll(x, n - sh, axis=0), 0.0)
        else:
            x = x + jnp.where(row_in_chunk >= sh, pltpu.roll(x, sh, axis=0), 0.0)
        sh *= 2
    return x


def _hgrn_scan_kernel(q_ref, z_ref, v_ref, la_ref, l1_ref, om_ref, o_ref,
                      st_ref, qd_ref, kin_ref, kte_ref, vb_ref, dec_ref, *, reverse):
    ts = pl.program_id(2)

    @pl.when(ts == 0)
    def _():
        st_ref[...] = jnp.zeros_like(st_ref)

    z = z_ref[...]
    e = jnp.exp(-jnp.abs(z))
    r = 1.0 / (1.0 + e)
    sig_neg = jnp.where(z >= 0, e * r, r)
    log_sig = jnp.minimum(z, 0.0) - jnp.log1p(e)
    la = la_ref[...]
    b2 = l1_ref[...] + log_sig
    log_f = jnp.maximum(la, b2) + jnp.log1p(jnp.exp(-jnp.abs(la - b2)))
    k = om_ref[...] * sig_neg
    row = lax.broadcasted_iota(jnp.int32, z.shape, 0) % HG_CHUNK
    b_inc = _chunk_cumsum(log_f, row, reverse)
    b_rest = _chunk_cumsum(log_f, row, not reverse) - log_f
    qr = q_ref[...]
    qd_ref[...] = (qr * jax.nn.sigmoid(qr) * jnp.exp(b_inc)).astype(BF16)
    kin_ref[...] = (k * jnp.exp(-b_inc)).astype(BF16)
    kte_ref[...] = (k * jnp.exp(b_rest)).astype(BF16)
    vb_ref[...] = v_ref[...].astype(BF16)
    dec_ref[...] = jnp.exp(b_inc + b_rest)

    wi = lax.broadcasted_iota(jnp.int32, (HG_WIN, HG_WIN), 0)
    wj = lax.broadcasted_iota(jnp.int32, (HG_WIN, HG_WIN), 1)
    same_chunk = (wi // HG_CHUNK) == (wj // HG_CHUNK)
    causal = same_chunk & ((wj >= wi) if reverse else (wj <= wi))
    tok = (lax.broadcasted_iota(jnp.int32, (HG_WIN, HG_HEAD_DIM), 0) // HG_CHUNK).astype(F32).astype(BF16)

    n_win = TMX // HG_WIN
    per_win = HG_WIN // HG_CHUNK
    wins = range(n_win - 1, -1, -1) if reverse else range(n_win)
    chunks = range(per_win - 1, -1, -1) if reverse else range(per_win)
    for w in wins:
        rows = slice(w * HG_WIN, (w + 1) * HG_WIN)
        for h in range(HG_NH):
            cols = slice(h * HG_HEAD_DIM, (h + 1) * HG_HEAD_DIM)
            qd_w, kin_w, kte_w, v_w = qd_ref[rows, cols], kin_ref[rows, cols], kte_ref[rows, cols], vb_ref[rows, cols]
            a = lax.dot_general(qd_w, kin_w, (((1,), (1,)), ((), ())), preferred_element_type=F32)
            a = jnp.where(causal, a, 0.0).astype(BF16)
            o_intra = jnp.dot(a, v_w, preferred_element_type=F32)
            v_t = v_ref[rows, cols].T.astype(BF16)
            for c in chunks:
                crow = slice(c * HG_CHUNK, (c + 1) * HG_CHUNK)
                grow = slice(w * HG_WIN + c * HG_CHUNK, w * HG_WIN + (c + 1) * HG_CHUNK)
                st = st_ref[h]
                o_inter = lax.dot_general(qd_ref[grow, cols], st.astype(BF16), (((1,), (1,)), ((), ())),
                                          preferred_element_type=F32)
                o_ref[grow, cols] = o_intra[crow, :] + o_inter
                kte_c = jnp.where(tok == c, kte_w, jnp.zeros_like(kte_w))
                upd = jnp.dot(v_t, kte_c, preferred_element_type=F32)
                dec = dec_ref[w * HG_WIN + c * HG_CHUNK:w * HG_WIN + c * HG_CHUNK + 1, cols]
                st_ref[h] = st * dec + upd


def hgrn_scan(p, la, l1, om, *, reverse, z_col, layer):
    batch, s_all, n_out = p.shape
    d = n_out // 5
    wcols = HG_NH * HG_HEAD_DIM
    groups = d // wcols

    def tblk(ts):
        lat = (N_TBLK - 1 - ts) if reverse else (ts - 1)
        return jnp.where(ts == 0, CTX_BLOCK, lat)

    col = lambda base: (lambda b, g, ts: (b, tblk(ts), base * groups + g))
    row_spec = pl.BlockSpec((1, wcols), lambda b, g, ts: (0, g))
    kern = functools.partial(_hgrn_scan_kernel, reverse=reverse)
    return pl.pallas_call(
        kern,
        grid=(batch, groups, N_TBLK),
        in_specs=[
            pl.BlockSpec((None, TMX, wcols), col(0)),
            pl.BlockSpec((None, TMX, wcols), col(z_col)),
            pl.BlockSpec((None, TMX, wcols), col(3)),
            row_spec, row_spec, row_spec,
        ],
        out_specs=pl.BlockSpec((None, TMX, wcols), lambda b, g, ts: (b, tblk(ts), g)),
        out_shape=jax.ShapeDtypeStruct((batch, s_all, d), F32),
        scratch_shapes=[
            pltpu.VMEM((HG_NH, HG_HEAD_DIM, HG_HEAD_DIM), F32),
            pltpu.VMEM((TMX, wcols), BF16), pltpu.VMEM((TMX, wcols), BF16),
            pltpu.VMEM((TMX, wcols), BF16), pltpu.VMEM((TMX, wcols), BF16),
            pltpu.VMEM((TMX, wcols), F32),
        ],
        compiler_params=pltpu.CompilerParams(
            dimension_semantics=("arbitrary", "arbitrary", "arbitrary"), vmem_limit_bytes=VMEM_LIMIT_BYTES),
        name=f"hgrn_scan_{'bwd' if reverse else 'fwd'}_l{layer}",
    )(p, p, p, la, l1, om)


def _hgrn_readout_kernel(of_ref, ob_ref, gate_ref, gn_ref, w_ref, x_ref, mod_ref, o_ref):
    o = of_ref[...] + ob_ref[...]
    gate = gate_ref[...]
    gs = gate * jax.nn.sigmoid(gate)
    parts = []
    for h in range(HG_HEADS):
        cols = slice(h * HG_HEAD_DIM, (h + 1) * HG_HEAD_DIM)
        parts.append((_rms(o[:, cols]) * gn_ref[:, cols] * gs[:, cols]).astype(BF16))
    y = jnp.dot(jnp.concatenate(parts, axis=1), w_ref[...], preferred_element_type=F32)
    o_ref[...] = x_ref[...] + mod_ref[5:6, :] * y


def hgrn_readout(o_f, o_b, p, g_norm, w_out, hs, mods, *, layer, batch, n_tiles):
    d = hs.shape[1]
    seq = lambda t: (_tile_batch(t, batch), _tile_block(t, batch), 0)
    return pl.pallas_call(
        _hgrn_readout_kernel,
        grid=(n_tiles,),
        in_specs=[
            pl.BlockSpec((None, TMX, d), seq),
            pl.BlockSpec((None, TMX, d), seq),
            pl.BlockSpec((None, TMX, d), lambda t: (_tile_batch(t, batch), _tile_block(t, batch), 4)),
            _resident((1, d)),
            _resident(w_out.shape),
            pl.BlockSpec((TMX, d), lambda t: (t, 0)),
            pl.BlockSpec((None, None, N_MOD, d), lambda t: (layer, _tile_mod_row(t, batch), 0, 0)),
        ],
        out_specs=pl.BlockSpec((TMX, d), lambda t: (t, 0)),
        out_shape=jax.ShapeDtypeStruct(hs.shape, F32),
        input_output_aliases={5: 0},
        compiler_params=pltpu.CompilerParams(
            dimension_semantics=("arbitrary",), vmem_limit_bytes=VMEM_LIMIT_BYTES),
        name=f"hgrn_out_l{layer}",
    )(o_f, o_b, p, g_norm, w_out, hs, mods)


def hgrn_layer(hs, mods, norm_g3, w_in, lb_fwd, lb_bwd, g_norm, w_out, *, layer, batch, with_ctx_out):
    p = hgrn_project(hs, mods, norm_g3, w_in, layer=layer, batch=batch)
    outs = []
    for reverse, lb, z_col in ((False, lb_fwd, 1), (True, lb_bwd, 2)):
        lb = lb.reshape(1, -1)
        outs.append(hgrn_scan(p, jnp.log(lb), jnp.log1p(-lb), 1.0 - lb, reverse=reverse, z_col=z_col, layer=layer))
    n_tiles = _stream_tiles(batch) if with_ctx_out else batch * LAT_TILES
    return hgrn_readout(outs[0], outs[1], p, g_norm.reshape(1, -1), w_out, hs, mods,
                        layer=layer, batch=batch, n_tiles=n_tiles)


def rmsnorm(x, g):
    xf = x.astype(jnp.float32)
    y = xf * lax.rsqrt(jnp.mean(xf * xf, axis=-1, keepdims=True) + RMS_EPS)
    return (y * g.astype(jnp.float32)).astype(x.dtype)


def adanorm(z, g, shift, scale):
    return rmsnorm(z, g) * (1 + scale) + shift


def axial_rope_tables(n_tokens):
    rows = n_tokens // GRID_W
    r = jnp.broadcast_to(jnp.arange(rows, dtype=jnp.float32)[:, None], (rows, GRID_W)).reshape(-1)
    col = jnp.broadcast_to(jnp.arange(GRID_W, dtype=jnp.float32)[None, :], (rows, GRID_W)).reshape(-1)
    inv_freq = ROPE_THETA ** (-jnp.arange(ROPE_FREQS, dtype=jnp.float32) / ROPE_FREQS)
    ang = jnp.stack([r, col], axis=-1)[..., None] * inv_freq
    return jnp.cos(ang), jnp.sin(ang)


def axial_rope(x, cos, sin):
    xs = x.reshape(x.shape[:-1] + (2, 2, ROPE_FREQS))
    xa, xb = xs[..., 0, :], xs[..., 1, :]
    cos, sin = cos.astype(x.dtype), sin.astype(x.dtype)
    return jnp.stack([xa * cos - xb * sin, xb * cos + xa * sin], axis=-2).reshape(x.shape)


def hgrn2_chunk_scan(q, k, v, log_f, s0):
    B, T, H, _ = q.shape
    n = T // HG_CHUNK

    def chunks(a):
        return a.reshape(B, n, HG_CHUNK, H, a.shape[-1]).transpose(1, 0, 3, 2, 4)

    q, k, v, log_f = chunks(q), chunks(k), chunks(v), chunks(log_f)
    b = jnp.cumsum(log_f, axis=-2)
    b_end = b[..., -1:, :]
    q_dec = q * jnp.exp(b)
    k_to_end = k * jnp.exp(b_end - b)
    lower_tri = jnp.tril(jnp.ones((HG_CHUNK, HG_CHUNK), dtype=bool))
    a = jnp.einsum('nbhck,nbhsk->nbhcs', q_dec, k * jnp.exp(-b))
    o_intra = jnp.einsum('nbhcs,nbhsv->nbhcv', jnp.where(lower_tri, a, 0.0), v)
    decay = jnp.exp(b_end[..., 0, :])

    def step(s, inp):
        qd, kd, vn, dn = inp
        o_inter = jnp.einsum('bhck,bhkv->bhcv', qd, s)
        s = s * dn[..., None] + jnp.einsum('bhck,bhcv->bhkv', kd, vn)
        return s, o_inter

    s_final, o_inter = lax.scan(step, s0, (q_dec, k_to_end, v, decay))
    o = (o_intra + o_inter).transpose(1, 0, 3, 2, 4).reshape(B, T, H, v.shape[-1])
    return o, s_final


def hgrn2_mixer(h_lat, h_ctx, w_in, lb_fwd, lb_bwd, g_norm, w_out, with_ctx_out):
    splits = [HG_QF, 2 * HG_QF, 3 * HG_QF, 3 * HG_QF + HG_IV]

    def branch_inputs(h):
        B, T, _ = h.shape
        q, zf, zb, i, gate = jnp.split(h @ w_in, splits, axis=-1)
        heads = lambda a: a.astype(jnp.float32).reshape(B, T, HG_HEADS, HG_HEAD_DIM)
        q, i = heads(jax.nn.silu(q)), heads(i)
        dirs = []
        for z, lb in ((zf, lb_fwd), (zb, lb_bwd)):
            z = heads(z)
            lb = lb.reshape(HG_HEADS, HG_HEAD_DIM)
            log_f = jnp.logaddexp(jnp.log(lb), jnp.log1p(-lb) + jax.nn.log_sigmoid(z))
            k = (1.0 - lb) * jax.nn.sigmoid(-z)
            dirs.append((log_f, k))
        return q, i, gate, dirs

    def readout(o, gate, dtype):
        B, T = o.shape[:2]
        o = rmsnorm(o, g_norm.reshape(HG_HEADS, HG_HEAD_DIM)).reshape(B, T, HG_IV).astype(dtype)
        return (o * jax.nn.silu(gate)) @ w_out

    flip = lambda a: a[:, ::-1]
    qc, ic, gc, (fc, bc) = branch_inputs(h_ctx)
    ql, il, gl, (fl, bl) = branch_inputs(h_lat)
    s_zero = jnp.zeros((h_lat.shape[0], HG_HEADS, HG_HEAD_DIM, HG_HEAD_DIM), jnp.float32)
    o_cf, s_f = hgrn2_chunk_scan(qc, fc[1], ic, fc[0], s_zero)
    o_cb, s_b = hgrn2_chunk_scan(flip(qc), flip(bc[1]), flip(ic), flip(bc[0]), s_zero)
    o_lf, _ = hgrn2_chunk_scan(ql, fl[1], il, fl[0], s_f)
    o_lb, _ = hgrn2_chunk_scan(flip(ql), flip(bl[1]), flip(il), flip(bl[0]), s_b)
    y_lat = readout(o_lf + flip(o_lb), gl, h_lat.dtype)
    y_ctx = readout(o_cf + flip(o_cb), gc, h_ctx.dtype) if with_ctx_out else None
    return y_lat, y_ctx


def mla_mixer(h_lat, h_ctx, w_dqkv, q_norm, kv_norm, w_uq, w_ukv, w_o, cos, sin, with_ctx_out):
    def project(h, rotate):
        B, T, _ = h.shape
        cq, ckv, k_rope = jnp.split(h @ w_dqkv, [MLA_Q_RANK, MLA_Q_RANK + MLA_KV_RANK], axis=-1)
        q = (rmsnorm(cq, q_norm) @ w_uq).reshape(B, T, MLA_HEADS, MLA_NOPE + MLA_ROPE)
        kv = (rmsnorm(ckv, kv_norm) @ w_ukv).reshape(B, T, MLA_HEADS, MLA_NOPE + MLA_V)
        q_nope, q_rope = q[..., :MLA_NOPE], q[..., MLA_NOPE:]
        k_nope, v = kv[..., :MLA_NOPE], kv[..., MLA_NOPE:]
        if rotate:
            q_rope = axial_rope(q_rope, cos[:, None], sin[:, None])
            k_rope = axial_rope(k_rope, cos, sin)
        return q_nope, q_rope, k_nope, k_rope, v

    def attend(q_nope, q_rope, k_nope, k_rope, v):
        s = (jnp.einsum('bqhd,bkhd->bhqk', q_nope, k_nope, preferred_element_type=jnp.float32)
             + jnp.einsum('bqhr,bkr->bhqk', q_rope, k_rope, preferred_element_type=jnp.float32)) * MLA_SCALE
        p = jax.nn.softmax(s, axis=-1).astype(v.dtype)
        return jnp.einsum('bhqk,bkhd->bqhd', p, v)

    qn_c, qr_c, kn_c, kr_c, v_c = project(h_ctx, False)
    qn_l, qr_l, kn_l, kr_l, v_l = project(h_lat, True)
    kn = jnp.concatenate([kn_c, kn_l], axis=1)
    kr = jnp.concatenate([kr_c, kr_l], axis=1)
    vv = jnp.concatenate([v_c, v_l], axis=1)
    B, T = h_lat.shape[:2]
    nb = T // ATTN_BLOCK
    to_blocks = lambda a: jnp.moveaxis(a.reshape((B, nb, ATTN_BLOCK) + a.shape[2:]), 1, 0)
    o_blk = lax.map(lambda qb: attend(qb[0], qb[1], kn, kr, vv), (to_blocks(qn_l), to_blocks(qr_l)))
    y_lat = jnp.moveaxis(o_blk, 0, 1).reshape(B, T, MLA_HEADS * MLA_V) @ w_o
    y_ctx = None
    if with_ctx_out:
        y_ctx = attend(qn_c, qr_c, kn_c, kr_c, v_c).reshape(B, h_ctx.shape[1], MLA_HEADS * MLA_V) @ w_o
    return y_lat, y_ctx


def fourier_mixer(h, w_out):
    B, T, D = h.shape
    hg = h.astype(jnp.float32).reshape(B, T, FOURIER_GROUPS, D // FOURIER_GROUPS)
    y = jnp.fft.fft2(hg, axes=(1, 3), norm='ortho').real
    return y.reshape(B, T, D).astype(h.dtype) @ w_out


def _jax_mixer_layer(hs, mods, norm_g, mixer, *, layer, batch, ctx_in, last):
    n_lat, n_ctx = batch * SEQ, batch * CTX_LEN
    d = hs.shape[1]
    m = mods[layer, :batch].reshape(batch, 1, N_SUB, 3, d)
    mc = mods[layer, batch].reshape(1, 1, N_SUB, 3, d)
    h = hs[:n_lat].reshape(batch, SEQ, d)
    hc = hs[n_lat:].reshape(batch, CTX_LEN, d)
    n = adanorm(h, norm_g[layer, 1], m[:, :, 1, 0], m[:, :, 1, 1])
    nc = adanorm(hc, norm_g[layer, 1], mc[:, :, 1, 0], mc[:, :, 1, 1]) if ctx_in else None
    y, yc = mixer(n, nc)
    h = h + m[:, :, 1, 2] * y
    if not last:
        hc = hc + mc[:, :, 1, 2] * yc
    return jnp.concatenate([h.reshape(n_lat, d), hc.reshape(n_ctx, d)], axis=0)


def kernel(x, c, ctx, c_ctx, mod_w, mod_b, norm_g, ffn1_w_gu, ffn1_w_down, ffn2_w_gu, ffn2_w_down,
           hgrn_w_in, hgrn_lb_logits, hgrn_g_norm, hgrn_w_out,
           mla_w_dqkv, mla_q_norm, mla_kv_norm, mla_w_uq, mla_w_ukv, mla_w_o,
           fnet_w_out, final_g):
    B, T, D = x.shape
    n_lat = B * T
    n_ctx = B * CTX_LEN
    tm, tf = 512, 512
    rows_all = n_lat + n_ctx

    lb = jnp.cumsum(jax.nn.softmax(hgrn_lb_logits.astype(jnp.float32), axis=1), axis=1)
    lb = lb - lb[:, :1]
    rope_cos, rope_sin = mla_rope_tables(T)

    cc = jnp.concatenate([c, c_ctx[None, :], jnp.zeros((MOD_ROWS - B - 1, D), F32)], axis=0)
    mods = modulation(cc, mod_w, mod_b).reshape(DEPTH, MOD_ROWS, N_MOD, D)

    w1_gu, w1_down = ffn1_w_gu.astype(BF16), ffn1_w_down.astype(BF16)
    w2_gu, w2_down = ffn2_w_gu.astype(BF16), ffn2_w_down.astype(BF16)
    norm_g3 = norm_g.reshape(DEPTH * N_SUB, 1, D)
    final_g2 = final_g.reshape(1, D)

    hs = jnp.concatenate([x.reshape(n_lat, D), ctx.reshape(n_ctx, D)], axis=0)
    for i in range(DEPTH):
        kind, j = i % N_MIXERS, i // N_MIXERS
        last = i == DEPTH - 1
        ctx_in = not (last and kind == 2)
        hs = ffn_sublayer(hs, mods, norm_g3, w1_gu, w1_down, final_g2, layer=i, s=0,
                          rows=rows_all if ctx_in else n_lat, tm=tm, tf=tf)
        if kind == 0:
            hs = hgrn_layer(hs, mods, norm_g3, hgrn_w_in[j].astype(BF16), lb[0, j], lb[1, j], hgrn_g_norm[j],
                            hgrn_w_out[j].astype(BF16), layer=i, batch=B, with_ctx_out=not last)
        elif kind == 1:
            hs = mla_layer(hs, mods, norm_g3, mla_weights(mla_w_dqkv[j], mla_w_uq[j], mla_w_ukv[j]),
                           mla_q_norm[j].reshape(1, -1), mla_kv_norm[j].reshape(1, -1),
                           mla_w_o[j].astype(BF16), rope_cos, rope_sin, layer=i, batch=B, with_ctx_out=not last)
        else:
            mixer = lambda n, nc: (fourier_mixer(n, fnet_w_out[j]),
                                   None if last else fourier_mixer(nc, fnet_w_out[j]))
            hs = _jax_mixer_layer(hs, mods, norm_g, mixer, layer=i, batch=B, ctx_in=ctx_in, last=last)
        hs = ffn_sublayer(hs, mods, norm_g3, w2_gu, w2_down, final_g2, layer=i, s=2,
                          rows=n_lat if last else rows_all, tm=tm, tf=tf, final=last)
    return hs.reshape(B, T, D)
```

```python
import functools

import numpy as np
import jax
import jax.numpy as jnp
from jax import lax
from jax.experimental import pallas as pl
from jax.experimental.pallas import tpu as pltpu

D_MODEL = 2048
SEQ = 4096
DEPTH = 4
GRID_W = 64
CTX_LEN = 256
N_MIXERS = 3
N_SUB = 3
D_FF = 5632
RMS_EPS = 1e-6

HG_HEAD_DIM = 128
HG_HEADS = D_MODEL // HG_HEAD_DIM
HG_QF = HG_HEADS * HG_HEAD_DIM
HG_IV = HG_HEADS * HG_HEAD_DIM
HG_CHUNK = 16

MLA_HEADS = 16
MLA_Q_RANK = 512
MLA_KV_RANK = 512
MLA_NOPE = 128
MLA_ROPE = 64
MLA_V = 128
MLA_SCALE = (MLA_NOPE + MLA_ROPE) ** -0.5
ATTN_BLOCK = 128
ROPE_THETA = 10000.0
ROPE_FREQS = MLA_ROPE // 4

FOURIER_GROUPS = 8

BF16 = jnp.bfloat16
F32 = jnp.float32

VMEM_LIMIT_BYTES = 56 * 1024 * 1024
MOD_ROWS = 8
N_MOD = N_SUB * 3


def _mod_row(t, tm):
    return jnp.minimum(t // (SEQ // tm), 2)


def _mod_kernel(c_ref, w_ref, b_ref, o_ref):
    c = c_ref[...]
    a = (c * jax.nn.sigmoid(c)).astype(BF16)
    o_ref[...] = jnp.dot(a, w_ref[...].astype(BF16), preferred_element_type=F32) + b_ref[...]


def modulation(cc, mod_w, mod_b, *, tn=1024):
    depth, d, n = mod_w.shape
    return pl.pallas_call(
        _mod_kernel,
        grid=(depth, n // tn),
        in_specs=[
            pl.BlockSpec((MOD_ROWS, d), lambda i, j: (0, 0)),
            pl.BlockSpec((None, d, tn), lambda i, j: (i, 0, j)),
            pl.BlockSpec((None, 1, tn), lambda i, j: (i, 0, j)),
        ],
        out_specs=pl.BlockSpec((None, MOD_ROWS, tn), lambda i, j: (i, 0, j)),
        out_shape=jax.ShapeDtypeStruct((depth, MOD_ROWS, n), F32),
        compiler_params=pltpu.CompilerParams(
            dimension_semantics=("arbitrary", "arbitrary"), vmem_limit_bytes=VMEM_LIMIT_BYTES),
        name="modulation",
    )(cc, mod_w, mod_b.reshape(depth, 1, n))


def _adanorm(x, g, shift, scale):
    y = x * lax.rsqrt(jnp.mean(x * x, axis=-1, keepdims=True) + RMS_EPS)
    return (y * g) * (1.0 + scale) + shift


def _ffn_kernel(x_ref, mod_ref, g_ref, wg_ref, wu_ref, wd_ref, fg_ref, o_ref, n_ref, *, s, nf, final):
    f = pl.program_id(1)

    @pl.when(f == 0)
    def _():
        n = _adanorm(x_ref[...], g_ref[...], mod_ref[3 * s:3 * s + 1, :], mod_ref[3 * s + 1:3 * s + 2, :])
        n_ref[...] = n.astype(BF16)

    n = n_ref[...]
    gate = jnp.dot(n, wg_ref[...], preferred_element_type=F32)
    up = jnp.dot(n, wu_ref[...], preferred_element_type=F32)
    act = (gate * jax.nn.sigmoid(gate) * up).astype(BF16)
    part = jnp.dot(act, wd_ref[...], preferred_element_type=F32)

    @pl.when(f == 0)
    def _():
        o_ref[...] = part

    @pl.when(f > 0)
    def _():
        o_ref[...] += part

    @pl.when(f == nf - 1)
    def _():
        h = x_ref[...] + (0.5 * mod_ref[3 * s + 2:3 * s + 3, :]) * o_ref[...]
        if final:
            h = h * lax.rsqrt(jnp.mean(h * h, axis=-1, keepdims=True) + RMS_EPS) * fg_ref[...]
        o_ref[...] = h


def ffn_sublayer(h, mods, g, w_gu, w_down, final_g, *, layer, s, rows, tm, tf, final=False):
    d = h.shape[1]
    nf = D_FF // tf
    kern = functools.partial(_ffn_kernel, s=s, nf=nf, final=final)
    return pl.pallas_call(
        kern,
        grid=(rows // tm, nf),
        in_specs=[
            pl.BlockSpec((tm, d), lambda t, f: (t, 0)),
            pl.BlockSpec((None, None, N_MOD, d), lambda t, f: (layer, _mod_row(t, tm), 0, 0)),
            pl.BlockSpec((None, 1, d), lambda t, f: (layer * N_SUB + s, 0, 0)),
            pl.BlockSpec((None, d, tf), lambda t, f: (layer, 0, f)),
            pl.BlockSpec((None, d, tf), lambda t, f: (layer, 0, f + nf)),
            pl.BlockSpec((None, tf, d), lambda t, f: (layer, f, 0)),
            pl.BlockSpec((1, d), lambda t, f: (0, 0)),
        ],
        out_specs=pl.BlockSpec((tm, d), lambda t, f: (t, 0)),
        out_shape=jax.ShapeDtypeStruct((rows if final else h.shape[0], d), F32),
        scratch_shapes=[pltpu.VMEM((tm, d), BF16)],
        compiler_params=pltpu.CompilerParams(
            dimension_semantics=("arbitrary", "arbitrary"), vmem_limit_bytes=VMEM_LIMIT_BYTES),
        name=f"ffn_l{layer}_s{s}",
    )(h, mods, g, w_gu, w_gu, w_down, final_g)


TMX = 256
S_ALL = SEQ + CTX_LEN
LAT_TILES = SEQ // TMX
CTX_BLOCK = SEQ // TMX


def _resident(shape):
    return pl.BlockSpec(shape, lambda *_: (0,) * len(shape), pipeline_mode=pl.Buffered(1))


def _stream_tiles(batch):
    return batch * (LAT_TILES + CTX_LEN // TMX)


def _tile_batch(t, batch):
    lat = t < batch * LAT_TILES
    return jnp.where(lat, t // LAT_TILES, t - batch * LAT_TILES)


def _tile_block(t, batch):
    return jnp.where(t < batch * LAT_TILES, t % LAT_TILES, CTX_BLOCK)


def _tile_mod_row(t, batch):
    return jnp.where(t < batch * LAT_TILES, t // LAT_TILES, batch)


def _rms(x):
    return x * lax.rsqrt(jnp.mean(x * x, axis=-1, keepdims=True) + RMS_EPS)


MLA_QK = 2 * MLA_NOPE
N_DQ = MLA_Q_RANK + MLA_KV_RANK


def _mla_proj_kernel(x_ref, mod_ref, g_ref, wd_ref, qn_ref, kvn_ref, wuq_ref, wukv_ref, cos_ref, sin_ref,
                     q_ref, k_ref, v_ref):
    n = _adanorm(x_ref[...], g_ref[...], mod_ref[3:4, :], mod_ref[4:5, :]).astype(BF16)
    proj = jnp.dot(n, wd_ref[...], preferred_element_type=F32)
    cq = (_rms(proj[:, :MLA_Q_RANK]) * qn_ref[...]).astype(BF16)
    ckv = (_rms(proj[:, MLA_Q_RANK:N_DQ]) * kvn_ref[...]).astype(BF16)
    cos, sin = cos_ref[...], sin_ref[...]
    kr = proj[:, N_DQ:N_DQ + 128] * cos + proj[:, N_DQ + 128:N_DQ + 256] * sin
    kr = kr.astype(BF16)
    q = jnp.dot(cq, wuq_ref[...], preferred_element_type=F32) * MLA_SCALE
    kv = jnp.dot(ckv, wukv_ref[...], preferred_element_type=F32)
    hn = MLA_HEADS * MLA_NOPE
    for h in range(MLA_HEADS):
        lo = h * 128
        q_ref[h, :, 0:128] = q[:, lo:lo + 128].astype(BF16)
        qr = q[:, hn + lo:hn + lo + 128] * cos + q[:, 2 * hn + lo:2 * hn + lo + 128] * sin
        q_ref[h, :, 128:256] = qr.astype(BF16)
        k_ref[h, :, 0:128] = kv[:, lo:lo + 128].astype(BF16)
        k_ref[h, :, 128:256] = kr
        v_ref[h, :, :] = kv[:, hn + lo:hn + lo + 128].astype(BF16)


def _rope_partner(width):
    idx = jnp.arange(width)
    return idx ^ ROPE_FREQS


def mla_weights(w_dqkv, w_uq, w_ukv):
    d = w_dqkv.shape[0]
    z = jnp.zeros((d, 128 - MLA_ROPE), w_dqkv.dtype)
    kr = w_dqkv[:, N_DQ:]
    wd = jnp.concatenate([w_dqkv[:, :N_DQ], kr, z, kr[:, _rope_partner(MLA_ROPE)], z], axis=1)
    wq = w_uq.reshape(MLA_Q_RANK, MLA_HEADS, MLA_NOPE + MLA_ROPE)
    qr = wq[:, :, MLA_NOPE:]
    zq = jnp.zeros((MLA_Q_RANK, MLA_HEADS, 128 - MLA_ROPE), w_uq.dtype)
    wuq = jnp.concatenate([
        wq[:, :, :MLA_NOPE].reshape(MLA_Q_RANK, -1),
        jnp.concatenate([qr, zq], axis=2).reshape(MLA_Q_RANK, -1),
        jnp.concatenate([qr[:, :, _rope_partner(MLA_ROPE)], zq], axis=2).reshape(MLA_Q_RANK, -1)], axis=1)
    wkv = w_ukv.reshape(MLA_KV_RANK, MLA_HEADS, MLA_NOPE + MLA_V)
    wukv = jnp.concatenate([wkv[:, :, :MLA_NOPE].reshape(MLA_KV_RANK, -1),
                            wkv[:, :, MLA_NOPE:].reshape(MLA_KV_RANK, -1)], axis=1)
    return wd.astype(BF16), wuq.astype(BF16), wukv.astype(BF16)


def mla_rope_tables(n_tokens):
    rows = n_tokens // GRID_W
    r = jnp.broadcast_to(jnp.arange(rows, dtype=F32)[:, None], (rows, GRID_W)).reshape(-1)
    col = jnp.broadcast_to(jnp.arange(GRID_W, dtype=F32)[None, :], (rows, GRID_W)).reshape(-1)
    inv_freq = ROPE_THETA ** (-jnp.arange(ROPE_FREQS, dtype=F32) / ROPE_FREQS)
    ang = jnp.stack([r, col], axis=-1)[..., None] * inv_freq
    cos = jnp.broadcast_to(jnp.cos(ang)[:, :, None, :], (n_tokens, 2, 2, ROPE_FREQS)).reshape(n_tokens, MLA_ROPE)
    sin = jnp.sin(ang)
    sin = jnp.stack([-sin, sin], axis=2).reshape(n_tokens, MLA_ROPE)
    pad = jnp.zeros((n_tokens, 128 - MLA_ROPE), F32)
    cos = jnp.concatenate([cos, pad], axis=1)
    sin = jnp.concatenate([sin, pad], axis=1)
    ctx_cos = jnp.concatenate([jnp.ones((CTX_LEN, MLA_ROPE), F32), jnp.zeros((CTX_LEN, 128 - MLA_ROPE), F32)], axis=1)
    return (jnp.concatenate([cos, ctx_cos], axis=0),
            jnp.concatenate([sin, jnp.zeros((CTX_LEN, 128), F32)], axis=0))


def mla_project(hs, mods, norm_g3, wd, q_norm, kv_norm, wuq, wukv, cos, sin, *, layer, batch):
    d = hs.shape[1]
    bmap = lambda t: (_tile_batch(t, batch), 0, _tile_block(t, batch), 0)
    return pl.pallas_call(
        _mla_proj_kernel,
        grid=(_stream_tiles(batch),),
        in_specs=[
            pl.BlockSpec((TMX, d), lambda t: (t, 0)),
            pl.BlockSpec((None, None, N_MOD, d), lambda t: (layer, _tile_mod_row(t, batch), 0, 0)),
            pl.BlockSpec((None, 1, d), lambda t: (layer * N_SUB + 1, 0, 0)),
            _resident(wd.shape),
            _resident((1, MLA_Q_RANK)),
            _resident((1, MLA_KV_RANK)),
            _resident(wuq.shape),
            _resident(wukv.shape),
            pl.BlockSpec((TMX, 128), lambda t: (_tile_block(t, batch), 0)),
            pl.BlockSpec((TMX, 128), lambda t: (_tile_block(t, batch), 0)),
        ],
        out_specs=[
            pl.BlockSpec((None, MLA_HEADS, TMX, MLA_QK), bmap),
            pl.BlockSpec((None, MLA_HEADS, TMX, MLA_QK), bmap),
            pl.BlockSpec((None, MLA_HEADS, TMX, MLA_V), bmap),
        ],
        out_shape=[
            jax.ShapeDtypeStruct((batch, MLA_HEADS, S_ALL, MLA_QK), BF16),
            jax.ShapeDtypeStruct((batch, MLA_HEADS, S_ALL, MLA_QK), BF16),
            jax.ShapeDtypeStruct((batch, MLA_HEADS, S_ALL, MLA_V), BF16),
        ],
        compiler_params=pltpu.CompilerParams(
            dimension_semantics=("arbitrary",), vmem_limit_bytes=VMEM_LIMIT_BYTES),
        name=f"mla_proj_l{layer}",
    )(hs, mods, norm_g3, wd, q_norm, kv_norm, wuq, wukv, cos, sin)


def _attn_kernel(q_ref, k_ref, v_ref, *rest):
    o_ref = rest[-1]
    s = lax.dot_general(q_ref[...], k_ref[...], (((1,), (1,)), ((), ())), preferred_element_type=F32)
    p = jnp.exp(s - jnp.max(s, axis=-1, keepdims=True))
    l = jnp.sum(p, axis=-1, keepdims=True)
    o = jnp.dot(p.astype(BF16), v_ref[...], preferred_element_type=F32)
    o_ref[...] = (o / l).astype(BF16)


def mla_attention(q, k, v, *, tq, q_block0, n_q, k_rows, k_block, prev=None, name):
    batch, heads, s_all, _ = q.shape
    in_specs = [
        pl.BlockSpec((None, None, tq, MLA_QK), lambda b, h, i: (b, h, q_block0 + i, 0)),
        pl.BlockSpec((None, None, k_rows, MLA_QK), lambda b, h, i: (b, h, k_block, 0)),
        pl.BlockSpec((None, None, k_rows, MLA_V), lambda b, h, i: (b, h, k_block, 0)),
    ]
    args = [q, k, v]
    aliases = {}
    if prev is not None:
        in_specs.append(pl.BlockSpec(memory_space=pl.ANY))
        args.append(prev)
        aliases = {3: 0}
    return pl.pallas_call(
        _attn_kernel,
        grid=(batch, heads, n_q),
        in_specs=in_specs,
        out_specs=pl.BlockSpec((None, tq, MLA_V), lambda b, h, i: (b, q_block0 + i, h)),
        out_shape=jax.ShapeDtypeStruct((batch, s_all, heads * MLA_V), BF16),
        input_output_aliases=aliases,
        compiler_params=pltpu.CompilerParams(
            dimension_semantics=("arbitrary", "arbitrary", "arbitrary"), vmem_limit_bytes=VMEM_LIMIT_BYTES),
        name=name,
    )(*args)


def _out_proj_kernel(y_ref, w_ref, x_ref, mod_ref, o_ref):
    y = jnp.dot(y_ref[...], w_ref[...], preferred_element_type=F32)
    o_ref[...] = x_ref[...] + mod_ref[5:6, :] * y


def mixer_out_proj(y, w, hs, mods, *, layer, batch, n_tiles, name):
    d = hs.shape[1]
    kdim = y.shape[2]
    return pl.pallas_call(
        _out_proj_kernel,
        grid=(n_tiles,),
        in_specs=[
            pl.BlockSpec((None, TMX, kdim), lambda t: (_tile_batch(t, batch), _tile_block(t, batch), 0)),
            _resident(w.shape),
            pl.BlockSpec((TMX, d), lambda t: (t, 0)),
            pl.BlockSpec((None, None, N_MOD, d), lambda t: (layer, _tile_mod_row(t, batch), 0, 0)),
        ],
        out_specs=pl.BlockSpec((TMX, d), lambda t: (t, 0)),
        out_shape=jax.ShapeDtypeStruct(hs.shape, F32),
        input_output_aliases={2: 0},
        compiler_params=pltpu.CompilerParams(
            dimension_semantics=("arbitrary",), vmem_limit_bytes=VMEM_LIMIT_BYTES),
        name=name,
    )(y, w, hs, mods)


def mla_layer(hs, mods, norm_g3, weights, q_norm, kv_norm, w_o, cos, sin, *, layer, batch, with_ctx_out):
    wd, wuq, wukv = weights
    q, k, v = mla_project(hs, mods, norm_g3, wd, q_norm, kv_norm, wuq, wukv, cos, sin, layer=layer, batch=batch)
    tq = 512
    att = mla_attention(q, k, v, tq=tq, q_block0=0, n_q=SEQ // tq, k_rows=S_ALL, k_block=0,
                        name=f"mla_attn_lat_l{layer}")
    n_tiles = batch * LAT_TILES
    if with_ctx_out:
        att = mla_attention(q, k, v, tq=CTX_LEN, q_block0=SEQ // CTX_LEN, n_q=1, k_rows=CTX_LEN,
                            k_block=SEQ // CTX_LEN, prev=att, name=f"mla_attn_ctx_l{layer}")
        n_tiles = _stream_tiles(batch)
    return mixer_out_proj(att, w_o, hs, mods, layer=layer, batch=batch, n_tiles=n_tiles, name=f"mla_out_l{layer}")


HG_WIN = 128
HG_NH = 2
N_TBLK = S_ALL // TMX


def _hgrn_proj_kernel(x_ref, mod_ref, g_ref, w_ref, o_ref):
    n = _adanorm(x_ref[...], g_ref[...], mod_ref[3:4, :], mod_ref[4:5, :]).astype(BF16)
    o_ref[...] = jnp.dot(n, w_ref[...], preferred_element_type=F32)


def hgrn_project(hs, mods, norm_g3, w_in, *, layer, batch):
    d = hs.shape[1]
    n_out = w_in.shape[1]
    return pl.pallas_call(
        _hgrn_proj_kernel,
        grid=(n_out // d, _stream_tiles(batch)),
        in_specs=[
            pl.BlockSpec((TMX, d), lambda j, t: (t, 0)),
            pl.BlockSpec((None, None, N_MOD, d), lambda j, t: (layer, _tile_mod_row(t, batch), 0, 0)),
            pl.BlockSpec((None, 1, d), lambda j, t: (layer * N_SUB + 1, 0, 0)),
            pl.BlockSpec((d, d), lambda j, t: (0, j)),
        ],
        out_specs=pl.BlockSpec((None, TMX, d), lambda j, t: (_tile_batch(t, batch), _tile_block(t, batch), j)),
        out_shape=jax.ShapeDtypeStruct((batch, S_ALL, n_out), F32),
        compiler_params=pltpu.CompilerParams(
            dimension_semantics=("arbitrary", "arbitrary"), vmem_limit_bytes=VMEM_LIMIT_BYTES),
        name=f"hgrn_proj_l{layer}",
    )(hs, mods, norm_g3, w_in)


def _chunk_cumsum(x, row_in_chunk, reverse):
    n = x.shape[0]
    sh = 1
    while sh < HG_CHUNK:
        if reverse:
            x = x + jnp.where(row_in_chunk < HG_CHUNK - sh, pltpu.roll(x, n - sh, axis=0), 0.0)
        else:
            x = x + jnp.where(row_in_chunk >= sh, pltpu.roll(x, sh, axis=0), 0.0)
        sh *= 2
    return x


def _hgrn_scan_kernel(q_ref, z_ref, v_ref, la_ref, l1_ref, om_ref, o_ref,
                      st_ref, qd_ref, kin_ref, kte_ref, vb_ref, dec_ref, *, reverse):
    ts = pl.program_id(2)

    @pl.when(ts == 0)
    def _():
        st_ref[...] = jnp.zeros_like(st_ref)

    z = z_ref[...]
    e = jnp.exp(-jnp.abs(z))
    r = 1.0 / (1.0 + e)
    sig_neg = jnp.where(z >= 0, e * r, r)
    log_sig = jnp.minimum(z, 0.0) - jnp.log1p(e)
    la = la_ref[...]
    b2 = l1_ref[...] + log_sig
    log_f = jnp.maximum(la, b2) + jnp.log1p(jnp.exp(-jnp.abs(la - b2)))
    k = om_ref[...] * sig_neg
    row = lax.broadcasted_iota(jnp.int32, z.shape, 0) % HG_CHUNK
    b_inc = _chunk_cumsum(log_f, row, reverse)
    b_rest = _chunk_cumsum(log_f, row, not reverse) - log_f
    qr = q_ref[...]
    qd_ref[...] = (qr * jax.nn.sigmoid(qr) * jnp.exp(b_inc)).astype(BF16)
    kin_ref[...] = (k * jnp.exp(-b_inc)).astype(BF16)
    kte_ref[...] = (k * jnp.exp(b_rest)).astype(BF16)
    vb_ref[...] = v_ref[...].astype(BF16)
    dec_ref[...] = jnp.exp(b_inc + b_rest)

    wi = lax.broadcasted_iota(jnp.int32, (HG_WIN, HG_WIN), 0)
    wj = lax.broadcasted_iota(jnp.int32, (HG_WIN, HG_WIN), 1)
    same_chunk = (wi // HG_CHUNK) == (wj // HG_CHUNK)
    causal = same_chunk & ((wj >= wi) if reverse else (wj <= wi))
    tok = (lax.broadcasted_iota(jnp.int32, (HG_WIN, HG_HEAD_DIM), 0) // HG_CHUNK).astype(F32).astype(BF16)

    n_win = TMX // HG_WIN
    per_win = HG_WIN // HG_CHUNK
    wins = range(n_win - 1, -1, -1) if reverse else range(n_win)
    chunks = range(per_win - 1, -1, -1) if reverse else range(per_win)
    for w in wins:
        rows = slice(w * HG_WIN, (w + 1) * HG_WIN)
        for h in range(HG_NH):
            cols = slice(h * HG_HEAD_DIM, (h + 1) * HG_HEAD_DIM)
            qd_w, kin_w, kte_w, v_w = qd_ref[rows, cols], kin_ref[rows, cols], kte_ref[rows, cols], vb_ref[rows, cols]
            a = lax.dot_general(qd_w, kin_w, (((1,), (1,)), ((), ())), preferred_element_type=F32)
            a = jnp.where(causal, a, 0.0).astype(BF16)
            o_intra = jnp.dot(a, v_w, preferred_element_type=F32)
            v_t = v_ref[rows, cols].T.astype(BF16)
            for c in chunks:
                crow = slice(c * HG_CHUNK, (c + 1) * HG_CHUNK)
                grow = slice(w * HG_WIN + c * HG_CHUNK, w * HG_WIN + (c + 1) * HG_CHUNK)
                st = st_ref[h]
                o_inter = lax.dot_general(qd_ref[grow, cols], st.astype(BF16), (((1,), (1,)), ((), ())),
                                          preferred_element_type=F32)
                o_ref[grow, cols] = o_intra[crow, :] + o_inter
                kte_c = jnp.where(tok == c, kte_w, jnp.zeros_like(kte_w))
                upd = jnp.dot(v_t, kte_c, preferred_element_type=F32)
                dec = dec_ref[w * HG_WIN + c * HG_CHUNK:w * HG_WIN + c * HG_CHUNK + 1, cols]
                st_ref[h] = st * dec + upd


def hgrn_scan(p, la, l1, om, *, reverse, z_col, layer):
    batch, s_all, n_out = p.shape
    d = n_out // 5
    wcols = HG_NH * HG_HEAD_DIM
    groups = d // wcols

    def tblk(ts):
        lat = (N_TBLK - 1 - ts) if reverse else (ts - 1)
        return jnp.where(ts == 0, CTX_BLOCK, lat)

    col = lambda base: (lambda b, g, ts: (b, tblk(ts), base * groups + g))
    row_spec = pl.BlockSpec((1, wcols), lambda b, g, ts: (0, g))
    kern = functools.partial(_hgrn_scan_kernel, reverse=reverse)
    return pl.pallas_call(
        kern,
        grid=(batch, groups, N_TBLK),
        in_specs=[
            pl.BlockSpec((None, TMX, wcols), col(0)),
            pl.BlockSpec((None, TMX, wcols), col(z_col)),
            pl.BlockSpec((None, TMX, wcols), col(3)),
            row_spec, row_spec, row_spec,
        ],
        out_specs=pl.BlockSpec((None, TMX, wcols), lambda b, g, ts: (b, tblk(ts), g)),
        out_shape=jax.ShapeDtypeStruct((batch, s_all, d), F32),
        scratch_shapes=[
            pltpu.VMEM((HG_NH, HG_HEAD_DIM, HG_HEAD_DIM), F32),
            pltpu.VMEM((TMX, wcols), BF16), pltpu.VMEM((TMX, wcols), BF16),
            pltpu.VMEM((TMX, wcols), BF16), pltpu.VMEM((TMX, wcols), BF16),
            pltpu.VMEM((TMX, wcols), F32),
        ],
        compiler_params=pltpu.CompilerParams(
            dimension_semantics=("arbitrary", "arbitrary", "arbitrary"), vmem_limit_bytes=VMEM_LIMIT_BYTES),
        name=f"hgrn_scan_{'bwd' if reverse else 'fwd'}_l{layer}",
    )(p, p, p, la, l1, om)


def _hgrn_readout_kernel(of_ref, ob_ref, gate_ref, gn_ref, w_ref, x_ref, mod_ref, o_ref):
    o = of_ref[...] + ob_ref[...]
    gate = gate_ref[...]
    gs = gate * jax.nn.sigmoid(gate)
    parts = []
    for h in range(HG_HEADS):
        cols = slice(h * HG_HEAD_DIM, (h + 1) * HG_HEAD_DIM)
        parts.append((_rms(o[:, cols]) * gn_ref[:, cols] * gs[:, cols]).astype(BF16))
    y = jnp.dot(jnp.concatenate(parts, axis=1), w_ref[...], preferred_element_type=F32)
    o_ref[...] = x_ref[...] + mod_ref[5:6, :] * y


def hgrn_readout(o_f, o_b, p, g_norm, w_out, hs, mods, *, layer, batch, n_tiles):
    d = hs.shape[1]
    seq = lambda t: (_tile_batch(t, batch), _tile_block(t, batch), 0)
    return pl.pallas_call(
        _hgrn_readout_kernel,
        grid=(n_tiles,),
        in_specs=[
            pl.BlockSpec((None, TMX, d), seq),
            pl.BlockSpec((None, TMX, d), seq),
            pl.BlockSpec((None, TMX, d), lambda t: (_tile_batch(t, batch), _tile_block(t, batch), 4)),
            _resident((1, d)),
            _resident(w_out.shape),
            pl.BlockSpec((TMX, d), lambda t: (t, 0)),
            pl.BlockSpec((None, None, N_MOD, d), lambda t: (layer, _tile_mod_row(t, batch), 0, 0)),
        ],
        out_specs=pl.BlockSpec((TMX, d), lambda t: (t, 0)),
        out_shape=jax.ShapeDtypeStruct(hs.shape, F32),
        input_output_aliases={5: 0},
        compiler_params=pltpu.CompilerParams(
            dimension_semantics=("arbitrary",), vmem_limit_bytes=VMEM_LIMIT_BYTES),
        name=f"hgrn_out_l{layer}",
    )(o_f, o_b, p, g_norm, w_out, hs, mods)


def hgrn_layer(hs, mods, norm_g3, w_in, lb_fwd, lb_bwd, g_norm, w_out, *, layer, batch, with_ctx_out):
    p = hgrn_project(hs, mods, norm_g3, w_in, layer=layer, batch=batch)
    outs = []
    for reverse, lb, z_col in ((False, lb_fwd, 1), (True, lb_bwd, 2)):
        lb = lb.reshape(1, -1)
        outs.append(hgrn_scan(p, jnp.log(lb), jnp.log1p(-lb), 1.0 - lb, reverse=reverse, z_col=z_col, layer=layer))
    n_tiles = _stream_tiles(batch) if with_ctx_out else batch * LAT_TILES
    return hgrn_readout(outs[0], outs[1], p, g_norm.reshape(1, -1), w_out, hs, mods,
                        layer=layer, batch=batch, n_tiles=n_tiles)


FN_GC = D_MODEL // FOURIER_GROUPS
FN_TM = 512
FN_TK = 1024


def _fnet_tables(n_pos, tm):
    t = np.arange(n_pos, dtype=np.int64)
    ang = lambda k: 2.0 * np.pi * ((k[:, None] * t[None, :]) % n_pos) / n_pos
    phi = ang(np.arange(tm, dtype=np.int64))
    th = ang(np.arange(0, n_pos, tm, dtype=np.int64))
    sc = n_pos ** -0.5
    rows = np.stack([np.stack([np.cos(th), -np.sin(th)], axis=1),
                     np.stack([-np.sin(th), -np.cos(th)], axis=1)], axis=1) * sc
    rows = rows.reshape(-1, 2, n_pos)
    return (jnp.asarray(rows, F32), jnp.asarray(np.cos(phi), F32), jnp.asarray(np.sin(phi), F32))


def _fnet_channel_table():
    c = np.arange(FN_GC, dtype=np.int64)
    ang = 2.0 * np.pi * ((c[:, None] * c[None, :]) % FN_GC) / FN_GC
    return jnp.asarray(np.concatenate([np.cos(ang), np.sin(ang)], axis=1) * FN_GC ** -0.5, BF16)


def _fnet_chan_kernel(x_ref, mod_ref, g_ref, cs_ref, o_ref):
    n = _adanorm(x_ref[...], g_ref[...], mod_ref[3:4, :], mod_ref[4:5, :]).astype(BF16)
    for g in range(FOURIER_GROUPS):
        cols = slice(g * FN_GC, (g + 1) * FN_GC)
        pq = jnp.dot(n[:, cols], cs_ref[...], preferred_element_type=F32)
        o_ref[0, :, cols] = pq[:, :FN_GC].astype(BF16)
        o_ref[1, :, cols] = pq[:, FN_GC:].astype(BF16)


def fnet_channel_dft(hs, mods, norm_g3, cs, *, layer, batch):
    d = hs.shape[1]
    return pl.pallas_call(
        _fnet_chan_kernel,
        grid=(_stream_tiles(batch),),
        in_specs=[
            pl.BlockSpec((TMX, d), lambda t: (t, 0)),
            pl.BlockSpec((None, None, N_MOD, d), lambda t: (layer, _tile_mod_row(t, batch), 0, 0)),
            pl.BlockSpec((None, 1, d), lambda t: (layer * N_SUB + 1, 0, 0)),
            _resident(cs.shape),
        ],
        out_specs=pl.BlockSpec((None, 2, TMX, d), lambda t: (_tile_batch(t, batch), 0, _tile_block(t, batch), 0)),
        out_shape=jax.ShapeDtypeStruct((batch, 2, S_ALL, d), BF16),
        compiler_params=pltpu.CompilerParams(
            dimension_semantics=("arbitrary",), vmem_limit_bytes=VMEM_LIMIT_BYTES),
        name=f"fnet_chan_l{layer}",
    )(hs, mods, norm_g3, cs)


def _fnet_pos_kernel(rt_ref, cphi_ref, sphi_ref, pq_ref, w_ref, x_ref, mod_ref, o_ref, acc_ref, *, nk):
    kk = pl.program_id(2)
    tile = (rt_ref[0:1, :] * cphi_ref[...] + rt_ref[1:2, :] * sphi_ref[...]).astype(BF16)
    part = jnp.dot(tile, pq_ref[...], preferred_element_type=F32)

    @pl.when(kk == 0)
    def _():
        acc_ref[...] = part

    @pl.when(kk > 0)
    def _():
        acc_ref[...] += part

    @pl.when(kk == nk - 1)
    def _():
        y = jnp.dot(acc_ref[...].astype(BF16), w_ref[...], preferred_element_type=F32)
        o_ref[...] = x_ref[...] + mod_ref[5:6, :] * y


def fnet_position_dft(pq, tables, w_out, hs, mods, *, layer, n_pos, tm, tk, row_blk0, stream_blk0, mod_row, name):
    batch = pq.shape[0]
    d = hs.shape[1]
    rows, cphi, sphi = tables
    n_m, nkh = n_pos // tm, n_pos // tk
    nk = 2 * nkh
    kern = functools.partial(_fnet_pos_kernel, nk=nk)
    sblk = lambda b, m, kk: (stream_blk0 + b * n_m + m, 0)
    return pl.pallas_call(
        kern,
        grid=(batch, n_m, nk),
        in_specs=[
            pl.BlockSpec((None, 2, tk), lambda b, m, kk: (m * 2 + kk // nkh, 0, kk % nkh)),
            pl.BlockSpec((tm, tk), lambda b, m, kk: (0, kk % nkh)),
            pl.BlockSpec((tm, tk), lambda b, m, kk: (0, kk % nkh)),
            pl.BlockSpec((None, None, tk, d), lambda b, m, kk: (b, kk // nkh, row_blk0 + kk % nkh, 0)),
            _resident(w_out.shape),
            pl.BlockSpec((tm, d), sblk),
            pl.BlockSpec((None, None, N_MOD, d), lambda b, m, kk: (layer, mod_row(b), 0, 0)),
        ],
        out_specs=pl.BlockSpec((tm, d), sblk),
        out_shape=jax.ShapeDtypeStruct(hs.shape, F32),
        scratch_shapes=[pltpu.VMEM((tm, d), F32)],
        input_output_aliases={5: 0},
        compiler_params=pltpu.CompilerParams(
            dimension_semantics=("arbitrary", "arbitrary", "arbitrary"), vmem_limit_bytes=VMEM_LIMIT_BYTES),
        name=name,
    )(rows, cphi, sphi, pq, w_out, hs, mods)


def fnet_layer(hs, mods, norm_g3, w_out, *, layer, batch, with_ctx_out):
    pq = fnet_channel_dft(hs, mods, norm_g3, _fnet_channel_table(), layer=layer, batch=batch)
    hs = fnet_position_dft(pq, _fnet_tables(SEQ, FN_TM), w_out, hs, mods, layer=layer, n_pos=SEQ, tm=FN_TM,
                           tk=FN_TK, row_blk0=0, stream_blk0=0, mod_row=lambda b: b, name=f"fnet_pos_lat_l{layer}")
    if with_ctx_out:
        hs = fnet_position_dft(pq, _fnet_tables(CTX_LEN, CTX_LEN), w_out, hs, mods, layer=layer, n_pos=CTX_LEN,
                               tm=CTX_LEN, tk=CTX_LEN, row_blk0=SEQ // CTX_LEN,
                               stream_blk0=batch * SEQ // CTX_LEN, mod_row=lambda b: batch,
                               name=f"fnet_pos_ctx_l{layer}")
    return hs


def kernel(x, c, ctx, c_ctx, mod_w, mod_b, norm_g, ffn1_w_gu, ffn1_w_down, ffn2_w_gu, ffn2_w_down,
           hgrn_w_in, hgrn_lb_logits, hgrn_g_norm, hgrn_w_out,
           mla_w_dqkv, mla_q_norm, mla_kv_norm, mla_w_uq, mla_w_ukv, mla_w_o,
           fnet_w_out, final_g):
    B, T, D = x.shape
    n_lat = B * T
    n_ctx = B * CTX_LEN
    tm, tf = 512, 512
    rows_all = n_lat + n_ctx

    lb = jnp.cumsum(jax.nn.softmax(hgrn_lb_logits.astype(jnp.float32), axis=1), axis=1)
    lb = lb - lb[:, :1]
    rope_cos, rope_sin = mla_rope_tables(T)

    cc = jnp.concatenate([c, c_ctx[None, :], jnp.zeros((MOD_ROWS - B - 1, D), F32)], axis=0)
    mods = modulation(cc, mod_w, mod_b).reshape(DEPTH, MOD_ROWS, N_MOD, D)

    w1_gu, w1_down = ffn1_w_gu.astype(BF16), ffn1_w_down.astype(BF16)
    w2_gu, w2_down = ffn2_w_gu.astype(BF16), ffn2_w_down.astype(BF16)
    norm_g3 = norm_g.reshape(DEPTH * N_SUB, 1, D)
    final_g2 = final_g.reshape(1, D)

    hs = jnp.concatenate([x.reshape(n_lat, D), ctx.reshape(n_ctx, D)], axis=0)
    for i in range(DEPTH):
        kind, j = i % N_MIXERS, i // N_MIXERS
        last = i == DEPTH - 1
        ctx_in = not (last and kind == 2)
        hs = ffn_sublayer(hs, mods, norm_g3, w1_gu, w1_down, final_g2, layer=i, s=0,
                          rows=rows_all if ctx_in else n_lat, tm=tm, tf=tf)
        if kind == 0:
            hs = hgrn_layer(hs, mods, norm_g3, hgrn_w_in[j].astype(BF16), lb[0, j], lb[1, j], hgrn_g_norm[j],
                            hgrn_w_out[j].astype(BF16), layer=i, batch=B, with_ctx_out=not last)
        elif kind == 1:
            hs = mla_layer(hs, mods, norm_g3, mla_weights(mla_w_dqkv[j], mla_w_uq[j], mla_w_ukv[j]),
                           mla_q_norm[j].reshape(1, -1), mla_kv_norm[j].reshape(1, -1),
                           mla_w_o[j].astype(BF16), rope_cos, rope_sin, layer=i, batch=B, with_ctx_out=not last)
        else:
            hs = fnet_layer(hs, mods, norm_g3, fnet_w_out[j].astype(BF16), layer=i, batch=B, with_ctx_out=not last)
        hs = ffn_sublayer(hs, mods, norm_g3, w2_gu, w2_down, final_g2, layer=i, s=2,
                          rows=n_lat if last else rows_all, tm=tm, tf=tf, final=last)
    return hs.reshape(B, T, D)
```

```python
import functools

import numpy as np
import jax
import jax.numpy as jnp
from jax import lax
from jax.experimental import pallas as pl
from jax.experimental.pallas import tpu as pltpu

D_MODEL = 2048
SEQ = 4096
DEPTH = 4
GRID_W = 64
CTX_LEN = 256
N_MIXERS = 3
N_SUB = 3
D_FF = 5632
RMS_EPS = 1e-6

HG_HEAD_DIM = 128
HG_HEADS = D_MODEL // HG_HEAD_DIM
HG_QF = HG_HEADS * HG_HEAD_DIM
HG_IV = HG_HEADS * HG_HEAD_DIM
HG_CHUNK = 16

MLA_HEADS = 16
MLA_Q_RANK = 512
MLA_KV_RANK = 512
MLA_NOPE = 128
MLA_ROPE = 64
MLA_V = 128
MLA_SCALE = (MLA_NOPE + MLA_ROPE) ** -0.5
ATTN_BLOCK = 128
ROPE_THETA = 10000.0
ROPE_FREQS = MLA_ROPE // 4

FOURIER_GROUPS = 8

BF16 = jnp.bfloat16
F32 = jnp.float32

VMEM_LIMIT_BYTES = 56 * 1024 * 1024
MOD_ROWS = 8
N_MOD = N_SUB * 3


def _mod_row(t, tm):
    return jnp.minimum(t // (SEQ // tm), 2)


def _mod_kernel(c_ref, w_ref, b_ref, o_ref):
    c = c_ref[...]
    a = (c * jax.nn.sigmoid(c)).astype(BF16)
    o_ref[...] = jnp.dot(a, w_ref[...].astype(BF16), preferred_element_type=F32) + b_ref[...]


def modulation(cc, mod_w, mod_b, *, tn=1024):
    depth, d, n = mod_w.shape
    return pl.pallas_call(
        _mod_kernel,
        grid=(depth, n // tn),
        in_specs=[
            pl.BlockSpec((MOD_ROWS, d), lambda i, j: (0, 0)),
            pl.BlockSpec((None, d, tn), lambda i, j: (i, 0, j)),
            pl.BlockSpec((None, 1, tn), lambda i, j: (i, 0, j)),
        ],
        out_specs=pl.BlockSpec((None, MOD_ROWS, tn), lambda i, j: (i, 0, j)),
        out_shape=jax.ShapeDtypeStruct((depth, MOD_ROWS, n), F32),
        compiler_params=pltpu.CompilerParams(
            dimension_semantics=("arbitrary", "arbitrary"), vmem_limit_bytes=VMEM_LIMIT_BYTES),
        name="modulation",
    )(cc, mod_w, mod_b.reshape(depth, 1, n))


def _adanorm(x, g, shift, scale):
    y = x * lax.rsqrt(jnp.mean(x * x, axis=-1, keepdims=True) + RMS_EPS)
    return (y * g) * (1.0 + scale) + shift


def _ffn_kernel(x_ref, mod_ref, g_ref, wg_ref, wu_ref, wd_ref, fg_ref, o_ref, n_ref, *, s, nf, final):
    f = pl.program_id(1)

    @pl.when(f == 0)
    def _():
        n = _adanorm(x_ref[...], g_ref[...], mod_ref[3 * s:3 * s + 1, :], mod_ref[3 * s + 1:3 * s + 2, :])
        n_ref[...] = n.astype(BF16)

    n = n_ref[...]
    gate = jnp.dot(n, wg_ref[...], preferred_element_type=F32)
    up = jnp.dot(n, wu_ref[...], preferred_element_type=F32)
    act = (gate * jax.nn.sigmoid(gate) * up).astype(BF16)
    part = jnp.dot(act, wd_ref[...], preferred_element_type=F32)

    @pl.when(f == 0)
    def _():
        o_ref[...] = part

    @pl.when(f > 0)
    def _():
        o_ref[...] += part

    @pl.when(f == nf - 1)
    def _():
        h = x_ref[...] + (0.5 * mod_ref[3 * s + 2:3 * s + 3, :]) * o_ref[...]
        if final:
            h = h * lax.rsqrt(jnp.mean(h * h, axis=-1, keepdims=True) + RMS_EPS) * fg_ref[...]
        o_ref[...] = h


def ffn_sublayer(h, mods, g, w_gu, w_down, final_g, *, layer, s, rows, tm, tf, final=False):
    d = h.shape[1]
    nf = D_FF // tf
    kern = functools.partial(_ffn_kernel, s=s, nf=nf, final=final)
    return pl.pallas_call(
        kern,
        grid=(rows // tm, nf),
        in_specs=[
            pl.BlockSpec((tm, d), lambda t, f: (t, 0)),
            pl.BlockSpec((None, None, N_MOD, d), lambda t, f: (layer, _mod_row(t, tm), 0, 0)),
            pl.BlockSpec((None, 1, d), lambda t, f: (layer * N_SUB + s, 0, 0)),
            pl.BlockSpec((None, d, tf), lambda t, f: (layer, 0, f)),
            pl.BlockSpec((None, d, tf), lambda t, f: (layer, 0, f + nf)),
            pl.BlockSpec((None, tf, d), lambda t, f: (layer, f, 0)),
            pl.BlockSpec((1, d), lambda t, f: (0, 0)),
        ],
        out_specs=pl.BlockSpec((tm, d), lambda t, f: (t, 0)),
        out_shape=jax.ShapeDtypeStruct((rows if final else h.shape[0], d), F32),
        scratch_shapes=[pltpu.VMEM((tm, d), BF16)],
        compiler_params=pltpu.CompilerParams(
            dimension_semantics=("arbitrary", "arbitrary"), vmem_limit_bytes=VMEM_LIMIT_BYTES),
        name=f"ffn_l{layer}_s{s}",
    )(h, mods, g, w_gu, w_gu, w_down, final_g)


TMX = 256
S_ALL = SEQ + CTX_LEN
LAT_TILES = SEQ // TMX
CTX_BLOCK = SEQ // TMX


def _resident(shape):
    return pl.BlockSpec(shape, lambda *_: (0,) * len(shape), pipeline_mode=pl.Buffered(1))


def _stream_tiles(batch):
    return batch * (LAT_TILES + CTX_LEN // TMX)


def _tile_batch(t, batch):
    lat = t < batch * LAT_TILES
    return jnp.where(lat, t // LAT_TILES, t - batch * LAT_TILES)


def _tile_block(t, batch):
    return jnp.where(t < batch * LAT_TILES, t % LAT_TILES, CTX_BLOCK)


def _tile_mod_row(t, batch):
    return jnp.where(t < batch * LAT_TILES, t // LAT_TILES, batch)


def _rms(x):
    return x * lax.rsqrt(jnp.mean(x * x, axis=-1, keepdims=True) + RMS_EPS)


MLA_QK = 2 * MLA_NOPE
N_DQ = MLA_Q_RANK + MLA_KV_RANK


def _mla_proj_kernel(x_ref, mod_ref, g_ref, wd_ref, qn_ref, kvn_ref, wuq_ref, wukv_ref, cos_ref, sin_ref,
                     q_ref, k_ref, v_ref):
    n = _adanorm(x_ref[...], g_ref[...], mod_ref[3:4, :], mod_ref[4:5, :]).astype(BF16)
    proj = jnp.dot(n, wd_ref[...], preferred_element_type=F32)
    cq = (_rms(proj[:, :MLA_Q_RANK]) * qn_ref[...]).astype(BF16)
    ckv = (_rms(proj[:, MLA_Q_RANK:N_DQ]) * kvn_ref[...]).astype(BF16)
    cos, sin = cos_ref[...], sin_ref[...]
    kr = proj[:, N_DQ:N_DQ + 128] * cos + proj[:, N_DQ + 128:N_DQ + 256] * sin
    kr = kr.astype(BF16)
    q = jnp.dot(cq, wuq_ref[...], preferred_element_type=F32) * MLA_SCALE
    kv = jnp.dot(ckv, wukv_ref[...], preferred_element_type=F32)
    hn = MLA_HEADS * MLA_NOPE
    for h in range(MLA_HEADS):
        lo = h * 128
        q_ref[h, :, 0:128] = q[:, lo:lo + 128].astype(BF16)
        qr = q[:, hn + lo:hn + lo + 128] * cos + q[:, 2 * hn + lo:2 * hn + lo + 128] * sin
        q_ref[h, :, 128:256] = qr.astype(BF16)
        k_ref[h, :, 0:128] = kv[:, lo:lo + 128].astype(BF16)
        k_ref[h, :, 128:256] = kr
        v_ref[h, :, :] = kv[:, hn + lo:hn + lo + 128].astype(BF16)


def _rope_partner(width):
    idx = jnp.arange(width)
    return idx ^ ROPE_FREQS


def mla_weights(w_dqkv, w_uq, w_ukv):
    d = w_dqkv.shape[0]
    z = jnp.zeros((d, 128 - MLA_ROPE), w_dqkv.dtype)
    kr = w_dqkv[:, N_DQ:]
    wd = jnp.concatenate([w_dqkv[:, :N_DQ], kr, z, kr[:, _rope_partner(MLA_ROPE)], z], axis=1)
    wq = w_uq.reshape(MLA_Q_RANK, MLA_HEADS, MLA_NOPE + MLA_ROPE)
    qr = wq[:, :, MLA_NOPE:]
    zq = jnp.zeros((MLA_Q_RANK, MLA_HEADS, 128 - MLA_ROPE), w_uq.dtype)
    wuq = jnp.concatenate([
        wq[:, :, :MLA_NOPE].reshape(MLA_Q_RANK, -1),
        jnp.concatenate([qr, zq], axis=2).reshape(MLA_Q_RANK, -1),
        jnp.concatenate([qr[:, :, _rope_partner(MLA_ROPE)], zq], axis=2).reshape(MLA_Q_RANK, -1)], axis=1)
    wkv = w_ukv.reshape(MLA_KV_RANK, MLA_HEADS, MLA_NOPE + MLA_V)
    wukv = jnp.concatenate([wkv[:, :, :MLA_NOPE].reshape(MLA_KV_RANK, -1),
                            wkv[:, :, MLA_NOPE:].reshape(MLA_KV_RANK, -1)], axis=1)
    return wd.astype(BF16), wuq.astype(BF16), wukv.astype(BF16)


def mla_rope_tables(n_tokens):
    rows = n_tokens // GRID_W
    r = jnp.broadcast_to(jnp.arange(rows, dtype=F32)[:, None], (rows, GRID_W)).reshape(-1)
    col = jnp.broadcast_to(jnp.arange(GRID_W, dtype=F32)[None, :], (rows, GRID_W)).reshape(-1)
    inv_freq = ROPE_THETA ** (-jnp.arange(ROPE_FREQS, dtype=F32) / ROPE_FREQS)
    ang = jnp.stack([r, col], axis=-1)[..., None] * inv_freq
    cos = jnp.broadcast_to(jnp.cos(ang)[:, :, None, :], (n_tokens, 2, 2, ROPE_FREQS)).reshape(n_tokens, MLA_ROPE)
    sin = jnp.sin(ang)
    sin = jnp.stack([-sin, sin], axis=2).reshape(n_tokens, MLA_ROPE)
    pad = jnp.zeros((n_tokens, 128 - MLA_ROPE), F32)
    cos = jnp.concatenate([cos, pad], axis=1)
    sin = jnp.concatenate([sin, pad], axis=1)
    ctx_cos = jnp.concatenate([jnp.ones((CTX_LEN, MLA_ROPE), F32), jnp.zeros((CTX_LEN, 128 - MLA_ROPE), F32)], axis=1)
    return (jnp.concatenate([cos, ctx_cos], axis=0),
            jnp.concatenate([sin, jnp.zeros((CTX_LEN, 128), F32)], axis=0))


def mla_project(hs, mods, norm_g3, wd, q_norm, kv_norm, wuq, wukv, cos, sin, *, layer, batch):
    d = hs.shape[1]
    bmap = lambda t: (_tile_batch(t, batch), 0, _tile_block(t, batch), 0)
    return pl.pallas_call(
        _mla_proj_kernel,
        grid=(_stream_tiles(batch),),
        in_specs=[
            pl.BlockSpec((TMX, d), lambda t: (t, 0)),
            pl.BlockSpec((None, None, N_MOD, d), lambda t: (layer, _tile_mod_row(t, batch), 0, 0)),
            pl.BlockSpec((None, 1, d), lambda t: (layer * N_SUB + 1, 0, 0)),
            _resident(wd.shape),
            _resident((1, MLA_Q_RANK)),
            _resident((1, MLA_KV_RANK)),
            _resident(wuq.shape),
            _resident(wukv.shape),
            pl.BlockSpec((TMX, 128), lambda t: (_tile_block(t, batch), 0)),
            pl.BlockSpec((TMX, 128), lambda t: (_tile_block(t, batch), 0)),
        ],
        out_specs=[
            pl.BlockSpec((None, MLA_HEADS, TMX, MLA_QK), bmap),
            pl.BlockSpec((None, MLA_HEADS, TMX, MLA_QK), bmap),
            pl.BlockSpec((None, MLA_HEADS, TMX, MLA_V), bmap),
        ],
        out_shape=[
            jax.ShapeDtypeStruct((batch, MLA_HEADS, S_ALL, MLA_QK), BF16),
            jax.ShapeDtypeStruct((batch, MLA_HEADS, S_ALL, MLA_QK), BF16),
            jax.ShapeDtypeStruct((batch, MLA_HEADS, S_ALL, MLA_V), BF16),
        ],
        compiler_params=pltpu.CompilerParams(
            dimension_semantics=("arbitrary",), vmem_limit_bytes=VMEM_LIMIT_BYTES),
        name=f"mla_proj_l{layer}",
    )(hs, mods, norm_g3, wd, q_norm, kv_norm, wuq, wukv, cos, sin)


def _attn_kernel(q_ref, k_ref, v_ref, *rest):
    o_ref = rest[-1]
    s = lax.dot_general(q_ref[...], k_ref[...], (((1,), (1,)), ((), ())), preferred_element_type=F32)
    p = jnp.exp(s - jnp.max(s, axis=-1, keepdims=True))
    l = jnp.sum(p, axis=-1, keepdims=True)
    o = jnp.dot(p.astype(BF16), v_ref[...], preferred_element_type=F32)
    o_ref[...] = (o / l).astype(BF16)


def mla_attention(q, k, v, *, tq, q_block0, n_q, k_rows, k_block, prev=None, name):
    batch, heads, s_all, _ = q.shape
    in_specs = [
        pl.BlockSpec((None, None, tq, MLA_QK), lambda b, h, i: (b, h, q_block0 + i, 0)),
        pl.BlockSpec((None, None, k_rows, MLA_QK), lambda b, h, i: (b, h, k_block, 0)),
        pl.BlockSpec((None, None, k_rows, MLA_V), lambda b, h, i: (b, h, k_block, 0)),
    ]
    args = [q, k, v]
    aliases = {}
    if prev is not None:
        in_specs.append(pl.BlockSpec(memory_space=pl.ANY))
        args.append(prev)
        aliases = {3: 0}
    return pl.pallas_call(
        _attn_kernel,
        grid=(batch, heads, n_q),
        in_specs=in_specs,
        out_specs=pl.BlockSpec((None, tq, MLA_V), lambda b, h, i: (b, q_block0 + i, h)),
        out_shape=jax.ShapeDtypeStruct((batch, s_all, heads * MLA_V), BF16),
        input_output_aliases=aliases,
        compiler_params=pltpu.CompilerParams(
            dimension_semantics=("arbitrary", "arbitrary", "arbitrary"), vmem_limit_bytes=VMEM_LIMIT_BYTES),
        name=name,
    )(*args)


def _out_proj_kernel(y_ref, w_ref, x_ref, mod_ref, o_ref):
    y = jnp.dot(y_ref[...], w_ref[...], preferred_element_type=F32)
    o_ref[...] = x_ref[...] + mod_ref[5:6, :] * y


def mixer_out_proj(y, w, hs, mods, *, layer, batch, n_tiles, name):
    d = hs.shape[1]
    kdim = y.shape[2]
    return pl.pallas_call(
        _out_proj_kernel,
        grid=(n_tiles,),
        in_specs=[
            pl.BlockSpec((None, TMX, kdim), lambda t: (_tile_batch(t, batch), _tile_block(t, batch), 0)),
            _resident(w.shape),
            pl.BlockSpec((TMX, d), lambda t: (t, 0)),
            pl.BlockSpec((None, None, N_MOD, d), lambda t: (layer, _tile_mod_row(t, batch), 0, 0)),
        ],
        out_specs=pl.BlockSpec((TMX, d), lambda t: (t, 0)),
        out_shape=jax.ShapeDtypeStruct(hs.shape, F32),
        input_output_aliases={2: 0},
        compiler_params=pltpu.CompilerParams(
            dimension_semantics=("arbitrary",), vmem_limit_bytes=VMEM_LIMIT_BYTES),
        name=name,
    )(y, w, hs, mods)


def mla_layer(hs, mods, norm_g3, weights, q_norm, kv_norm, w_o, cos, sin, *, layer, batch, with_ctx_out):
    wd, wuq, wukv = weights
    q, k, v = mla_project(hs, mods, norm_g3, wd, q_norm, kv_norm, wuq, wukv, cos, sin, layer=layer, batch=batch)
    tq = 512
    att = mla_attention(q, k, v, tq=tq, q_block0=0, n_q=SEQ // tq, k_rows=S_ALL, k_block=0,
                        name=f"mla_attn_lat_l{layer}")
    n_tiles = batch * LAT_TILES
    if with_ctx_out:
        att = mla_attention(q, k, v, tq=CTX_LEN, q_block0=SEQ // CTX_LEN, n_q=1, k_rows=CTX_LEN,
                            k_block=SEQ // CTX_LEN, prev=att, name=f"mla_attn_ctx_l{layer}")
        n_tiles = _stream_tiles(batch)
    return mixer_out_proj(att, w_o, hs, mods, layer=layer, batch=batch, n_tiles=n_tiles, name=f"mla_out_l{layer}")


HG_WIN = 128
HG_NH = 4
N_TBLK = S_ALL // TMX


def _hgrn_proj_kernel(x_ref, mod_ref, g_ref, w_ref, o_ref):
    n = _adanorm(x_ref[...], g_ref[...], mod_ref[3:4, :], mod_ref[4:5, :]).astype(BF16)
    o_ref[...] = jnp.dot(n, w_ref[...], preferred_element_type=F32)


def hgrn_project(hs, mods, norm_g3, w_in, *, layer, batch):
    d = hs.shape[1]
    n_out = w_in.shape[1]
    return pl.pallas_call(
        _hgrn_proj_kernel,
        grid=(n_out // d, _stream_tiles(batch)),
        in_specs=[
            pl.BlockSpec((TMX, d), lambda j, t: (t, 0)),
            pl.BlockSpec((None, None, N_MOD, d), lambda j, t: (layer, _tile_mod_row(t, batch), 0, 0)),
            pl.BlockSpec((None, 1, d), lambda j, t: (layer * N_SUB + 1, 0, 0)),
            pl.BlockSpec((d, d), lambda j, t: (0, j)),
        ],
        out_specs=pl.BlockSpec((None, TMX, d), lambda j, t: (_tile_batch(t, batch), _tile_block(t, batch), j)),
        out_shape=jax.ShapeDtypeStruct((batch, S_ALL, n_out), F32),
        compiler_params=pltpu.CompilerParams(
            dimension_semantics=("arbitrary", "arbitrary"), vmem_limit_bytes=VMEM_LIMIT_BYTES),
        name=f"hgrn_proj_l{layer}",
    )(hs, mods, norm_g3, w_in)


def _split3(x):
    hi = x.astype(BF16)
    r1 = x - hi.astype(F32)
    mid = r1.astype(BF16)
    lo = (r1 - mid.astype(F32)).astype(BF16)
    return hi, mid, lo


def _hgrn_scan_kernel(q_ref, z_ref, v_ref, lb_ref, om_ref, tri_ref, o_ref,
                      st_ref, qd_ref, kin_ref, kte_ref, vb_ref, tot_ref, *, reverse):
    ts = pl.program_id(2)

    @pl.when(ts == 0)
    def _():
        st_ref[...] = jnp.zeros_like(st_ref)

    z = z_ref[...]
    e = jnp.exp(-jnp.abs(z))
    r = 1.0 / (1.0 + e)
    er = e * r
    pos = z >= 0
    om = om_ref[...]
    f = lb_ref[...] + om * jnp.where(pos, r, er)
    log_f = jnp.log(f)
    k = om * jnp.where(pos, er, r)
    tri = tri_ref[...]
    cum = sum(jnp.dot(tri, piece, preferred_element_type=F32) for piece in _split3(log_f))
    b_inc = cum[:TMX]
    b_rest = cum[TMX:]
    qr = q_ref[...]
    qd_ref[...] = (qr * jax.nn.sigmoid(qr) * jnp.exp(b_inc)).astype(BF16)
    kin_ref[...] = (k * jnp.exp(-b_inc)).astype(BF16)
    kte_ref[...] = (k * jnp.exp(b_rest)).astype(BF16)
    vb_ref[...] = v_ref[...].astype(BF16)
    tot = b_inc + b_rest
    dec_all = []
    for h in range(HG_NH):
        tot_ref[h] = tot[:, h * HG_HEAD_DIM:(h + 1) * HG_HEAD_DIM]
        dec_all.append(jnp.exp(tot_ref[h, pl.ds(0, TMX // HG_CHUNK, stride=HG_CHUNK), :]))

    wi = lax.broadcasted_iota(jnp.int32, (HG_WIN, HG_WIN), 0)
    wj = lax.broadcasted_iota(jnp.int32, (HG_WIN, HG_WIN), 1)
    same_chunk = (wi // HG_CHUNK) == (wj // HG_CHUNK)
    causal = same_chunk & ((wj >= wi) if reverse else (wj <= wi))

    n_win = TMX // HG_WIN
    per_win = HG_WIN // HG_CHUNK
    wins = list(range(n_win - 1, -1, -1) if reverse else range(n_win))
    chunks = list(range(per_win - 1, -1, -1) if reverse else range(per_win))
    def only_chunk(x, c):
        parts = []
        if c > 0:
            parts.append(jnp.zeros((c * HG_CHUNK, x.shape[1]), x.dtype))
        parts.append(x[c * HG_CHUNK:(c + 1) * HG_CHUNK, :])
        if c < per_win - 1:
            parts.append(jnp.zeros(((per_win - 1 - c) * HG_CHUNK, x.shape[1]), x.dtype))
        return jnp.concatenate(parts, axis=0)

    states = [st_ref[h] for h in range(HG_NH)]
    heads = range(HG_NH)
    hcols = [slice(h * HG_HEAD_DIM, (h + 1) * HG_HEAD_DIM) for h in heads]
    for w in wins:
        rows = slice(w * HG_WIN, (w + 1) * HG_WIN)
        o_intra, upd = [], []
        for h in heads:
            qd_w, kin_w, kte_w, v_w = (r[rows, hcols[h]] for r in (qd_ref, kin_ref, kte_ref, vb_ref))
            a = lax.dot_general(qd_w, kin_w, (((1,), (1,)), ((), ())), preferred_element_type=F32)
            a = jnp.where(causal, a, 0.0).astype(BF16)
            o_intra.append(jnp.dot(a, v_w, preferred_element_type=F32))
            v_t = v_ref[rows, hcols[h]].T.astype(BF16)
            rhs = jnp.concatenate([only_chunk(kte_w, c) for c in range(per_win)], axis=1)
            upd.append(jnp.dot(v_t, rhs, preferred_element_type=F32))
        entry = [dict() for _ in heads]
        for h in heads:
            st = states[h]
            for c in chunks:
                entry[h][c] = st.astype(BF16)
                ci = w * per_win + c
                st = st * dec_all[h][ci:ci + 1, :] + upd[h][:, c * HG_HEAD_DIM:(c + 1) * HG_HEAD_DIM]
            states[h] = st
        for h in heads:
            qd_w = qd_ref[rows, hcols[h]]
            lhs = jnp.concatenate([only_chunk(qd_w, c) for c in range(per_win)], axis=1)
            ent = jnp.concatenate([entry[h][c] for c in range(per_win)], axis=1)
            o_inter = lax.dot_general(lhs, ent, (((1,), (1,)), ((), ())), preferred_element_type=F32)
            o_ref[rows, hcols[h]] = o_intra[h] + o_inter
    for h in heads:
        st_ref[h] = states[h]


def _hgrn_cumsum_matrix(reverse):
    r = np.arange(TMX)[:, None]
    c = np.arange(TMX)[None, :]
    same = (r // HG_CHUNK) == (c // HG_CHUNK)
    inc, rest = ((c >= r), (c < r)) if reverse else ((c <= r), (c > r))
    return jnp.asarray(np.concatenate([same & inc, same & rest], axis=0), BF16)


def hgrn_scan(p, lb, om, *, reverse, z_col, layer):
    batch, s_all, n_out = p.shape
    d = n_out // 5
    wcols = HG_NH * HG_HEAD_DIM
    groups = d // wcols

    def tblk(ts):
        lat = (N_TBLK - 1 - ts) if reverse else (ts - 1)
        return jnp.where(ts == 0, CTX_BLOCK, lat)

    col = lambda base: (lambda b, g, ts: (b, tblk(ts), base * groups + g))
    row_spec = pl.BlockSpec((1, wcols), lambda b, g, ts: (0, g))
    kern = functools.partial(_hgrn_scan_kernel, reverse=reverse)
    return pl.pallas_call(
        kern,
        grid=(batch, groups, N_TBLK),
        in_specs=[
            pl.BlockSpec((None, TMX, wcols), col(0)),
            pl.BlockSpec((None, TMX, wcols), col(z_col)),
            pl.BlockSpec((None, TMX, wcols), col(3)),
            row_spec, row_spec, _resident((2 * TMX, TMX)),
        ],
        out_specs=pl.BlockSpec((None, TMX, wcols), lambda b, g, ts: (b, tblk(ts), g)),
        out_shape=jax.ShapeDtypeStruct((batch, s_all, d), F32),
        scratch_shapes=[
            pltpu.VMEM((HG_NH, HG_HEAD_DIM, HG_HEAD_DIM), F32),
            pltpu.VMEM((TMX, wcols), BF16), pltpu.VMEM((TMX, wcols), BF16),
            pltpu.VMEM((TMX, wcols), BF16), pltpu.VMEM((TMX, wcols), BF16),
            pltpu.VMEM((HG_NH, TMX, HG_HEAD_DIM), F32),
        ],
        compiler_params=pltpu.CompilerParams(
            dimension_semantics=("arbitrary", "arbitrary", "arbitrary"), vmem_limit_bytes=VMEM_LIMIT_BYTES),
        name=f"hgrn_scan_{'bwd' if reverse else 'fwd'}_l{layer}",
    )(p, p, p, lb, om, _hgrn_cumsum_matrix(reverse))


def _hgrn_readout_kernel(of_ref, ob_ref, gate_ref, gn_ref, w_ref, x_ref, mod_ref, o_ref):
    o = of_ref[...] + ob_ref[...]
    gate = gate_ref[...]
    gs = gate * jax.nn.sigmoid(gate)
    parts = []
    for h in range(HG_HEADS):
        cols = slice(h * HG_HEAD_DIM, (h + 1) * HG_HEAD_DIM)
        parts.append((_rms(o[:, cols]) * gn_ref[:, cols] * gs[:, cols]).astype(BF16))
    y = jnp.dot(jnp.concatenate(parts, axis=1), w_ref[...], preferred_element_type=F32)
    o_ref[...] = x_ref[...] + mod_ref[5:6, :] * y


def hgrn_readout(o_f, o_b, p, g_norm, w_out, hs, mods, *, layer, batch, n_tiles):
    d = hs.shape[1]
    seq = lambda t: (_tile_batch(t, batch), _tile_block(t, batch), 0)
    return pl.pallas_call(
        _hgrn_readout_kernel,
        grid=(n_tiles,),
        in_specs=[
            pl.BlockSpec((None, TMX, d), seq),
            pl.BlockSpec((None, TMX, d), seq),
            pl.BlockSpec((None, TMX, d), lambda t: (_tile_batch(t, batch), _tile_block(t, batch), 4)),
            _resident((1, d)),
            _resident(w_out.shape),
            pl.BlockSpec((TMX, d), lambda t: (t, 0)),
            pl.BlockSpec((None, None, N_MOD, d), lambda t: (layer, _tile_mod_row(t, batch), 0, 0)),
        ],
        out_specs=pl.BlockSpec((TMX, d), lambda t: (t, 0)),
        out_shape=jax.ShapeDtypeStruct(hs.shape, F32),
        input_output_aliases={5: 0},
        compiler_params=pltpu.CompilerParams(
            dimension_semantics=("arbitrary",), vmem_limit_bytes=VMEM_LIMIT_BYTES),
        name=f"hgrn_out_l{layer}",
    )(o_f, o_b, p, g_norm, w_out, hs, mods)


def hgrn_layer(hs, mods, norm_g3, w_in, lb_fwd, lb_bwd, g_norm, w_out, *, layer, batch, with_ctx_out):
    p = hgrn_project(hs, mods, norm_g3, w_in, layer=layer, batch=batch)
    outs = []
    for reverse, lb, z_col in ((False, lb_fwd, 1), (True, lb_bwd, 2)):
        lb = lb.reshape(1, -1)
        outs.append(hgrn_scan(p, lb, 1.0 - lb, reverse=reverse, z_col=z_col, layer=layer))
    n_tiles = _stream_tiles(batch) if with_ctx_out else batch * LAT_TILES
    return hgrn_readout(outs[0], outs[1], p, g_norm.reshape(1, -1), w_out, hs, mods,
                        layer=layer, batch=batch, n_tiles=n_tiles)


FN_GC = D_MODEL // FOURIER_GROUPS
FN_TM = 512
FN_TK = 1024


def _fnet_tables(n_pos, tm):
    t = np.arange(n_pos, dtype=np.int64)
    ang = lambda k: 2.0 * np.pi * ((k[:, None] * t[None, :]) % n_pos) / n_pos
    phi = ang(np.arange(tm, dtype=np.int64))
    th = ang(np.arange(0, n_pos, tm, dtype=np.int64))
    sc = n_pos ** -0.5
    rows = np.stack([np.stack([np.cos(th), -np.sin(th)], axis=1),
                     np.stack([-np.sin(th), -np.cos(th)], axis=1)], axis=1) * sc
    rows = rows.reshape(-1, 2, n_pos)
    return (jnp.asarray(rows, F32), jnp.asarray(np.cos(phi), F32), jnp.asarray(np.sin(phi), F32))


def _fnet_channel_table():
    c = np.arange(FN_GC, dtype=np.int64)
    ang = 2.0 * np.pi * ((c[:, None] * c[None, :]) % FN_GC) / FN_GC
    return jnp.asarray(np.concatenate([np.cos(ang), np.sin(ang)], axis=1) * FN_GC ** -0.5, BF16)


def _fnet_chan_kernel(x_ref, mod_ref, g_ref, cs_ref, o_ref):
    n = _adanorm(x_ref[...], g_ref[...], mod_ref[3:4, :], mod_ref[4:5, :]).astype(BF16)
    for g in range(FOURIER_GROUPS):
        cols = slice(g * FN_GC, (g + 1) * FN_GC)
        pq = jnp.dot(n[:, cols], cs_ref[...], preferred_element_type=F32)
        o_ref[0, :, cols] = pq[:, :FN_GC].astype(BF16)
        o_ref[1, :, cols] = pq[:, FN_GC:].astype(BF16)


def fnet_channel_dft(hs, mods, norm_g3, cs, *, layer, batch):
    d = hs.shape[1]
    return pl.pallas_call(
        _fnet_chan_kernel,
        grid=(_stream_tiles(batch),),
        in_specs=[
            pl.BlockSpec((TMX, d), lambda t: (t, 0)),
            pl.BlockSpec((None, None, N_MOD, d), lambda t: (layer, _tile_mod_row(t, batch), 0, 0)),
            pl.BlockSpec((None, 1, d), lambda t: (layer * N_SUB + 1, 0, 0)),
            _resident(cs.shape),
        ],
        out_specs=pl.BlockSpec((None, 2, TMX, d), lambda t: (_tile_batch(t, batch), 0, _tile_block(t, batch), 0)),
        out_shape=jax.ShapeDtypeStruct((batch, 2, S_ALL, d), BF16),
        compiler_params=pltpu.CompilerParams(
            dimension_semantics=("arbitrary",), vmem_limit_bytes=VMEM_LIMIT_BYTES),
        name=f"fnet_chan_l{layer}",
    )(hs, mods, norm_g3, cs)


def _fnet_pos_kernel(rt_ref, cphi_ref, sphi_ref, pq_ref, w_ref, x_ref, mod_ref, o_ref, acc_ref, *, nk):
    kk = pl.program_id(2)
    tile = (rt_ref[0:1, :] * cphi_ref[...] + rt_ref[1:2, :] * sphi_ref[...]).astype(BF16)
    part = jnp.dot(tile, pq_ref[...], preferred_element_type=F32)

    @pl.when(kk == 0)
    def _():
        acc_ref[...] = part

    @pl.when(kk > 0)
    def _():
        acc_ref[...] += part

    @pl.when(kk == nk - 1)
    def _():
        y = jnp.dot(acc_ref[...].astype(BF16), w_ref[...], preferred_element_type=F32)
        o_ref[...] = x_ref[...] + mod_ref[5:6, :] * y


def fnet_position_dft(pq, tables, w_out, hs, mods, *, layer, n_pos, tm, tk, row_blk0, stream_blk0, mod_row, name):
    batch = pq.shape[0]
    d = hs.shape[1]
    rows, cphi, sphi = tables
    n_m, nkh = n_pos // tm, n_pos // tk
    nk = 2 * nkh
    kern = functools.partial(_fnet_pos_kernel, nk=nk)
    sblk = lambda b, m, kk: (stream_blk0 + b * n_m + m, 0)
    return pl.pallas_call(
        kern,
        grid=(batch, n_m, nk),
        in_specs=[
            pl.BlockSpec((None, 2, tk), lambda b, m, kk: (m * 2 + kk // nkh, 0, kk % nkh)),
            pl.BlockSpec((tm, tk), lambda b, m, kk: (0, kk % nkh)),
            pl.BlockSpec((tm, tk), lambda b, m, kk: (0, kk % nkh)),
            pl.BlockSpec((None, None, tk, d), lambda b, m, kk: (b, kk // nkh, row_blk0 + kk % nkh, 0)),
            _resident(w_out.shape),
            pl.BlockSpec((tm, d), sblk),
            pl.BlockSpec((None, None, N_MOD, d), lambda b, m, kk: (layer, mod_row(b), 0, 0)),
        ],
        out_specs=pl.BlockSpec((tm, d), sblk),
        out_shape=jax.ShapeDtypeStruct(hs.shape, F32),
        scratch_shapes=[pltpu.VMEM((tm, d), F32)],
        input_output_aliases={5: 0},
        compiler_params=pltpu.CompilerParams(
            dimension_semantics=("arbitrary", "arbitrary", "arbitrary"), vmem_limit_bytes=VMEM_LIMIT_BYTES),
        name=name,
    )(rows, cphi, sphi, pq, w_out, hs, mods)


def fnet_layer(hs, mods, norm_g3, w_out, *, layer, batch, with_ctx_out):
    pq = fnet_channel_dft(hs, mods, norm_g3, _fnet_channel_table(), layer=layer, batch=batch)
    hs = fnet_position_dft(pq, _fnet_tables(SEQ, FN_TM), w_out, hs, mods, layer=layer, n_pos=SEQ, tm=FN_TM,
                           tk=FN_TK, row_blk0=0, stream_blk0=0, mod_row=lambda b: b, name=f"fnet_pos_lat_l{layer}")
    if with_ctx_out:
        hs = fnet_position_dft(pq, _fnet_tables(CTX_LEN, CTX_LEN), w_out, hs, mods, layer=layer, n_pos=CTX_LEN,
                               tm=CTX_LEN, tk=CTX_LEN, row_blk0=SEQ // CTX_LEN,
                               stream_blk0=batch * SEQ // CTX_LEN, mod_row=lambda b: batch,
                               name=f"fnet_pos_ctx_l{layer}")
    return hs


def kernel(x, c, ctx, c_ctx, mod_w, mod_b, norm_g, ffn1_w_gu, ffn1_w_down, ffn2_w_gu, ffn2_w_down,
           hgrn_w_in, hgrn_lb_logits, hgrn_g_norm, hgrn_w_out,
           mla_w_dqkv, mla_q_norm, mla_kv_norm, mla_w_uq, mla_w_ukv, mla_w_o,
           fnet_w_out, final_g):
    B, T, D = x.shape
    n_lat = B * T
    n_ctx = B * CTX_LEN
    tm, tf = 512, 512
    rows_all = n_lat + n_ctx

    lb = jnp.cumsum(jax.nn.softmax(hgrn_lb_logits.astype(jnp.float32), axis=1), axis=1)
    lb = lb - lb[:, :1]
    rope_cos, rope_sin = mla_rope_tables(T)

    cc = jnp.concatenate([c, c_ctx[None, :], jnp.zeros((MOD_ROWS - B - 1, D), F32)], axis=0)
    mods = modulation(cc, mod_w, mod_b).reshape(DEPTH, MOD_ROWS, N_MOD, D)

    w1_gu, w1_down = ffn1_w_gu.astype(BF16), ffn1_w_down.astype(BF16)
    w2_gu, w2_down = ffn2_w_gu.astype(BF16), ffn2_w_down.astype(BF16)
    norm_g3 = norm_g.reshape(DEPTH * N_SUB, 1, D)
    final_g2 = final_g.reshape(1, D)

    hs = jnp.concatenate([x.reshape(n_lat, D), ctx.reshape(n_ctx, D)], axis=0)
    for i in range(DEPTH):
        kind, j = i % N_MIXERS, i // N_MIXERS
        last = i == DEPTH - 1
        ctx_in = not (last and kind == 2)
        hs = ffn_sublayer(hs, mods, norm_g3, w1_gu, w1_down, final_g2, layer=i, s=0,
                          rows=rows_all if ctx_in else n_lat, tm=tm, tf=tf)
        if kind == 0:
            hs = hgrn_layer(hs, mods, norm_g3, hgrn_w_in[j].astype(BF16), lb[0, j], lb[1, j], hgrn_g_norm[j],
                            hgrn_w_out[j].astype(BF16), layer=i, batch=B, with_ctx_out=not last)
        elif kind == 1:
            hs = mla_layer(hs, mods, norm_g3, mla_weights(mla_w_dqkv[j], mla_w_uq[j], mla_w_ukv[j]),
                           mla_q_norm[j].reshape(1, -1), mla_kv_norm[j].reshape(1, -1),
                           mla_w_o[j].astype(BF16), rope_cos, rope_sin, layer=i, batch=B, with_ctx_out=not last)
        else:
            hs = fnet_layer(hs, mods, norm_g3, fnet_w_out[j].astype(BF16), layer=i, batch=B, with_ctx_out=not last)
        hs = ffn_sublayer(hs, mods, norm_g3, w2_gu, w2_down, final_g2, layer=i, s=2,
                          rows=n_lat if last else rows_all, tm=tm, tf=tf, final=last)
    return hs.reshape(B, T, D)
```

```python
import functools

import numpy as np
import jax
import jax.numpy as jnp
from jax import lax
from jax.experimental import pallas as pl
from jax.experimental.pallas import tpu as pltpu

D_MODEL = 2048
SEQ = 4096
DEPTH = 4
GRID_W = 64
CTX_LEN = 256
N_MIXERS = 3
N_SUB = 3
D_FF = 5632
RMS_EPS = 1e-6

HG_HEAD_DIM = 128
HG_HEADS = D_MODEL // HG_HEAD_DIM
HG_QF = HG_HEADS * HG_HEAD_DIM
HG_IV = HG_HEADS * HG_HEAD_DIM
HG_CHUNK = 16

MLA_HEADS = 16
MLA_Q_RANK = 512
MLA_KV_RANK = 512
MLA_NOPE = 128
MLA_ROPE = 64
MLA_V = 128
MLA_SCALE = (MLA_NOPE + MLA_ROPE) ** -0.5
ATTN_BLOCK = 128
ROPE_THETA = 10000.0
ROPE_FREQS = MLA_ROPE // 4

FOURIER_GROUPS = 8

BF16 = jnp.bfloat16
F32 = jnp.float32

VMEM_LIMIT_BYTES = 56 * 1024 * 1024
MOD_ROWS = 8
N_MOD = N_SUB * 3


def _mod_row(t, tm):
    return jnp.minimum(t // (SEQ // tm), 2)


def _mod_kernel(c_ref, w_ref, b_ref, o_ref):
    c = c_ref[...]
    a = (c * jax.nn.sigmoid(c)).astype(BF16)
    o_ref[...] = jnp.dot(a, w_ref[...].astype(BF16), preferred_element_type=F32) + b_ref[...]


def modulation(cc, mod_w, mod_b, *, tn=1024):
    depth, d, n = mod_w.shape
    return pl.pallas_call(
        _mod_kernel,
        grid=(depth, n // tn),
        in_specs=[
            pl.BlockSpec((MOD_ROWS, d), lambda i, j: (0, 0)),
            pl.BlockSpec((None, d, tn), lambda i, j: (i, 0, j)),
            pl.BlockSpec((None, 1, tn), lambda i, j: (i, 0, j)),
        ],
        out_specs=pl.BlockSpec((None, MOD_ROWS, tn), lambda i, j: (i, 0, j)),
        out_shape=jax.ShapeDtypeStruct((depth, MOD_ROWS, n), F32),
        compiler_params=pltpu.CompilerParams(
            dimension_semantics=("arbitrary", "arbitrary"), vmem_limit_bytes=VMEM_LIMIT_BYTES),
        name="modulation",
    )(cc, mod_w, mod_b.reshape(depth, 1, n))


def _adanorm(x, g, shift, scale):
    y = x * lax.rsqrt(jnp.mean(x * x, axis=-1, keepdims=True) + RMS_EPS)
    return (y * g) * (1.0 + scale) + shift


def _ffn_kernel(x_ref, mod_ref, g_ref, wg_ref, wu_ref, wd_ref, fg_ref, o_ref, n_ref, *, s, nf, final):
    f = pl.program_id(1)

    @pl.when(f == 0)
    def _():
        n = _adanorm(x_ref[...], g_ref[...], mod_ref[3 * s:3 * s + 1, :], mod_ref[3 * s + 1:3 * s + 2, :])
        n_ref[...] = n.astype(BF16)

    n = n_ref[...]
    gate = jnp.dot(n, wg_ref[...], preferred_element_type=F32)
    up = jnp.dot(n, wu_ref[...], preferred_element_type=F32)
    act = (gate * jax.nn.sigmoid(gate) * up).astype(BF16)
    part = jnp.dot(act, wd_ref[...], preferred_element_type=F32)

    @pl.when(f == 0)
    def _():
        o_ref[...] = part

    @pl.when(f > 0)
    def _():
        o_ref[...] += part

    @pl.when(f == nf - 1)
    def _():
        h = x_ref[...] + (0.5 * mod_ref[3 * s + 2:3 * s + 3, :]) * o_ref[...]
        if final:
            h = h * lax.rsqrt(jnp.mean(h * h, axis=-1, keepdims=True) + RMS_EPS) * fg_ref[...]
        o_ref[...] = h


def ffn_sublayer(h, mods, g, w_gu, w_down, final_g, *, layer, s, rows, tm, tf, final=False):
    d = h.shape[1]
    nf = D_FF // tf
    kern = functools.partial(_ffn_kernel, s=s, nf=nf, final=final)
    return pl.pallas_call(
        kern,
        grid=(rows // tm, nf),
        in_specs=[
            pl.BlockSpec((tm, d), lambda t, f: (t, 0)),
            pl.BlockSpec((None, None, N_MOD, d), lambda t, f: (layer, _mod_row(t, tm), 0, 0)),
            pl.BlockSpec((None, 1, d), lambda t, f: (layer * N_SUB + s, 0, 0)),
            pl.BlockSpec((None, d, tf), lambda t, f: (layer, 0, f)),
            pl.BlockSpec((None, d, tf), lambda t, f: (layer, 0, f + nf)),
            pl.BlockSpec((None, tf, d), lambda t, f: (layer, f, 0)),
            pl.BlockSpec((1, d), lambda t, f: (0, 0)),
        ],
        out_specs=pl.BlockSpec((tm, d), lambda t, f: (t, 0)),
        out_shape=jax.ShapeDtypeStruct((rows if final else h.shape[0], d), F32),
        scratch_shapes=[pltpu.VMEM((tm, d), BF16)],
        compiler_params=pltpu.CompilerParams(
            dimension_semantics=("arbitrary", "arbitrary"), vmem_limit_bytes=VMEM_LIMIT_BYTES),
        name=f"ffn_l{layer}_s{s}",
    )(h, mods, g, w_gu, w_gu, w_down, final_g)


TMX = 256
S_ALL = SEQ + CTX_LEN
LAT_TILES = SEQ // TMX
CTX_BLOCK = SEQ // TMX


def _resident(shape):
    return pl.BlockSpec(shape, lambda *_: (0,) * len(shape), pipeline_mode=pl.Buffered(1))


def _stream_tiles(batch):
    return batch * (LAT_TILES + CTX_LEN // TMX)


def _tile_batch(t, batch):
    lat = t < batch * LAT_TILES
    return jnp.where(lat, t // LAT_TILES, t - batch * LAT_TILES)


def _tile_block(t, batch):
    return jnp.where(t < batch * LAT_TILES, t % LAT_TILES, CTX_BLOCK)


def _tile_mod_row(t, batch):
    return jnp.where(t < batch * LAT_TILES, t // LAT_TILES, batch)


def _rms(x):
    return x * lax.rsqrt(jnp.mean(x * x, axis=-1, keepdims=True) + RMS_EPS)


MLA_QK = 2 * MLA_NOPE
N_DQ = MLA_Q_RANK + MLA_KV_RANK


def _mla_proj_kernel(x_ref, mod_ref, g_ref, wd_ref, qn_ref, kvn_ref, wuq_ref, wukv_ref, cos_ref, sin_ref,
                     q_ref, k_ref, v_ref):
    n = _adanorm(x_ref[...], g_ref[...], mod_ref[3:4, :], mod_ref[4:5, :]).astype(BF16)
    proj = jnp.dot(n, wd_ref[...], preferred_element_type=F32)
    cq = (_rms(proj[:, :MLA_Q_RANK]) * qn_ref[...]).astype(BF16)
    ckv = (_rms(proj[:, MLA_Q_RANK:N_DQ]) * kvn_ref[...]).astype(BF16)
    cos, sin = cos_ref[...], sin_ref[...]
    kr = proj[:, N_DQ:N_DQ + 128] * cos + proj[:, N_DQ + 128:N_DQ + 256] * sin
    kr = kr.astype(BF16)
    q = jnp.dot(cq, wuq_ref[...], preferred_element_type=F32) * MLA_SCALE
    kv = jnp.dot(ckv, wukv_ref[...], preferred_element_type=F32)
    hn = MLA_HEADS * MLA_NOPE
    for h in range(MLA_HEADS):
        lo = h * 128
        q_ref[h, :, 0:128] = q[:, lo:lo + 128].astype(BF16)
        qr = q[:, hn + lo:hn + lo + 128] * cos + q[:, 2 * hn + lo:2 * hn + lo + 128] * sin
        q_ref[h, :, 128:256] = qr.astype(BF16)
        k_ref[h, :, 0:128] = kv[:, lo:lo + 128].astype(BF16)
        k_ref[h, :, 128:256] = kr
        v_ref[h, :, :] = kv[:, hn + lo:hn + lo + 128].astype(BF16)


def _rope_partner(width):
    idx = jnp.arange(width)
    return idx ^ ROPE_FREQS


def mla_weights(w_dqkv, w_uq, w_ukv):
    d = w_dqkv.shape[0]
    z = jnp.zeros((d, 128 - MLA_ROPE), w_dqkv.dtype)
    kr = w_dqkv[:, N_DQ:]
    wd = jnp.concatenate([w_dqkv[:, :N_DQ], kr, z, kr[:, _rope_partner(MLA_ROPE)], z], axis=1)
    wq = w_uq.reshape(MLA_Q_RANK, MLA_HEADS, MLA_NOPE + MLA_ROPE)
    qr = wq[:, :, MLA_NOPE:]
    zq = jnp.zeros((MLA_Q_RANK, MLA_HEADS, 128 - MLA_ROPE), w_uq.dtype)
    wuq = jnp.concatenate([
        wq[:, :, :MLA_NOPE].reshape(MLA_Q_RANK, -1),
        jnp.concatenate([qr, zq], axis=2).reshape(MLA_Q_RANK, -1),
        jnp.concatenate([qr[:, :, _rope_partner(MLA_ROPE)], zq], axis=2).reshape(MLA_Q_RANK, -1)], axis=1)
    wkv = w_ukv.reshape(MLA_KV_RANK, MLA_HEADS, MLA_NOPE + MLA_V)
    wukv = jnp.concatenate([wkv[:, :, :MLA_NOPE].reshape(MLA_KV_RANK, -1),
                            wkv[:, :, MLA_NOPE:].reshape(MLA_KV_RANK, -1)], axis=1)
    return wd.astype(BF16), wuq.astype(BF16), wukv.astype(BF16)


def mla_rope_tables(n_tokens):
    rows = n_tokens // GRID_W
    r = jnp.broadcast_to(jnp.arange(rows, dtype=F32)[:, None], (rows, GRID_W)).reshape(-1)
    col = jnp.broadcast_to(jnp.arange(GRID_W, dtype=F32)[None, :], (rows, GRID_W)).reshape(-1)
    inv_freq = ROPE_THETA ** (-jnp.arange(ROPE_FREQS, dtype=F32) / ROPE_FREQS)
    ang = jnp.stack([r, col], axis=-1)[..., None] * inv_freq
    cos = jnp.broadcast_to(jnp.cos(ang)[:, :, None, :], (n_tokens, 2, 2, ROPE_FREQS)).reshape(n_tokens, MLA_ROPE)
    sin = jnp.sin(ang)
    sin = jnp.stack([-sin, sin], axis=2).reshape(n_tokens, MLA_ROPE)
    pad = jnp.zeros((n_tokens, 128 - MLA_ROPE), F32)
    cos = jnp.concatenate([cos, pad], axis=1)
    sin = jnp.concatenate([sin, pad], axis=1)
    ctx_cos = jnp.concatenate([jnp.ones((CTX_LEN, MLA_ROPE), F32), jnp.zeros((CTX_LEN, 128 - MLA_ROPE), F32)], axis=1)
    return (jnp.concatenate([cos, ctx_cos], axis=0),
            jnp.concatenate([sin, jnp.zeros((CTX_LEN, 128), F32)], axis=0))


def mla_project(hs, mods, norm_g3, wd, q_norm, kv_norm, wuq, wukv, cos, sin, *, layer, batch):
    d = hs.shape[1]
    bmap = lambda t: (_tile_batch(t, batch), 0, _tile_block(t, batch), 0)
    return pl.pallas_call(
        _mla_proj_kernel,
        grid=(_stream_tiles(batch),),
        in_specs=[
            pl.BlockSpec((TMX, d), lambda t: (t, 0)),
            pl.BlockSpec((None, None, N_MOD, d), lambda t: (layer, _tile_mod_row(t, batch), 0, 0)),
            pl.BlockSpec((None, 1, d), lambda t: (layer * N_SUB + 1, 0, 0)),
            _resident(wd.shape),
            _resident((1, MLA_Q_RANK)),
            _resident((1, MLA_KV_RANK)),
            _resident(wuq.shape),
            _resident(wukv.shape),
            pl.BlockSpec((TMX, 128), lambda t: (_tile_block(t, batch), 0)),
            pl.BlockSpec((TMX, 128), lambda t: (_tile_block(t, batch), 0)),
        ],
        out_specs=[
            pl.BlockSpec((None, MLA_HEADS, TMX, MLA_QK), bmap),
            pl.BlockSpec((None, MLA_HEADS, TMX, MLA_QK), bmap),
            pl.BlockSpec((None, MLA_HEADS, TMX, MLA_V), bmap),
        ],
        out_shape=[
            jax.ShapeDtypeStruct((batch, MLA_HEADS, S_ALL, MLA_QK), BF16),
            jax.ShapeDtypeStruct((batch, MLA_HEADS, S_ALL, MLA_QK), BF16),
            jax.ShapeDtypeStruct((batch, MLA_HEADS, S_ALL, MLA_V), BF16),
        ],
        compiler_params=pltpu.CompilerParams(
            dimension_semantics=("arbitrary",), vmem_limit_bytes=VMEM_LIMIT_BYTES),
        name=f"mla_proj_l{layer}",
    )(hs, mods, norm_g3, wd, q_norm, kv_norm, wuq, wukv, cos, sin)


def _attn_kernel(q_ref, k_ref, v_ref, o_ref):
    s = lax.dot_general(q_ref[...], k_ref[...], (((1,), (1,)), ((), ())), preferred_element_type=F32)
    p = jnp.exp(s - jnp.max(s, axis=-1, keepdims=True))
    l = jnp.sum(p, axis=-1, keepdims=True)
    o = jnp.dot(p.astype(BF16), v_ref[...], preferred_element_type=F32)
    o_ref[...] = (o / l).astype(BF16)


def mla_attention(q, k, v, *, tq, q_block0, n_q, k_rows, k_block, name):
    batch, heads, _, _ = q.shape
    return pl.pallas_call(
        _attn_kernel,
        grid=(batch, heads, n_q),
        in_specs=[
            pl.BlockSpec((None, None, tq, MLA_QK), lambda b, h, i: (b, h, q_block0 + i, 0)),
            pl.BlockSpec((None, None, k_rows, MLA_QK), lambda b, h, i: (b, h, k_block, 0)),
            pl.BlockSpec((None, None, k_rows, MLA_V), lambda b, h, i: (b, h, k_block, 0)),
        ],
        out_specs=pl.BlockSpec((None, tq, MLA_V), lambda b, h, i: (b, i, h)),
        out_shape=jax.ShapeDtypeStruct((batch, n_q * tq, heads * MLA_V), BF16),
        compiler_params=pltpu.CompilerParams(
            dimension_semantics=("arbitrary", "arbitrary", "arbitrary"), vmem_limit_bytes=VMEM_LIMIT_BYTES),
        name=name,
    )(q, k, v)


def _out_proj_kernel(yl_ref, yc_ref, w_ref, x_ref, mod_ref, o_ref, *, n_lat_tiles, ctx_out):
    t = pl.program_id(0)

    def update(y_ref):
        y = jnp.dot(y_ref[...], w_ref[...], preferred_element_type=F32)
        o_ref[...] = x_ref[...] + mod_ref[5:6, :] * y

    @pl.when(t < n_lat_tiles)
    def _():
        update(yl_ref)

    @pl.when(t >= n_lat_tiles)
    def _():
        if ctx_out:
            update(yc_ref)
        else:
            o_ref[...] = x_ref[...]


def mixer_out_proj(y_lat, y_ctx, w, hs, mods, *, layer, batch, name):
    d = hs.shape[1]
    kdim = y_lat.shape[2]
    nl = batch * LAT_TILES
    ctx_out = y_ctx is not None
    kern = functools.partial(_out_proj_kernel, n_lat_tiles=nl, ctx_out=ctx_out)

    def lat_map(t):
        tl = jnp.minimum(t, nl - 1)
        return (tl // LAT_TILES, tl % LAT_TILES, 0)

    return pl.pallas_call(
        kern,
        grid=(_stream_tiles(batch),),
        in_specs=[
            pl.BlockSpec((None, TMX, kdim), lat_map),
            pl.BlockSpec((None, TMX, kdim), lambda t: (jnp.maximum(t - nl, 0), 0, 0)),
            _resident(w.shape),
            pl.BlockSpec((TMX, d), lambda t: (t, 0)),
            pl.BlockSpec((None, None, N_MOD, d), lambda t: (layer, _tile_mod_row(t, batch), 0, 0)),
        ],
        out_specs=pl.BlockSpec((TMX, d), lambda t: (t, 0)),
        out_shape=jax.ShapeDtypeStruct(hs.shape, F32),
        compiler_params=pltpu.CompilerParams(
            dimension_semantics=("arbitrary",), vmem_limit_bytes=VMEM_LIMIT_BYTES),
        name=name,
    )(y_lat, y_ctx if ctx_out else y_lat[:, :CTX_LEN], w, hs, mods)


def mla_layer(hs, mods, norm_g3, weights, q_norm, kv_norm, w_o, cos, sin, *, layer, batch, with_ctx_out):
    wd, wuq, wukv = weights
    q, k, v = mla_project(hs, mods, norm_g3, wd, q_norm, kv_norm, wuq, wukv, cos, sin, layer=layer, batch=batch)
    tq = 512
    att = mla_attention(q, k, v, tq=tq, q_block0=0, n_q=SEQ // tq, k_rows=S_ALL, k_block=0,
                        name=f"mla_attn_lat_l{layer}")
    att_ctx = None
    if with_ctx_out:
        att_ctx = mla_attention(q, k, v, tq=CTX_LEN, q_block0=SEQ // CTX_LEN, n_q=1, k_rows=CTX_LEN,
                                k_block=SEQ // CTX_LEN, name=f"mla_attn_ctx_l{layer}")
    return mixer_out_proj(att, att_ctx, w_o, hs, mods, layer=layer, batch=batch, name=f"mla_out_l{layer}")


HG_WIN = 128
HG_NH = 4
N_TBLK = S_ALL // TMX


def _hgrn_proj_kernel(x_ref, mod_ref, g_ref, w_ref, o_ref):
    n = _adanorm(x_ref[...], g_ref[...], mod_ref[3:4, :], mod_ref[4:5, :]).astype(BF16)
    o_ref[...] = jnp.dot(n, w_ref[...], preferred_element_type=F32)


def hgrn_project(hs, mods, norm_g3, w_in, *, layer, batch):
    d = hs.shape[1]
    n_out = w_in.shape[1]
    return pl.pallas_call(
        _hgrn_proj_kernel,
        grid=(n_out // d, _stream_tiles(batch)),
        in_specs=[
            pl.BlockSpec((TMX, d), lambda j, t: (t, 0)),
            pl.BlockSpec((None, None, N_MOD, d), lambda j, t: (layer, _tile_mod_row(t, batch), 0, 0)),
            pl.BlockSpec((None, 1, d), lambda j, t: (layer * N_SUB + 1, 0, 0)),
            pl.BlockSpec((d, d), lambda j, t: (0, j)),
        ],
        out_specs=pl.BlockSpec((None, TMX, d), lambda j, t: (_tile_batch(t, batch), _tile_block(t, batch), j)),
        out_shape=jax.ShapeDtypeStruct((batch, S_ALL, n_out), F32),
        compiler_params=pltpu.CompilerParams(
            dimension_semantics=("arbitrary", "arbitrary"), vmem_limit_bytes=VMEM_LIMIT_BYTES),
        name=f"hgrn_proj_l{layer}",
    )(hs, mods, norm_g3, w_in)


def _split3(x):
    hi = x.astype(BF16)
    r1 = x - hi.astype(F32)
    mid = r1.astype(BF16)
    lo = (r1 - mid.astype(F32)).astype(BF16)
    return hi, mid, lo


def _hgrn_scan_kernel(q_ref, z_ref, v_ref, lb_ref, om_ref, tri_ref, o_ref,
                      st_ref, qd_ref, kin_ref, kte_ref, vb_ref, tot_ref, *, reverse):
    ts = pl.program_id(2)

    @pl.when(ts == 0)
    def _():
        st_ref[...] = jnp.zeros_like(st_ref)

    z = z_ref[...]
    e = jnp.exp(-jnp.abs(z))
    r = 1.0 / (1.0 + e)
    er = e * r
    pos = z >= 0
    om = om_ref[...]
    f = lb_ref[...] + om * jnp.where(pos, r, er)
    log_f = jnp.log(f)
    k = om * jnp.where(pos, er, r)
    tri = tri_ref[...]
    cum = sum(jnp.dot(tri, piece, preferred_element_type=F32) for piece in _split3(log_f))
    b_inc = cum[:TMX]
    b_rest = cum[TMX:]
    qr = q_ref[...]
    qd_ref[...] = (qr * jax.nn.sigmoid(qr) * jnp.exp(b_inc)).astype(BF16)
    kin_ref[...] = (k * jnp.exp(-b_inc)).astype(BF16)
    kte_ref[...] = (k * jnp.exp(b_rest)).astype(BF16)
    vb_ref[...] = v_ref[...].astype(BF16)
    tot = b_inc + b_rest
    dec_all = []
    for h in range(HG_NH):
        tot_ref[h] = tot[:, h * HG_HEAD_DIM:(h + 1) * HG_HEAD_DIM]
        dec_all.append(jnp.exp(tot_ref[h, pl.ds(0, TMX // HG_CHUNK, stride=HG_CHUNK), :]))

    wi = lax.broadcasted_iota(jnp.int32, (HG_WIN, HG_WIN), 0)
    wj = lax.broadcasted_iota(jnp.int32, (HG_WIN, HG_WIN), 1)
    same_chunk = (wi // HG_CHUNK) == (wj // HG_CHUNK)
    causal = same_chunk & ((wj >= wi) if reverse else (wj <= wi))

    n_win = TMX // HG_WIN
    per_win = HG_WIN // HG_CHUNK
    wins = list(range(n_win - 1, -1, -1) if reverse else range(n_win))
    chunks = list(range(per_win - 1, -1, -1) if reverse else range(per_win))
    def only_chunk(x, c):
        parts = []
        if c > 0:
            parts.append(jnp.zeros((c * HG_CHUNK, x.shape[1]), x.dtype))
        parts.append(x[c * HG_CHUNK:(c + 1) * HG_CHUNK, :])
        if c < per_win - 1:
            parts.append(jnp.zeros(((per_win - 1 - c) * HG_CHUNK, x.shape[1]), x.dtype))
        return jnp.concatenate(parts, axis=0)

    states = [st_ref[h] for h in range(HG_NH)]
    heads = range(HG_NH)
    hcols = [slice(h * HG_HEAD_DIM, (h + 1) * HG_HEAD_DIM) for h in heads]
    for w in wins:
        rows = slice(w * HG_WIN, (w + 1) * HG_WIN)
        o_intra, upd = [], []
        for h in heads:
            qd_w, kin_w, kte_w, v_w = (r[rows, hcols[h]] for r in (qd_ref, kin_ref, kte_ref, vb_ref))
            a = lax.dot_general(qd_w, kin_w, (((1,), (1,)), ((), ())), preferred_element_type=F32)
            a = jnp.where(causal, a, 0.0).astype(BF16)
            o_intra.append(jnp.dot(a, v_w, preferred_element_type=F32))
            v_t = v_ref[rows, hcols[h]].T.astype(BF16)
            rhs = jnp.concatenate([only_chunk(kte_w, c) for c in range(per_win)], axis=1)
            upd.append(jnp.dot(v_t, rhs, preferred_element_type=F32))
        entry = [dict() for _ in heads]
        for h in heads:
            st = states[h]
            for c in chunks:
                entry[h][c] = st.astype(BF16)
                ci = w * per_win + c
                st = st * dec_all[h][ci:ci + 1, :] + upd[h][:, c * HG_HEAD_DIM:(c + 1) * HG_HEAD_DIM]
            states[h] = st
        for h in heads:
            qd_w = qd_ref[rows, hcols[h]]
            lhs = jnp.concatenate([only_chunk(qd_w, c) for c in range(per_win)], axis=1)
            ent = jnp.concatenate([entry[h][c] for c in range(per_win)], axis=1)
            o_inter = lax.dot_general(lhs, ent, (((1,), (1,)), ((), ())), preferred_element_type=F32)
            o_ref[rows, hcols[h]] = o_intra[h] + o_inter
    for h in heads:
        st_ref[h] = states[h]


def _hgrn_cumsum_matrix(reverse):
    r = np.arange(TMX)[:, None]
    c = np.arange(TMX)[None, :]
    same = (r // HG_CHUNK) == (c // HG_CHUNK)
    inc, rest = ((c >= r), (c < r)) if reverse else ((c <= r), (c > r))
    return jnp.asarray(np.concatenate([same & inc, same & rest], axis=0), BF16)


def hgrn_scan(p, lb, om, *, reverse, z_col, layer):
    batch, s_all, n_out = p.shape
    d = n_out // 5
    wcols = HG_NH * HG_HEAD_DIM
    groups = d // wcols

    def tblk(ts):
        lat = (N_TBLK - 1 - ts) if reverse else (ts - 1)
        return jnp.where(ts == 0, CTX_BLOCK, lat)

    col = lambda base: (lambda b, g, ts: (b, tblk(ts), base * groups + g))
    row_spec = pl.BlockSpec((1, wcols), lambda b, g, ts: (0, g))
    kern = functools.partial(_hgrn_scan_kernel, reverse=reverse)
    return pl.pallas_call(
        kern,
        grid=(batch, groups, N_TBLK),
        in_specs=[
            pl.BlockSpec((None, TMX, wcols), col(0)),
            pl.BlockSpec((None, TMX, wcols), col(z_col)),
            pl.BlockSpec((None, TMX, wcols), col(3)),
            row_spec, row_spec, _resident((2 * TMX, TMX)),
        ],
        out_specs=pl.BlockSpec((None, TMX, wcols), lambda b, g, ts: (b, tblk(ts), g)),
        out_shape=jax.ShapeDtypeStruct((batch, s_all, d), F32),
        scratch_shapes=[
            pltpu.VMEM((HG_NH, HG_HEAD_DIM, HG_HEAD_DIM), F32),
            pltpu.VMEM((TMX, wcols), BF16), pltpu.VMEM((TMX, wcols), BF16),
            pltpu.VMEM((TMX, wcols), BF16), pltpu.VMEM((TMX, wcols), BF16),
            pltpu.VMEM((HG_NH, TMX, HG_HEAD_DIM), F32),
        ],
        compiler_params=pltpu.CompilerParams(
            dimension_semantics=("arbitrary", "arbitrary", "arbitrary"), vmem_limit_bytes=VMEM_LIMIT_BYTES),
        name=f"hgrn_scan_{'bwd' if reverse else 'fwd'}_l{layer}",
    )(p, p, p, lb, om, _hgrn_cumsum_matrix(reverse))


def _hgrn_readout_kernel(of_ref, ob_ref, gate_ref, gn_ref, w_ref, x_ref, mod_ref, o_ref, *, n_lat_tiles, ctx_out):
    t = pl.program_id(0)

    def update():
        o = of_ref[...] + ob_ref[...]
        gate = gate_ref[...]
        gs = gate * jax.nn.sigmoid(gate)
        parts = []
        for h in range(HG_HEADS):
            cols = slice(h * HG_HEAD_DIM, (h + 1) * HG_HEAD_DIM)
            parts.append((_rms(o[:, cols]) * gn_ref[:, cols] * gs[:, cols]).astype(BF16))
        y = jnp.dot(jnp.concatenate(parts, axis=1), w_ref[...], preferred_element_type=F32)
        o_ref[...] = x_ref[...] + mod_ref[5:6, :] * y

    if ctx_out:
        update()
    else:
        pl.when(t < n_lat_tiles)(update)

        @pl.when(t >= n_lat_tiles)
        def _():
            o_ref[...] = x_ref[...]


def hgrn_readout(o_f, o_b, p, g_norm, w_out, hs, mods, *, layer, batch, ctx_out):
    d = hs.shape[1]
    seq = lambda t: (_tile_batch(t, batch), _tile_block(t, batch), 0)
    kern = functools.partial(_hgrn_readout_kernel, n_lat_tiles=batch * LAT_TILES, ctx_out=ctx_out)
    return pl.pallas_call(
        kern,
        grid=(_stream_tiles(batch),),
        in_specs=[
            pl.BlockSpec((None, TMX, d), seq),
            pl.BlockSpec((None, TMX, d), seq),
            pl.BlockSpec((None, TMX, d), lambda t: (_tile_batch(t, batch), _tile_block(t, batch), 4)),
            _resident((1, d)),
            _resident(w_out.shape),
            pl.BlockSpec((TMX, d), lambda t: (t, 0)),
            pl.BlockSpec((None, None, N_MOD, d), lambda t: (layer, _tile_mod_row(t, batch), 0, 0)),
        ],
        out_specs=pl.BlockSpec((TMX, d), lambda t: (t, 0)),
        out_shape=jax.ShapeDtypeStruct(hs.shape, F32),
        compiler_params=pltpu.CompilerParams(
            dimension_semantics=("arbitrary",), vmem_limit_bytes=VMEM_LIMIT_BYTES),
        name=f"hgrn_out_l{layer}",
    )(o_f, o_b, p, g_norm, w_out, hs, mods)


def hgrn_layer(hs, mods, norm_g3, w_in, lb_fwd, lb_bwd, g_norm, w_out, *, layer, batch, with_ctx_out):
    p = hgrn_project(hs, mods, norm_g3, w_in, layer=layer, batch=batch)
    outs = []
    for reverse, lb, z_col in ((False, lb_fwd, 1), (True, lb_bwd, 2)):
        lb = lb.reshape(1, -1)
        outs.append(hgrn_scan(p, lb, 1.0 - lb, reverse=reverse, z_col=z_col, layer=layer))
    return hgrn_readout(outs[0], outs[1], p, g_norm.reshape(1, -1), w_out, hs, mods,
                        layer=layer, batch=batch, ctx_out=with_ctx_out)


FN_GC = D_MODEL // FOURIER_GROUPS
FN_TM = 512
FN_TK = 1024


def _fnet_tables(n_pos, tm):
    t = np.arange(n_pos, dtype=np.int64)
    ang = lambda k: 2.0 * np.pi * ((k[:, None] * t[None, :]) % n_pos) / n_pos
    phi = ang(np.arange(tm, dtype=np.int64))
    th = ang(np.arange(0, n_pos, tm, dtype=np.int64))
    sc = n_pos ** -0.5
    rows = np.stack([np.stack([np.cos(th), -np.sin(th)], axis=1),
                     np.stack([-np.sin(th), -np.cos(th)], axis=1)], axis=1) * sc
    rows = rows.reshape(-1, 2, n_pos)
    return (jnp.asarray(rows, F32), jnp.asarray(np.cos(phi), F32), jnp.asarray(np.sin(phi), F32))


def _fnet_channel_table():
    c = np.arange(FN_GC, dtype=np.int64)
    ang = 2.0 * np.pi * ((c[:, None] * c[None, :]) % FN_GC) / FN_GC
    return jnp.asarray(np.concatenate([np.cos(ang), np.sin(ang)], axis=1) * FN_GC ** -0.5, BF16)


def _fnet_chan_kernel(x_ref, mod_ref, g_ref, cs_ref, o_ref):
    n = _adanorm(x_ref[...], g_ref[...], mod_ref[3:4, :], mod_ref[4:5, :]).astype(BF16)
    for g in range(FOURIER_GROUPS):
        cols = slice(g * FN_GC, (g + 1) * FN_GC)
        pq = jnp.dot(n[:, cols], cs_ref[...], preferred_element_type=F32)
        o_ref[0, :, cols] = pq[:, :FN_GC].astype(BF16)
        o_ref[1, :, cols] = pq[:, FN_GC:].astype(BF16)


def fnet_channel_dft(hs, mods, norm_g3, cs, *, layer, batch):
    d = hs.shape[1]
    return pl.pallas_call(
        _fnet_chan_kernel,
        grid=(_stream_tiles(batch),),
        in_specs=[
            pl.BlockSpec((TMX, d), lambda t: (t, 0)),
            pl.BlockSpec((None, None, N_MOD, d), lambda t: (layer, _tile_mod_row(t, batch), 0, 0)),
            pl.BlockSpec((None, 1, d), lambda t: (layer * N_SUB + 1, 0, 0)),
            _resident(cs.shape),
        ],
        out_specs=pl.BlockSpec((None, 2, TMX, d), lambda t: (_tile_batch(t, batch), 0, _tile_block(t, batch), 0)),
        out_shape=jax.ShapeDtypeStruct((batch, 2, S_ALL, d), BF16),
        compiler_params=pltpu.CompilerParams(
            dimension_semantics=("arbitrary",), vmem_limit_bytes=VMEM_LIMIT_BYTES),
        name=f"fnet_chan_l{layer}",
    )(hs, mods, norm_g3, cs)


def _fnet_pos_kernel(rt_ref, cphi_ref, sphi_ref, pq_ref, o_ref, acc_ref, *, nk):
    kk = pl.program_id(2)
    tile = (rt_ref[0:1, :] * cphi_ref[...] + rt_ref[1:2, :] * sphi_ref[...]).astype(BF16)
    part = jnp.dot(tile, pq_ref[...], preferred_element_type=F32)

    @pl.when(kk == 0)
    def _():
        acc_ref[...] = part

    @pl.when(kk > 0)
    def _():
        acc_ref[...] += part

    @pl.when(kk == nk - 1)
    def _():
        o_ref[...] = acc_ref[...].astype(BF16)


def fnet_position_dft(pq, tables, *, n_pos, tm, tk, row_blk0, name):
    batch, _, _, d = pq.shape
    rows, cphi, sphi = tables
    n_m, nkh = n_pos // tm, n_pos // tk
    nk = 2 * nkh
    kern = functools.partial(_fnet_pos_kernel, nk=nk)
    return pl.pallas_call(
        kern,
        grid=(batch, n_m, nk),
        in_specs=[
            pl.BlockSpec((None, 2, tk), lambda b, m, kk: (m * 2 + kk // nkh, 0, kk % nkh)),
            pl.BlockSpec((tm, tk), lambda b, m, kk: (0, kk % nkh)),
            pl.BlockSpec((tm, tk), lambda b, m, kk: (0, kk % nkh)),
            pl.BlockSpec((None, None, tk, d), lambda b, m, kk: (b, kk // nkh, row_blk0 + kk % nkh, 0)),
        ],
        out_specs=pl.BlockSpec((None, tm, d), lambda b, m, kk: (b, m, 0)),
        out_shape=jax.ShapeDtypeStruct((batch, n_pos, d), BF16),
        scratch_shapes=[pltpu.VMEM((tm, d), F32)],
        compiler_params=pltpu.CompilerParams(
            dimension_semantics=("arbitrary", "arbitrary", "arbitrary"), vmem_limit_bytes=VMEM_LIMIT_BYTES),
        name=name,
    )(rows, cphi, sphi, pq)


def fnet_layer(hs, mods, norm_g3, w_out, *, layer, batch, with_ctx_out):
    pq = fnet_channel_dft(hs, mods, norm_g3, _fnet_channel_table(), layer=layer, batch=batch)
    y = fnet_position_dft(pq, _fnet_tables(SEQ, FN_TM), n_pos=SEQ, tm=FN_TM, tk=FN_TK, row_blk0=0,
                          name=f"fnet_pos_lat_l{layer}")
    y_ctx = None
    if with_ctx_out:
        y_ctx = fnet_position_dft(pq, _fnet_tables(CTX_LEN, CTX_LEN), n_pos=CTX_LEN, tm=CTX_LEN, tk=CTX_LEN,
                                  row_blk0=SEQ // CTX_LEN, name=f"fnet_pos_ctx_l{layer}")
    return mixer_out_proj(y, y_ctx, w_out, hs, mods, layer=layer, batch=batch, name=f"fnet_out_l{layer}")


def kernel(x, c, ctx, c_ctx, mod_w, mod_b, norm_g, ffn1_w_gu, ffn1_w_down, ffn2_w_gu, ffn2_w_down,
           hgrn_w_in, hgrn_lb_logits, hgrn_g_norm, hgrn_w_out,
           mla_w_dqkv, mla_q_norm, mla_kv_norm, mla_w_uq, mla_w_ukv, mla_w_o,
           fnet_w_out, final_g):
    B, T, D = x.shape
    n_lat = B * T
    n_ctx = B * CTX_LEN
    tm, tf = 512, 512
    rows_all = n_lat + n_ctx

    lb = jnp.cumsum(jax.nn.softmax(hgrn_lb_logits.astype(jnp.float32), axis=1), axis=1)
    lb = lb - lb[:, :1]
    rope_cos, rope_sin = mla_rope_tables(T)

    cc = jnp.concatenate([c, c_ctx[None, :], jnp.zeros((MOD_ROWS - B - 1, D), F32)], axis=0)
    mods = modulation(cc, mod_w, mod_b).reshape(DEPTH, MOD_ROWS, N_MOD, D)

    w1_gu, w1_down = ffn1_w_gu.astype(BF16), ffn1_w_down.astype(BF16)
    w2_gu, w2_down = ffn2_w_gu.astype(BF16), ffn2_w_down.astype(BF16)
    norm_g3 = norm_g.reshape(DEPTH * N_SUB, 1, D)
    final_g2 = final_g.reshape(1, D)

    hs = jnp.concatenate([x.reshape(n_lat, D), ctx.reshape(n_ctx, D)], axis=0)
    for i in range(DEPTH):
        kind, j = i % N_MIXERS, i // N_MIXERS
        last = i == DEPTH - 1
        ctx_in = not (last and kind == 2)
        hs = ffn_sublayer(hs, mods, norm_g3, w1_gu, w1_down, final_g2, layer=i, s=0,
                          rows=rows_all if ctx_in else n_lat, tm=tm, tf=tf)
        if kind == 0:
            hs = hgrn_layer(hs, mods, norm_g3, hgrn_w_in[j].astype(BF16), lb[0, j], lb[1, j], hgrn_g_norm[j],
                            hgrn_w_out[j].astype(BF16), layer=i, batch=B, with_ctx_out=not last)
        elif kind == 1:
            hs = mla_layer(hs, mods, norm_g3, mla_weights(mla_w_dqkv[j], mla_w_uq[j], mla_w_ukv[j]),
                           mla_q_norm[j].reshape(1, -1), mla_kv_norm[j].reshape(1, -1),
                           mla_w_o[j].astype(BF16), rope_cos, rope_sin, layer=i, batch=B, with_ctx_out=not last)
        else:
            hs = fnet_layer(hs, mods, norm_g3, fnet_w_out[j].astype(BF16), layer=i, batch=B, with_ctx_out=not last)
        hs = ffn_sublayer(hs, mods, norm_g3, w2_gu, w2_down, final_g2, layer=i, s=2,
                          rows=n_lat if last else rows_all, tm=tm, tf=tf, final=last)
    return hs.reshape(B, T, D)
```

```python
import functools

import numpy as np
import jax
import jax.numpy as jnp
from jax import lax
from jax.experimental import pallas as pl
from jax.experimental.pallas import tpu as pltpu

D_MODEL = 2048
SEQ = 4096
DEPTH = 4
GRID_W = 64
CTX_LEN = 256
N_MIXERS = 3
N_SUB = 3
D_FF = 5632
RMS_EPS = 1e-6

HG_HEAD_DIM = 128
HG_HEADS = D_MODEL // HG_HEAD_DIM
HG_QF = HG_HEADS * HG_HEAD_DIM
HG_IV = HG_HEADS * HG_HEAD_DIM
HG_CHUNK = 16

MLA_HEADS = 16
MLA_Q_RANK = 512
MLA_KV_RANK = 512
MLA_NOPE = 128
MLA_ROPE = 64
MLA_V = 128
MLA_SCALE = (MLA_NOPE + MLA_ROPE) ** -0.5
ATTN_BLOCK = 128
ROPE_THETA = 10000.0
ROPE_FREQS = MLA_ROPE // 4

FOURIER_GROUPS = 8

BF16 = jnp.bfloat16
F32 = jnp.float32

VMEM_LIMIT_BYTES = 56 * 1024 * 1024
MOD_ROWS = 8
N_MOD = N_SUB * 3


def _mod_row(t, tm):
    return jnp.minimum(t // (SEQ // tm), 2)


def _mod_kernel(c_ref, w_ref, b_ref, o_ref):
    c = c_ref[...]
    a = (c * jax.nn.sigmoid(c)).astype(BF16)
    o_ref[...] = jnp.dot(a, w_ref[...].astype(BF16), preferred_element_type=F32) + b_ref[...]


def modulation(cc, mod_w, mod_b, *, tn=1024):
    depth, d, n = mod_w.shape
    return pl.pallas_call(
        _mod_kernel,
        grid=(depth, n // tn),
        in_specs=[
            pl.BlockSpec((MOD_ROWS, d), lambda i, j: (0, 0)),
            pl.BlockSpec((None, d, tn), lambda i, j: (i, 0, j)),
            pl.BlockSpec((None, 1, tn), lambda i, j: (i, 0, j)),
        ],
        out_specs=pl.BlockSpec((None, MOD_ROWS, tn), lambda i, j: (i, 0, j)),
        out_shape=jax.ShapeDtypeStruct((depth, MOD_ROWS, n), F32),
        compiler_params=pltpu.CompilerParams(
            dimension_semantics=("arbitrary", "arbitrary"), vmem_limit_bytes=VMEM_LIMIT_BYTES),
        name="modulation",
    )(cc, mod_w, mod_b.reshape(depth, 1, n))


def _adanorm(x, g, shift, scale):
    y = x * lax.rsqrt(jnp.mean(x * x, axis=-1, keepdims=True) + RMS_EPS)
    return (y * g) * (1.0 + scale) + shift


def _ffn_kernel(x_ref, mod_ref, g_ref, wg_ref, wu_ref, wd_ref, fg_ref, o_ref, n_ref, *, s, nf, final):
    f = pl.program_id(1)

    @pl.when(f == 0)
    def _():
        n = _adanorm(x_ref[...], g_ref[...], mod_ref[3 * s:3 * s + 1, :], mod_ref[3 * s + 1:3 * s + 2, :])
        n_ref[...] = n.astype(BF16)
        o_ref[...] = jnp.zeros_like(o_ref)

    n = n_ref[...]
    gate = jnp.dot(n, wg_ref[...], preferred_element_type=F32)
    up = jnp.dot(n, wu_ref[...], preferred_element_type=F32)
    act = (gate * jax.nn.sigmoid(gate) * up).astype(BF16)
    o_ref[...] += jnp.dot(act, wd_ref[...], preferred_element_type=F32)

    @pl.when(f == nf - 1)
    def _():
        h = x_ref[...] + (0.5 * mod_ref[3 * s + 2:3 * s + 3, :]) * o_ref[...]
        if final:
            h = h * lax.rsqrt(jnp.mean(h * h, axis=-1, keepdims=True) + RMS_EPS) * fg_ref[...]
        o_ref[...] = h


def ffn_sublayer(h, mods, g, w_gu, w_down, final_g, *, layer, s, rows, tm, tf, final=False):
    d = h.shape[1]
    nf = D_FF // tf
    kern = functools.partial(_ffn_kernel, s=s, nf=nf, final=final)
    return pl.pallas_call(
        kern,
        grid=(rows // tm, nf),
        in_specs=[
            pl.BlockSpec((tm, d), lambda t, f: (t, 0)),
            pl.BlockSpec((None, None, N_MOD, d), lambda t, f: (layer, _mod_row(t, tm), 0, 0)),
            pl.BlockSpec((None, 1, d), lambda t, f: (layer * N_SUB + s, 0, 0)),
            pl.BlockSpec((None, d, tf), lambda t, f: (layer, 0, f)),
            pl.BlockSpec((None, d, tf), lambda t, f: (layer, 0, f + nf)),
            pl.BlockSpec((None, tf, d), lambda t, f: (layer, f, 0)),
            pl.BlockSpec((1, d), lambda t, f: (0, 0)),
        ],
        out_specs=pl.BlockSpec((tm, d), lambda t, f: (t, 0)),
        out_shape=jax.ShapeDtypeStruct((rows if final else h.shape[0], d), F32),
        scratch_shapes=[pltpu.VMEM((tm, d), BF16)],
        compiler_params=pltpu.CompilerParams(
            dimension_semantics=("arbitrary", "arbitrary"), vmem_limit_bytes=VMEM_LIMIT_BYTES),
        name=f"ffn_l{layer}_s{s}",
    )(h, mods, g, w_gu, w_gu, w_down, final_g)


TMX = 256
S_ALL = SEQ + CTX_LEN
LAT_TILES = SEQ // TMX
CTX_BLOCK = SEQ // TMX


def _resident(shape):
    return pl.BlockSpec(shape, lambda *_: (0,) * len(shape), pipeline_mode=pl.Buffered(1))


def _stream_tiles(batch):
    return batch * (LAT_TILES + CTX_LEN // TMX)


def _tile_batch(t, batch):
    lat = t < batch * LAT_TILES
    return jnp.where(lat, t // LAT_TILES, t - batch * LAT_TILES)


def _tile_block(t, batch):
    return jnp.where(t < batch * LAT_TILES, t % LAT_TILES, CTX_BLOCK)


def _tile_mod_row(t, batch):
    return jnp.where(t < batch * LAT_TILES, t // LAT_TILES, batch)


def _rms(x):
    return x * lax.rsqrt(jnp.mean(x * x, axis=-1, keepdims=True) + RMS_EPS)


MLA_QK = 2 * MLA_NOPE
LOG2_E = 1.4426950408889634
N_DQ = MLA_Q_RANK + MLA_KV_RANK


def _mla_proj_kernel(x_ref, mod_ref, g_ref, wd_ref, qn_ref, kvn_ref, wuq_ref, wukv_ref, cos_ref, sin_ref,
                     q_ref, k_ref, v_ref):
    n = _adanorm(x_ref[...], g_ref[...], mod_ref[3:4, :], mod_ref[4:5, :]).astype(BF16)
    proj = jnp.dot(n, wd_ref[...], preferred_element_type=F32)
    cq = (_rms(proj[:, :MLA_Q_RANK]) * qn_ref[...]).astype(BF16)
    ckv = (_rms(proj[:, MLA_Q_RANK:N_DQ]) * kvn_ref[...]).astype(BF16)
    cos, sin = cos_ref[...], sin_ref[...]
    kr = proj[:, N_DQ:N_DQ + 128] * cos + proj[:, N_DQ + 128:N_DQ + 256] * sin
    kr = kr.astype(BF16)
    q = jnp.dot(cq, wuq_ref[...], preferred_element_type=F32) * (MLA_SCALE * LOG2_E)
    kv = jnp.dot(ckv, wukv_ref[...], preferred_element_type=F32)
    hn = MLA_HEADS * MLA_NOPE
    lane = lax.broadcasted_iota(jnp.int32, (x_ref.shape[0], 128), 1)
    ones_col = jnp.where(lane == 0, 1.0, 0.0).astype(BF16)
    for h in range(MLA_HEADS):
        lo = h * 128
        q_ref[h, :, 0:128] = q[:, lo:lo + 128].astype(BF16)
        qr = q[:, hn + lo:hn + lo + 128] * cos + q[:, 2 * hn + lo:2 * hn + lo + 128] * sin
        q_ref[h, :, 128:256] = qr.astype(BF16)
        k_ref[h, :, 0:128] = kv[:, lo:lo + 128].astype(BF16)
        k_ref[h, :, 128:256] = kr
        v_ref[h, :, 0:128] = kv[:, hn + lo:hn + lo + 128].astype(BF16)
        v_ref[h, :, 128:256] = ones_col


def _rope_partner(width):
    idx = jnp.arange(width)
    return idx ^ ROPE_FREQS


def mla_weights(w_dqkv, w_uq, w_ukv):
    d = w_dqkv.shape[0]
    z = jnp.zeros((d, 128 - MLA_ROPE), w_dqkv.dtype)
    kr = w_dqkv[:, N_DQ:]
    wd = jnp.concatenate([w_dqkv[:, :N_DQ], kr, z, kr[:, _rope_partner(MLA_ROPE)], z], axis=1)
    wq = w_uq.reshape(MLA_Q_RANK, MLA_HEADS, MLA_NOPE + MLA_ROPE)
    qr = wq[:, :, MLA_NOPE:]
    zq = jnp.zeros((MLA_Q_RANK, MLA_HEADS, 128 - MLA_ROPE), w_uq.dtype)
    wuq = jnp.concatenate([
        wq[:, :, :MLA_NOPE].reshape(MLA_Q_RANK, -1),
        jnp.concatenate([qr, zq], axis=2).reshape(MLA_Q_RANK, -1),
        jnp.concatenate([qr[:, :, _rope_partner(MLA_ROPE)], zq], axis=2).reshape(MLA_Q_RANK, -1)], axis=1)
    wkv = w_ukv.reshape(MLA_KV_RANK, MLA_HEADS, MLA_NOPE + MLA_V)
    wukv = jnp.concatenate([wkv[:, :, :MLA_NOPE].reshape(MLA_KV_RANK, -1),
                            wkv[:, :, MLA_NOPE:].reshape(MLA_KV_RANK, -1)], axis=1)
    return wd.astype(BF16), wuq.astype(BF16), wukv.astype(BF16)


def mla_rope_tables(n_tokens):
    rows = n_tokens // GRID_W
    r = jnp.broadcast_to(jnp.arange(rows, dtype=F32)[:, None], (rows, GRID_W)).reshape(-1)
    col = jnp.broadcast_to(jnp.arange(GRID_W, dtype=F32)[None, :], (rows, GRID_W)).reshape(-1)
    inv_freq = ROPE_THETA ** (-jnp.arange(ROPE_FREQS, dtype=F32) / ROPE_FREQS)
    ang = jnp.stack([r, col], axis=-1)[..., None] * inv_freq
    cos = jnp.broadcast_to(jnp.cos(ang)[:, :, None, :], (n_tokens, 2, 2, ROPE_FREQS)).reshape(n_tokens, MLA_ROPE)
    sin = jnp.sin(ang)
    sin = jnp.stack([-sin, sin], axis=2).reshape(n_tokens, MLA_ROPE)
    pad = jnp.zeros((n_tokens, 128 - MLA_ROPE), F32)
    cos = jnp.concatenate([cos, pad], axis=1)
    sin = jnp.concatenate([sin, pad], axis=1)
    ctx_cos = jnp.concatenate([jnp.ones((CTX_LEN, MLA_ROPE), F32), jnp.zeros((CTX_LEN, 128 - MLA_ROPE), F32)], axis=1)
    return (jnp.concatenate([cos, ctx_cos], axis=0),
            jnp.concatenate([sin, jnp.zeros((CTX_LEN, 128), F32)], axis=0))


def mla_project(hs, mods, norm_g3, wd, q_norm, kv_norm, wuq, wukv, cos, sin, *, layer, batch):
    d = hs.shape[1]
    bmap = lambda t: (_tile_batch(t, batch), 0, _tile_block(t, batch), 0)
    return pl.pallas_call(
        _mla_proj_kernel,
        grid=(_stream_tiles(batch),),
        in_specs=[
            pl.BlockSpec((TMX, d), lambda t: (t, 0)),
            pl.BlockSpec((None, None, N_MOD, d), lambda t: (layer, _tile_mod_row(t, batch), 0, 0)),
            pl.BlockSpec((None, 1, d), lambda t: (layer * N_SUB + 1, 0, 0)),
            _resident(wd.shape),
            _resident((1, MLA_Q_RANK)),
            _resident((1, MLA_KV_RANK)),
            _resident(wuq.shape),
            _resident(wukv.shape),
            pl.BlockSpec((TMX, 128), lambda t: (_tile_block(t, batch), 0)),
            pl.BlockSpec((TMX, 128), lambda t: (_tile_block(t, batch), 0)),
        ],
        out_specs=[
            pl.BlockSpec((None, MLA_HEADS, TMX, MLA_QK), bmap),
            pl.BlockSpec((None, MLA_HEADS, TMX, MLA_QK), bmap),
            pl.BlockSpec((None, MLA_HEADS, TMX, MLA_QK), bmap),
        ],
        out_shape=[
            jax.ShapeDtypeStruct((batch, MLA_HEADS, S_ALL, MLA_QK), BF16),
            jax.ShapeDtypeStruct((batch, MLA_HEADS, S_ALL, MLA_QK), BF16),
            jax.ShapeDtypeStruct((batch, MLA_HEADS, S_ALL, MLA_QK), BF16),
        ],
        compiler_params=pltpu.CompilerParams(
            dimension_semantics=("arbitrary",), vmem_limit_bytes=VMEM_LIMIT_BYTES),
        name=f"mla_proj_l{layer}",
    )(hs, mods, norm_g3, wd, q_norm, kv_norm, wuq, wukv, cos, sin)


def _attn_kernel(q_ref, k_ref, v_ref, o_ref, *, n_chain):
    rows = q_ref.shape[0] // n_chain
    for c in range(n_chain):
        r = slice(c * rows, (c + 1) * rows)
        s = lax.dot_general(q_ref[r, :], k_ref[...], (((1,), (1,)), ((), ())), preferred_element_type=F32)
        p = jnp.exp2(s - jnp.max(s, axis=-1, keepdims=True))
        o = jnp.dot(p.astype(BF16), v_ref[...], preferred_element_type=F32)
        o_ref[r, :] = (o[:, :MLA_V] / o[:, MLA_V:MLA_V + 1]).astype(BF16)


def mla_attention(q, k, v, *, tq, n_chain, q_block0, n_q, k_rows, k_block, name):
    batch, heads, _, _ = q.shape
    return pl.pallas_call(
        functools.partial(_attn_kernel, n_chain=n_chain),
        grid=(batch, heads, n_q),
        in_specs=[
            pl.BlockSpec((None, None, tq, MLA_QK), lambda b, h, i: (b, h, q_block0 + i, 0)),
            pl.BlockSpec((None, None, k_rows, MLA_QK), lambda b, h, i: (b, h, k_block, 0)),
            pl.BlockSpec((None, None, k_rows, MLA_QK), lambda b, h, i: (b, h, k_block, 0)),
        ],
        out_specs=pl.BlockSpec((None, tq, MLA_V), lambda b, h, i: (b, i, h)),
        out_shape=jax.ShapeDtypeStruct((batch, n_q * tq, heads * MLA_V), BF16),
        compiler_params=pltpu.CompilerParams(
            dimension_semantics=("arbitrary", "arbitrary", "arbitrary"), vmem_limit_bytes=VMEM_LIMIT_BYTES),
        name=name,
    )(q, k, v)


def _out_proj_kernel(yl_ref, yc_ref, w_ref, x_ref, mod_ref, o_ref, *, n_lat_tiles, ctx_out):
    t = pl.program_id(0)

    def update(y_ref):
        y = jnp.dot(y_ref[...], w_ref[...], preferred_element_type=F32)
        o_ref[...] = x_ref[...] + mod_ref[5:6, :] * y

    @pl.when(t < n_lat_tiles)
    def _():
        update(yl_ref)

    @pl.when(t >= n_lat_tiles)
    def _():
        if ctx_out:
            update(yc_ref)
        else:
            o_ref[...] = x_ref[...]


def mixer_out_proj(y_lat, y_ctx, w, hs, mods, *, layer, batch, name):
    d = hs.shape[1]
    kdim = y_lat.shape[2]
    nl = batch * LAT_TILES
    ctx_out = y_ctx is not None
    kern = functools.partial(_out_proj_kernel, n_lat_tiles=nl, ctx_out=ctx_out)

    def lat_map(t):
        tl = jnp.minimum(t, nl - 1)
        return (tl // LAT_TILES, tl % LAT_TILES, 0)

    return pl.pallas_call(
        kern,
        grid=(_stream_tiles(batch),),
        in_specs=[
            pl.BlockSpec((None, TMX, kdim), lat_map),
            pl.BlockSpec((None, TMX, kdim), lambda t: (jnp.maximum(t - nl, 0), 0, 0)),
            _resident(w.shape),
            pl.BlockSpec((TMX, d), lambda t: (t, 0)),
            pl.BlockSpec((None, None, N_MOD, d), lambda t: (layer, _tile_mod_row(t, batch), 0, 0)),
        ],
        out_specs=pl.BlockSpec((TMX, d), lambda t: (t, 0)),
        out_shape=jax.ShapeDtypeStruct(hs.shape, F32),
        compiler_params=pltpu.CompilerParams(
            dimension_semantics=("arbitrary",), vmem_limit_bytes=VMEM_LIMIT_BYTES),
        name=name,
    )(y_lat, y_ctx if ctx_out else y_lat[:, :CTX_LEN], w, hs, mods)


def mla_layer(hs, mods, norm_g3, weights, q_norm, kv_norm, w_o, cos, sin, *, layer, batch, with_ctx_out):
    wd, wuq, wukv = weights
    q, k, v = mla_project(hs, mods, norm_g3, wd, q_norm, kv_norm, wuq, wukv, cos, sin, layer=layer, batch=batch)
    tq = 4096
    att = mla_attention(q, k, v, tq=tq, n_chain=16, q_block0=0, n_q=SEQ // tq, k_rows=S_ALL, k_block=0,
                        name=f"mla_attn_lat_l{layer}")
    att_ctx = None
    if with_ctx_out:
        att_ctx = mla_attention(q, k, v, tq=CTX_LEN, n_chain=1, q_block0=SEQ // CTX_LEN, n_q=1, k_rows=CTX_LEN,
                                k_block=SEQ // CTX_LEN, name=f"mla_attn_ctx_l{layer}")
    return mixer_out_proj(att, att_ctx, w_o, hs, mods, layer=layer, batch=batch, name=f"mla_out_l{layer}")


HG_WIN = 128
HG_NH = 4
N_TBLK = S_ALL // TMX


def _hgrn_proj_kernel(x_ref, mod_ref, g_ref, w_ref, o_ref):
    n = _adanorm(x_ref[...], g_ref[...], mod_ref[3:4, :], mod_ref[4:5, :]).astype(BF16)
    o_ref[...] = jnp.dot(n, w_ref[...], preferred_element_type=F32)


def hgrn_project(hs, mods, norm_g3, w_in, *, layer, batch):
    d = hs.shape[1]
    n_out = w_in.shape[1]
    return pl.pallas_call(
        _hgrn_proj_kernel,
        grid=(n_out // d, _stream_tiles(batch)),
        in_specs=[
            pl.BlockSpec((TMX, d), lambda j, t: (t, 0)),
            pl.BlockSpec((None, None, N_MOD, d), lambda j, t: (layer, _tile_mod_row(t, batch), 0, 0)),
            pl.BlockSpec((None, 1, d), lambda j, t: (layer * N_SUB + 1, 0, 0)),
            pl.BlockSpec((d, d), lambda j, t: (0, j)),
        ],
        out_specs=pl.BlockSpec((None, TMX, d), lambda j, t: (_tile_batch(t, batch), _tile_block(t, batch), j)),
        out_shape=jax.ShapeDtypeStruct((batch, S_ALL, n_out), F32),
        compiler_params=pltpu.CompilerParams(
            dimension_semantics=("arbitrary", "arbitrary"), vmem_limit_bytes=VMEM_LIMIT_BYTES),
        name=f"hgrn_proj_l{layer}",
    )(hs, mods, norm_g3, w_in)


def _split3(x):
    hi = x.astype(BF16)
    r1 = x - hi.astype(F32)
    mid = r1.astype(BF16)
    lo = (r1 - mid.astype(F32)).astype(BF16)
    return hi, mid, lo


def _hgrn_scan_kernel(q_ref, z_ref, v_ref, lb_ref, om_ref, tri_ref, o_ref,
                      st_ref, qd_ref, kin_ref, kte_ref, vb_ref, tot_ref, *, reverse):
    ts = pl.program_id(2)

    @pl.when(ts == 0)
    def _():
        st_ref[...] = jnp.zeros_like(st_ref)

    z = z_ref[...]
    e = jnp.exp(-jnp.abs(z))
    r = 1.0 / (1.0 + e)
    er = e * r
    pos = z >= 0
    om = om_ref[...]
    f = lb_ref[...] + om * jnp.where(pos, r, er)
    log_f = jnp.log(f)
    k = om * jnp.where(pos, er, r)
    tri = tri_ref[...]
    cum = sum(jnp.dot(tri, piece, preferred_element_type=F32) for piece in _split3(log_f))
    b_inc = cum[:TMX]
    b_rest = cum[TMX:]
    qr = q_ref[...]
    qd_ref[...] = (qr * jax.nn.sigmoid(qr) * jnp.exp(b_inc)).astype(BF16)
    kin_ref[...] = (k * jnp.exp(-b_inc)).astype(BF16)
    kte_ref[...] = (k * jnp.exp(b_rest)).astype(BF16)
    vb_ref[...] = v_ref[...].astype(BF16)
    tot = b_inc + b_rest
    dec_all = []
    for h in range(HG_NH):
        tot_ref[h] = tot[:, h * HG_HEAD_DIM:(h + 1) * HG_HEAD_DIM]
        dec_all.append(jnp.exp(tot_ref[h, pl.ds(0, TMX // HG_CHUNK, stride=HG_CHUNK), :]))

    wi = lax.broadcasted_iota(jnp.int32, (HG_WIN, HG_WIN), 0)
    wj = lax.broadcasted_iota(jnp.int32, (HG_WIN, HG_WIN), 1)
    same_chunk = (wi // HG_CHUNK) == (wj // HG_CHUNK)
    causal = same_chunk & ((wj >= wi) if reverse else (wj <= wi))

    n_win = TMX // HG_WIN
    per_win = HG_WIN // HG_CHUNK
    wins = list(range(n_win - 1, -1, -1) if reverse else range(n_win))
    chunks = list(range(per_win - 1, -1, -1) if reverse else range(per_win))
    def only_chunk(x, c):
        parts = []
        if c > 0:
            parts.append(jnp.zeros((c * HG_CHUNK, x.shape[1]), x.dtype))
        parts.append(x[c * HG_CHUNK:(c + 1) * HG_CHUNK, :])
        if c < per_win - 1:
            parts.append(jnp.zeros(((per_win - 1 - c) * HG_CHUNK, x.shape[1]), x.dtype))
        return jnp.concatenate(parts, axis=0)

    states = [st_ref[h] for h in range(HG_NH)]
    heads = range(HG_NH)
    hcols = [slice(h * HG_HEAD_DIM, (h + 1) * HG_HEAD_DIM) for h in heads]
    for w in wins:
        rows = slice(w * HG_WIN, (w + 1) * HG_WIN)
        o_intra, upd = [], []
        for h in heads:
            qd_w, kin_w, kte_w, v_w = (r[rows, hcols[h]] for r in (qd_ref, kin_ref, kte_ref, vb_ref))
            a = lax.dot_general(qd_w, kin_w, (((1,), (1,)), ((), ())), preferred_element_type=F32)
            a = jnp.where(causal, a, 0.0).astype(BF16)
            o_intra.append(jnp.dot(a, v_w, preferred_element_type=F32))
            v_t = v_ref[rows, hcols[h]].T.astype(BF16)
            rhs = jnp.concatenate([only_chunk(kte_w, c) for c in range(per_win)], axis=1)
            upd.append(jnp.dot(v_t, rhs, preferred_element_type=F32))
        entry = [dict() for _ in heads]
        for h in heads:
            st = states[h]
            for c in chunks:
                entry[h][c] = st.astype(BF16)
                ci = w * per_win + c
                st = st * dec_all[h][ci:ci + 1, :] + upd[h][:, c * HG_HEAD_DIM:(c + 1) * HG_HEAD_DIM]
            states[h] = st
        for h in heads:
            qd_w = qd_ref[rows, hcols[h]]
            lhs = jnp.concatenate([only_chunk(qd_w, c) for c in range(per_win)], axis=1)
            ent = jnp.concatenate([entry[h][c] for c in range(per_win)], axis=1)
            o_inter = lax.dot_general(lhs, ent, (((1,), (1,)), ((), ())), preferred_element_type=F32)
            o_ref[rows, hcols[h]] = o_intra[h] + o_inter
    for h in heads:
        st_ref[h] = states[h]


def _hgrn_cumsum_matrix(reverse):
    r = np.arange(TMX)[:, None]
    c = np.arange(TMX)[None, :]
    same = (r // HG_CHUNK) == (c // HG_CHUNK)
    inc, rest = ((c >= r), (c < r)) if reverse else ((c <= r), (c > r))
    return jnp.asarray(np.concatenate([same & inc, same & rest], axis=0), BF16)


def hgrn_scan(p, lb, om, *, reverse, z_col, layer):
    batch, s_all, n_out = p.shape
    d = n_out // 5
    wcols = HG_NH * HG_HEAD_DIM
    groups = d // wcols

    def tblk(ts):
        lat = (N_TBLK - 1 - ts) if reverse else (ts - 1)
        return jnp.where(ts == 0, CTX_BLOCK, lat)

    col = lambda base: (lambda b, g, ts: (b, tblk(ts), base * groups + g))
    row_spec = pl.BlockSpec((1, wcols), lambda b, g, ts: (0, g))
    kern = functools.partial(_hgrn_scan_kernel, reverse=reverse)
    return pl.pallas_call(
        kern,
        grid=(batch, groups, N_TBLK),
        in_specs=[
            pl.BlockSpec((None, TMX, wcols), col(0)),
            pl.BlockSpec((None, TMX, wcols), col(z_col)),
            pl.BlockSpec((None, TMX, wcols), col(3)),
            row_spec, row_spec, _resident((2 * TMX, TMX)),
        ],
        out_specs=pl.BlockSpec((None, TMX, wcols), lambda b, g, ts: (b, tblk(ts), g)),
        out_shape=jax.ShapeDtypeStruct((batch, s_all, d), F32),
        scratch_shapes=[
            pltpu.VMEM((HG_NH, HG_HEAD_DIM, HG_HEAD_DIM), F32),
            pltpu.VMEM((TMX, wcols), BF16), pltpu.VMEM((TMX, wcols), BF16),
            pltpu.VMEM((TMX, wcols), BF16), pltpu.VMEM((TMX, wcols), BF16),
            pltpu.VMEM((HG_NH, TMX, HG_HEAD_DIM), F32),
        ],
        compiler_params=pltpu.CompilerParams(
            dimension_semantics=("arbitrary", "arbitrary", "arbitrary"), vmem_limit_bytes=VMEM_LIMIT_BYTES),
        name=f"hgrn_scan_{'bwd' if reverse else 'fwd'}_l{layer}",
    )(p, p, p, lb, om, _hgrn_cumsum_matrix(reverse))


def _hgrn_readout_kernel(of_ref, ob_ref, gate_ref, gn_ref, w_ref, x_ref, mod_ref, o_ref, *, n_lat_tiles, ctx_out):
    t = pl.program_id(0)

    def update():
        o = of_ref[...] + ob_ref[...]
        gate = gate_ref[...]
        gs = gate * jax.nn.sigmoid(gate)
        parts = []
        for h in range(HG_HEADS):
            cols = slice(h * HG_HEAD_DIM, (h + 1) * HG_HEAD_DIM)
            parts.append((_rms(o[:, cols]) * gn_ref[:, cols] * gs[:, cols]).astype(BF16))
        y = jnp.dot(jnp.concatenate(parts, axis=1), w_ref[...], preferred_element_type=F32)
        o_ref[...] = x_ref[...] + mod_ref[5:6, :] * y

    if ctx_out:
        update()
    else:
        pl.when(t < n_lat_tiles)(update)

        @pl.when(t >= n_lat_tiles)
        def _():
            o_ref[...] = x_ref[...]


def hgrn_readout(o_f, o_b, p, g_norm, w_out, hs, mods, *, layer, batch, ctx_out):
    d = hs.shape[1]
    seq = lambda t: (_tile_batch(t, batch), _tile_block(t, batch), 0)
    kern = functools.partial(_hgrn_readout_kernel, n_lat_tiles=batch * LAT_TILES, ctx_out=ctx_out)
    return pl.pallas_call(
        kern,
        grid=(_stream_tiles(batch),),
        in_specs=[
            pl.BlockSpec((None, TMX, d), seq),
            pl.BlockSpec((None, TMX, d), seq),
            pl.BlockSpec((None, TMX, d), lambda t: (_tile_batch(t, batch), _tile_block(t, batch), 4)),
            _resident((1, d)),
            _resident(w_out.shape),
            pl.BlockSpec((TMX, d), lambda t: (t, 0)),
            pl.BlockSpec((None, None, N_MOD, d), lambda t: (layer, _tile_mod_row(t, batch), 0, 0)),
        ],
        out_specs=pl.BlockSpec((TMX, d), lambda t: (t, 0)),
        out_shape=jax.ShapeDtypeStruct(hs.shape, F32),
        compiler_params=pltpu.CompilerParams(
            dimension_semantics=("arbitrary",), vmem_limit_bytes=VMEM_LIMIT_BYTES),
        name=f"hgrn_out_l{layer}",
    )(o_f, o_b, p, g_norm, w_out, hs, mods)


def hgrn_layer(hs, mods, norm_g3, w_in, lb_fwd, lb_bwd, g_norm, w_out, *, layer, batch, with_ctx_out):
    p = hgrn_project(hs, mods, norm_g3, w_in, layer=layer, batch=batch)
    outs = []
    for reverse, lb, z_col in ((False, lb_fwd, 1), (True, lb_bwd, 2)):
        lb = lb.reshape(1, -1)
        outs.append(hgrn_scan(p, lb, 1.0 - lb, reverse=reverse, z_col=z_col, layer=layer))
    return hgrn_readout(outs[0], outs[1], p, g_norm.reshape(1, -1), w_out, hs, mods,
                        layer=layer, batch=batch, ctx_out=with_ctx_out)


FN_GC = D_MODEL // FOURIER_GROUPS
FN_TM = 512
FN_TK = 1024


def _fnet_tables(n_pos, tm):
    t = np.arange(n_pos, dtype=np.int64)
    ang = lambda k: 2.0 * np.pi * ((k[:, None] * t[None, :]) % n_pos) / n_pos
    phi = ang(np.arange(tm, dtype=np.int64))
    th = ang(np.arange(0, n_pos, tm, dtype=np.int64))
    sc = n_pos ** -0.5
    rows = np.stack([np.stack([np.cos(th), -np.sin(th)], axis=1),
                     np.stack([-np.sin(th), -np.cos(th)], axis=1)], axis=1) * sc
    rows = rows.reshape(-1, 2, n_pos)
    return (jnp.asarray(rows, F32), jnp.asarray(np.cos(phi), F32), jnp.asarray(np.sin(phi), F32))


def _fnet_channel_table():
    c = np.arange(FN_GC, dtype=np.int64)
    ang = 2.0 * np.pi * ((c[:, None] * c[None, :]) % FN_GC) / FN_GC
    return jnp.asarray(np.concatenate([np.cos(ang), np.sin(ang)], axis=1) * FN_GC ** -0.5, BF16)


def _fnet_chan_kernel(x_ref, mod_ref, g_ref, cs_ref, o_ref):
    n = _adanorm(x_ref[...], g_ref[...], mod_ref[3:4, :], mod_ref[4:5, :]).astype(BF16)
    for g in range(FOURIER_GROUPS):
        cols = slice(g * FN_GC, (g + 1) * FN_GC)
        pq = jnp.dot(n[:, cols], cs_ref[...], preferred_element_type=F32)
        o_ref[0, :, cols] = pq[:, :FN_GC].astype(BF16)
        o_ref[1, :, cols] = pq[:, FN_GC:].astype(BF16)


def fnet_channel_dft(hs, mods, norm_g3, cs, *, layer, batch):
    d = hs.shape[1]
    return pl.pallas_call(
        _fnet_chan_kernel,
        grid=(_stream_tiles(batch),),
        in_specs=[
            pl.BlockSpec((TMX, d), lambda t: (t, 0)),
            pl.BlockSpec((None, None, N_MOD, d), lambda t: (layer, _tile_mod_row(t, batch), 0, 0)),
            pl.BlockSpec((None, 1, d), lambda t: (layer * N_SUB + 1, 0, 0)),
            _resident(cs.shape),
        ],
        out_specs=pl.BlockSpec((None, 2, TMX, d), lambda t: (_tile_batch(t, batch), 0, _tile_block(t, batch), 0)),
        out_shape=jax.ShapeDtypeStruct((batch, 2, S_ALL, d), BF16),
        compiler_params=pltpu.CompilerParams(
            dimension_semantics=("arbitrary",), vmem_limit_bytes=VMEM_LIMIT_BYTES),
        name=f"fnet_chan_l{layer}",
    )(hs, mods, norm_g3, cs)


def _fnet_pos_kernel(rt_ref, cphi_ref, sphi_ref, pq_ref, o_ref, acc_ref, *, nk):
    kk = pl.program_id(2)
    tile = (rt_ref[0:1, :] * cphi_ref[...] + rt_ref[1:2, :] * sphi_ref[...]).astype(BF16)

    @pl.when(kk == 0)
    def _():
        acc_ref[...] = jnp.zeros_like(acc_ref)

    acc_ref[...] += jnp.dot(tile, pq_ref[...], preferred_element_type=F32)

    @pl.when(kk == nk - 1)
    def _():
        o_ref[...] = acc_ref[...].astype(BF16)


def fnet_position_dft(pq, tables, *, n_pos, tm, tk, row_blk0, name):
    batch, _, _, d = pq.shape
    rows, cphi, sphi = tables
    n_m, nkh = n_pos // tm, n_pos // tk
    nk = 2 * nkh
    kern = functools.partial(_fnet_pos_kernel, nk=nk)
    return pl.pallas_call(
        kern,
        grid=(batch, n_m, nk),
        in_specs=[
            pl.BlockSpec((None, 2, tk), lambda b, m, kk: (m * 2 + kk // nkh, 0, kk % nkh)),
            pl.BlockSpec((tm, tk), lambda b, m, kk: (0, kk % nkh)),
            pl.BlockSpec((tm, tk), lambda b, m, kk: (0, kk % nkh)),
            pl.BlockSpec((None, None, tk, d), lambda b, m, kk: (b, kk // nkh, row_blk0 + kk % nkh, 0)),
        ],
        out_specs=pl.BlockSpec((None, tm, d), lambda b, m, kk: (b, m, 0)),
        out_shape=jax.ShapeDtypeStruct((batch, n_pos, d), BF16),
        scratch_shapes=[pltpu.VMEM((tm, d), F32)],
        compiler_params=pltpu.CompilerParams(
            dimension_semantics=("arbitrary", "arbitrary", "arbitrary"), vmem_limit_bytes=VMEM_LIMIT_BYTES),
        name=name,
    )(rows, cphi, sphi, pq)


def fnet_layer(hs, mods, norm_g3, w_out, *, layer, batch, with_ctx_out):
    pq = fnet_channel_dft(hs, mods, norm_g3, _fnet_channel_table(), layer=layer, batch=batch)
    y = fnet_position_dft(pq, _fnet_tables(SEQ, FN_TM), n_pos=SEQ, tm=FN_TM, tk=FN_TK, row_blk0=0,
                          name=f"fnet_pos_lat_l{layer}")
    y_ctx = None
    if with_ctx_out:
        y_ctx = fnet_position_dft(pq, _fnet_tables(CTX_LEN, CTX_LEN), n_pos=CTX_LEN, tm=CTX_LEN, tk=CTX_LEN,
                                  row_blk0=SEQ // CTX_LEN, name=f"fnet_pos_ctx_l{layer}")
    return mixer_out_proj(y, y_ctx, w_out, hs, mods, layer=layer, batch=batch, name=f"fnet_out_l{layer}")


def kernel(x, c, ctx, c_ctx, mod_w, mod_b, norm_g, ffn1_w_gu, ffn1_w_down, ffn2_w_gu, ffn2_w_down,
           hgrn_w_in, hgrn_lb_logits, hgrn_g_norm, hgrn_w_out,
           mla_w_dqkv, mla_q_norm, mla_kv_norm, mla_w_uq, mla_w_ukv, mla_w_o,
           fnet_w_out, final_g):
    B, T, D = x.shape
    n_lat = B * T
    n_ctx = B * CTX_LEN
    tm, tf = 512, 512
    rows_all = n_lat + n_ctx

    lb = jnp.cumsum(jax.nn.softmax(hgrn_lb_logits.astype(jnp.float32), axis=1), axis=1)
    lb = lb - lb[:, :1]
    rope_cos, rope_sin = mla_rope_tables(T)

    cc = jnp.concatenate([c, c_ctx[None, :], jnp.zeros((MOD_ROWS - B - 1, D), F32)], axis=0)
    mods = modulation(cc, mod_w, mod_b).reshape(DEPTH, MOD_ROWS, N_MOD, D)

    w1_gu, w1_down = ffn1_w_gu.astype(BF16), ffn1_w_down.astype(BF16)
    w2_gu, w2_down = ffn2_w_gu.astype(BF16), ffn2_w_down.astype(BF16)
    norm_g3 = norm_g.reshape(DEPTH * N_SUB, 1, D)
    final_g2 = final_g.reshape(1, D)

    hs = jnp.concatenate([x.reshape(n_lat, D), ctx.reshape(n_ctx, D)], axis=0)
    for i in range(DEPTH):
        kind, j = i % N_MIXERS, i // N_MIXERS
        last = i == DEPTH - 1
        ctx_in = not (last and kind == 2)
        hs = ffn_sublayer(hs, mods, norm_g3, w1_gu, w1_down, final_g2, layer=i, s=0,
                          rows=rows_all if ctx_in else n_lat, tm=tm, tf=tf)
        if kind == 0:
            hs = hgrn_layer(hs, mods, norm_g3, hgrn_w_in[j].astype(BF16), lb[0, j], lb[1, j], hgrn_g_norm[j],
                            hgrn_w_out[j].astype(BF16), layer=i, batch=B, with_ctx_out=not last)
        elif kind == 1:
            hs = mla_layer(hs, mods, norm_g3, mla_weights(mla_w_dqkv[j], mla_w_uq[j], mla_w_ukv[j]),
                           mla_q_norm[j].reshape(1, -1), mla_kv_norm[j].reshape(1, -1),
                           mla_w_o[j].astype(BF16), rope_cos, rope_sin, layer=i, batch=B, with_ctx_out=not last)
        else:
            hs = fnet_layer(hs, mods, norm_g3, fnet_w_out[j].astype(BF16), layer=i, batch=B, with_ctx_out=not last)
        hs = ffn_sublayer(hs, mods, norm_g3, w2_gu, w2_down, final_g2, layer=i, s=2,
                          rows=n_lat if last else rows_all, tm=tm, tf=tf, final=last)
    return hs.reshape(B, T, D)
```

```python
import functools

import numpy as np
import jax
import jax.numpy as jnp
from jax import lax
from jax.experimental import pallas as pl
from jax.experimental.pallas import tpu as pltpu

D_MODEL = 2048
SEQ = 4096
DEPTH = 4
GRID_W = 64
CTX_LEN = 256
N_MIXERS = 3
N_SUB = 3
D_FF = 5632
RMS_EPS = 1e-6

HG_HEAD_DIM = 128
HG_HEADS = D_MODEL // HG_HEAD_DIM
HG_QF = HG_HEADS * HG_HEAD_DIM
HG_IV = HG_HEADS * HG_HEAD_DIM
HG_CHUNK = 16

MLA_HEADS = 16
MLA_Q_RANK = 512
MLA_KV_RANK = 512
MLA_NOPE = 128
MLA_ROPE = 64
MLA_V = 128
MLA_SCALE = (MLA_NOPE + MLA_ROPE) ** -0.5
ATTN_BLOCK = 128
ROPE_THETA = 10000.0
ROPE_FREQS = MLA_ROPE // 4

FOURIER_GROUPS = 8

BF16 = jnp.bfloat16
F32 = jnp.float32

VMEM_LIMIT_BYTES = 56 * 1024 * 1024
MOD_ROWS = 8
N_MOD = N_SUB * 3


def _mod_row(t, tm):
    return jnp.minimum(t // (SEQ // tm), 2)


def _mod_kernel(c_ref, w_ref, b_ref, o_ref):
    c = c_ref[...]
    a = (c * jax.nn.sigmoid(c)).astype(BF16)
    o_ref[...] = jnp.dot(a, w_ref[...].astype(BF16), preferred_element_type=F32) + b_ref[...]


def modulation(cc, mod_w, mod_b, *, tn=1024):
    depth, d, n = mod_w.shape
    return pl.pallas_call(
        _mod_kernel,
        grid=(depth, n // tn),
        in_specs=[
            pl.BlockSpec((MOD_ROWS, d), lambda i, j: (0, 0)),
            pl.BlockSpec((None, d, tn), lambda i, j: (i, 0, j)),
            pl.BlockSpec((None, 1, tn), lambda i, j: (i, 0, j)),
        ],
        out_specs=pl.BlockSpec((None, MOD_ROWS, tn), lambda i, j: (i, 0, j)),
        out_shape=jax.ShapeDtypeStruct((depth, MOD_ROWS, n), F32),
        compiler_params=pltpu.CompilerParams(
            dimension_semantics=("arbitrary", "arbitrary"), vmem_limit_bytes=VMEM_LIMIT_BYTES),
        name="modulation",
    )(cc, mod_w, mod_b.reshape(depth, 1, n))


def _adanorm(x, g, shift, scale):
    y = x * lax.rsqrt(jnp.mean(x * x, axis=-1, keepdims=True) + RMS_EPS)
    return (y * g) * (1.0 + scale) + shift


FFN_NORM_ROWS = 48


def _ffn_kernel(x_ref, xn_ref, mod_ref, modn_ref, g_ref, wg_ref, wu_ref, wd_ref, fg_ref, o_ref, na_ref, nb_ref,
                *, s, nf, final):
    t, f = pl.program_id(0), pl.program_id(1)
    tm = x_ref.shape[0]

    def norm(x, m_ref):
        return _adanorm(x, g_ref[...], m_ref[3 * s:3 * s + 1, :], m_ref[3 * s + 1:3 * s + 2, :]).astype(BF16)

    @pl.when((t == 0) & (f == 0))
    def _():
        na_ref[...] = norm(x_ref[...], mod_ref)

    @pl.when(f == 0)
    def _():
        o_ref[...] = jnp.zeros_like(o_ref)

    def step(cur_ref, nxt_ref):
        r0 = pl.multiple_of(jnp.minimum(f * FFN_NORM_ROWS, tm - FFN_NORM_ROWS), 16)
        nxt_ref[pl.ds(r0, FFN_NORM_ROWS), :] = norm(xn_ref[pl.ds(r0, FFN_NORM_ROWS), :], modn_ref)
        n = cur_ref[...]
        gate = jnp.dot(n, wg_ref[...], preferred_element_type=F32)
        up = jnp.dot(n, wu_ref[...], preferred_element_type=F32)
        act = (gate * jax.nn.sigmoid(gate) * up).astype(BF16)
        o_ref[...] += jnp.dot(act, wd_ref[...], preferred_element_type=F32)

    pl.when(t % 2 == 0)(functools.partial(step, na_ref, nb_ref))
    pl.when(t % 2 == 1)(functools.partial(step, nb_ref, na_ref))

    @pl.when(f == nf - 1)
    def _():
        h = x_ref[...] + (0.5 * mod_ref[3 * s + 2:3 * s + 3, :]) * o_ref[...]
        if final:
            h = h * lax.rsqrt(jnp.mean(h * h, axis=-1, keepdims=True) + RMS_EPS) * fg_ref[...]
        o_ref[...] = h


def ffn_sublayer(h, mods, g, w_gu, w_down, final_g, *, layer, s, rows, tm, tf, final=False):
    d = h.shape[1]
    nf = D_FF // tf
    nt = rows // tm
    assert nf * FFN_NORM_ROWS >= tm and FFN_NORM_ROWS % 16 == 0
    kern = functools.partial(_ffn_kernel, s=s, nf=nf, final=final)
    nxt = lambda t: jnp.minimum(t + 1, nt - 1)
    return pl.pallas_call(
        kern,
        grid=(nt, nf),
        in_specs=[
            pl.BlockSpec((tm, d), lambda t, f: (t, 0)),
            pl.BlockSpec((tm, d), lambda t, f: (nxt(t), 0)),
            pl.BlockSpec((None, None, N_MOD, d), lambda t, f: (layer, _mod_row(t, tm), 0, 0)),
            pl.BlockSpec((None, None, N_MOD, d), lambda t, f: (layer, _mod_row(nxt(t), tm), 0, 0)),
            pl.BlockSpec((None, 1, d), lambda t, f: (layer * N_SUB + s, 0, 0)),
            pl.BlockSpec((None, d, tf), lambda t, f: (layer, 0, f)),
            pl.BlockSpec((None, d, tf), lambda t, f: (layer, 0, f + nf)),
            pl.BlockSpec((None, tf, d), lambda t, f: (layer, f, 0)),
            pl.BlockSpec((1, d), lambda t, f: (0, 0)),
        ],
        out_specs=pl.BlockSpec((tm, d), lambda t, f: (t, 0)),
        out_shape=jax.ShapeDtypeStruct((rows if final else h.shape[0], d), F32),
        scratch_shapes=[pltpu.VMEM((tm, d), BF16), pltpu.VMEM((tm, d), BF16)],
        compiler_params=pltpu.CompilerParams(
            dimension_semantics=("arbitrary", "arbitrary"), vmem_limit_bytes=VMEM_LIMIT_BYTES),
        name=f"ffn_l{layer}_s{s}",
    )(h, h, mods, mods, g, w_gu, w_gu, w_down, final_g)


TMX = 256
S_ALL = SEQ + CTX_LEN
LAT_TILES = SEQ // TMX
CTX_BLOCK = SEQ // TMX


def _resident(shape):
    return pl.BlockSpec(shape, lambda *_: (0,) * len(shape), pipeline_mode=pl.Buffered(1))


def _stream_tiles(batch):
    return batch * (LAT_TILES + CTX_LEN // TMX)


def _tile_batch(t, batch):
    lat = t < batch * LAT_TILES
    return jnp.where(lat, t // LAT_TILES, t - batch * LAT_TILES)


def _tile_block(t, batch):
    return jnp.where(t < batch * LAT_TILES, t % LAT_TILES, CTX_BLOCK)


def _tile_mod_row(t, batch):
    return jnp.where(t < batch * LAT_TILES, t // LAT_TILES, batch)


def _rms(x):
    return x * lax.rsqrt(jnp.mean(x * x, axis=-1, keepdims=True) + RMS_EPS)


MLA_QK = 2 * MLA_NOPE
LOG2_E = 1.4426950408889634
N_DQ = MLA_Q_RANK + MLA_KV_RANK


def _mla_proj_kernel(x_ref, mod_ref, g_ref, wd_ref, qn_ref, kvn_ref, wuq_ref, wukv_ref, cos_ref, sin_ref,
                     q_ref, k_ref, v_ref):
    n = _adanorm(x_ref[...], g_ref[...], mod_ref[3:4, :], mod_ref[4:5, :]).astype(BF16)
    proj = jnp.dot(n, wd_ref[...], preferred_element_type=F32)
    cq = (_rms(proj[:, :MLA_Q_RANK]) * qn_ref[...]).astype(BF16)
    ckv = (_rms(proj[:, MLA_Q_RANK:N_DQ]) * kvn_ref[...]).astype(BF16)
    cos, sin = cos_ref[...], sin_ref[...]
    kr = proj[:, N_DQ:N_DQ + 128] * cos + proj[:, N_DQ + 128:N_DQ + 256] * sin
    kr = kr.astype(BF16)
    q = jnp.dot(cq, wuq_ref[...], preferred_element_type=F32) * (MLA_SCALE * LOG2_E)
    kv = jnp.dot(ckv, wukv_ref[...], preferred_element_type=F32)
    hn = MLA_HEADS * MLA_NOPE
    lane = lax.broadcasted_iota(jnp.int32, (x_ref.shape[0], 128), 1)
    ones_col = jnp.where(lane == 0, 1.0, 0.0).astype(BF16)
    for h in range(MLA_HEADS):
        lo = h * 128
        q_ref[h, :, 0:128] = q[:, lo:lo + 128].astype(BF16)
        qr = q[:, hn + lo:hn + lo + 128] * cos + q[:, 2 * hn + lo:2 * hn + lo + 128] * sin
        q_ref[h, :, 128:256] = qr.astype(BF16)
        k_ref[h, :, 0:128] = kv[:, lo:lo + 128].astype(BF16)
        k_ref[h, :, 128:256] = kr
        v_ref[h, :, 0:128] = kv[:, hn + lo:hn + lo + 128].astype(BF16)
        v_ref[h, :, 128:256] = ones_col


def _rope_partner(width):
    idx = jnp.arange(width)
    return idx ^ ROPE_FREQS


def mla_weights(w_dqkv, w_uq, w_ukv):
    d = w_dqkv.shape[0]
    z = jnp.zeros((d, 128 - MLA_ROPE), w_dqkv.dtype)
    kr = w_dqkv[:, N_DQ:]
    wd = jnp.concatenate([w_dqkv[:, :N_DQ], kr, z, kr[:, _rope_partner(MLA_ROPE)], z], axis=1)
    wq = w_uq.reshape(MLA_Q_RANK, MLA_HEADS, MLA_NOPE + MLA_ROPE)
    qr = wq[:, :, MLA_NOPE:]
    zq = jnp.zeros((MLA_Q_RANK, MLA_HEADS, 128 - MLA_ROPE), w_uq.dtype)
    wuq = jnp.concatenate([
        wq[:, :, :MLA_NOPE].reshape(MLA_Q_RANK, -1),
        jnp.concatenate([qr, zq], axis=2).reshape(MLA_Q_RANK, -1),
        jnp.concatenate([qr[:, :, _rope_partner(MLA_ROPE)], zq], axis=2).reshape(MLA_Q_RANK, -1)], axis=1)
    wkv = w_ukv.reshape(MLA_KV_RANK, MLA_HEADS, MLA_NOPE + MLA_V)
    wukv = jnp.concatenate([wkv[:, :, :MLA_NOPE].reshape(MLA_KV_RANK, -1),
                            wkv[:, :, MLA_NOPE:].reshape(MLA_KV_RANK, -1)], axis=1)
    return wd.astype(BF16), wuq.astype(BF16), wukv.astype(BF16)


def mla_rope_tables(n_tokens):
    rows = n_tokens // GRID_W
    r = jnp.broadcast_to(jnp.arange(rows, dtype=F32)[:, None], (rows, GRID_W)).reshape(-1)
    col = jnp.broadcast_to(jnp.arange(GRID_W, dtype=F32)[None, :], (rows, GRID_W)).reshape(-1)
    inv_freq = ROPE_THETA ** (-jnp.arange(ROPE_FREQS, dtype=F32) / ROPE_FREQS)
    ang = jnp.stack([r, col], axis=-1)[..., None] * inv_freq
    cos = jnp.broadcast_to(jnp.cos(ang)[:, :, None, :], (n_tokens, 2, 2, ROPE_FREQS)).reshape(n_tokens, MLA_ROPE)
    sin = jnp.sin(ang)
    sin = jnp.stack([-sin, sin], axis=2).reshape(n_tokens, MLA_ROPE)
    pad = jnp.zeros((n_tokens, 128 - MLA_ROPE), F32)
    cos = jnp.concatenate([cos, pad], axis=1)
    sin = jnp.concatenate([sin, pad], axis=1)
    ctx_cos = jnp.concatenate([jnp.ones((CTX_LEN, MLA_ROPE), F32), jnp.zeros((CTX_LEN, 128 - MLA_ROPE), F32)], axis=1)
    return (jnp.concatenate([cos, ctx_cos], axis=0),
            jnp.concatenate([sin, jnp.zeros((CTX_LEN, 128), F32)], axis=0))


def mla_project(hs, mods, norm_g3, wd, q_norm, kv_norm, wuq, wukv, cos, sin, *, layer, batch):
    d = hs.shape[1]
    bmap = lambda t: (_tile_batch(t, batch), 0, _tile_block(t, batch), 0)
    return pl.pallas_call(
        _mla_proj_kernel,
        grid=(_stream_tiles(batch),),
        in_specs=[
            pl.BlockSpec((TMX, d), lambda t: (t, 0)),
            pl.BlockSpec((None, None, N_MOD, d), lambda t: (layer, _tile_mod_row(t, batch), 0, 0)),
            pl.BlockSpec((None, 1, d), lambda t: (layer * N_SUB + 1, 0, 0)),
            _resident(wd.shape),
            _resident((1, MLA_Q_RANK)),
            _resident((1, MLA_KV_RANK)),
            _resident(wuq.shape),
            _resident(wukv.shape),
            pl.BlockSpec((TMX, 128), lambda t: (_tile_block(t, batch), 0)),
            pl.BlockSpec((TMX, 128), lambda t: (_tile_block(t, batch), 0)),
        ],
        out_specs=[
            pl.BlockSpec((None, MLA_HEADS, TMX, MLA_QK), bmap),
            pl.BlockSpec((None, MLA_HEADS, TMX, MLA_QK), bmap),
            pl.BlockSpec((None, MLA_HEADS, TMX, MLA_QK), bmap),
        ],
        out_shape=[
            jax.ShapeDtypeStruct((batch, MLA_HEADS, S_ALL, MLA_QK), BF16),
            jax.ShapeDtypeStruct((batch, MLA_HEADS, S_ALL, MLA_QK), BF16),
            jax.ShapeDtypeStruct((batch, MLA_HEADS, S_ALL, MLA_QK), BF16),
        ],
        compiler_params=pltpu.CompilerParams(
            dimension_semantics=("arbitrary",), vmem_limit_bytes=VMEM_LIMIT_BYTES),
        name=f"mla_proj_l{layer}",
    )(hs, mods, norm_g3, wd, q_norm, kv_norm, wuq, wukv, cos, sin)


def _attn_kernel(q_ref, k_ref, v_ref, o_ref, *, n_chain):
    rows = q_ref.shape[0] // n_chain
    for c in range(n_chain):
        r = slice(c * rows, (c + 1) * rows)
        s = lax.dot_general(q_ref[r, :], k_ref[...], (((1,), (1,)), ((), ())), preferred_element_type=F32)
        p = jnp.exp2(s - jnp.max(s, axis=-1, keepdims=True))
        o = jnp.dot(p.astype(BF16), v_ref[...], preferred_element_type=F32)
        o_ref[r, :] = (o[:, :MLA_V] / o[:, MLA_V:MLA_V + 1]).astype(BF16)


def mla_attention(q, k, v, *, tq, n_chain, q_block0, n_q, k_rows, k_block, name):
    batch, heads, _, _ = q.shape
    return pl.pallas_call(
        functools.partial(_attn_kernel, n_chain=n_chain),
        grid=(batch, heads, n_q),
        in_specs=[
            pl.BlockSpec((None, None, tq, MLA_QK), lambda b, h, i: (b, h, q_block0 + i, 0)),
            pl.BlockSpec((None, None, k_rows, MLA_QK), lambda b, h, i: (b, h, k_block, 0)),
            pl.BlockSpec((None, None, k_rows, MLA_QK), lambda b, h, i: (b, h, k_block, 0)),
        ],
        out_specs=pl.BlockSpec((None, tq, MLA_V), lambda b, h, i: (b, i, h)),
        out_shape=jax.ShapeDtypeStruct((batch, n_q * tq, heads * MLA_V), BF16),
        compiler_params=pltpu.CompilerParams(
            dimension_semantics=("arbitrary", "arbitrary", "arbitrary"), vmem_limit_bytes=VMEM_LIMIT_BYTES),
        name=name,
    )(q, k, v)


def _out_proj_kernel(yl_ref, yc_ref, w_ref, x_ref, mod_ref, o_ref, *, n_lat_tiles, ctx_out):
    t = pl.program_id(0)

    def update(y_ref):
        y = jnp.dot(y_ref[...], w_ref[...], preferred_element_type=F32)
        o_ref[...] = x_ref[...] + mod_ref[5:6, :] * y

    @pl.when(t < n_lat_tiles)
    def _():
        update(yl_ref)

    @pl.when(t >= n_lat_tiles)
    def _():
        if ctx_out:
            update(yc_ref)
        else:
            o_ref[...] = x_ref[...]


def mixer_out_proj(y_lat, y_ctx, w, hs, mods, *, layer, batch, name):
    d = hs.shape[1]
    kdim = y_lat.shape[2]
    nl = batch * LAT_TILES
    ctx_out = y_ctx is not None
    kern = functools.partial(_out_proj_kernel, n_lat_tiles=nl, ctx_out=ctx_out)

    def lat_map(t):
        tl = jnp.minimum(t, nl - 1)
        return (tl // LAT_TILES, tl % LAT_TILES, 0)

    return pl.pallas_call(
        kern,
        grid=(_stream_tiles(batch),),
        in_specs=[
            pl.BlockSpec((None, TMX, kdim), lat_map),
            pl.BlockSpec((None, TMX, kdim), lambda t: (jnp.maximum(t - nl, 0), 0, 0)),
            _resident(w.shape),
            pl.BlockSpec((TMX, d), lambda t: (t, 0)),
            pl.BlockSpec((None, None, N_MOD, d), lambda t: (layer, _tile_mod_row(t, batch), 0, 0)),
        ],
        out_specs=pl.BlockSpec((TMX, d), lambda t: (t, 0)),
        out_shape=jax.ShapeDtypeStruct(hs.shape, F32),
        compiler_params=pltpu.CompilerParams(
            dimension_semantics=("arbitrary",), vmem_limit_bytes=VMEM_LIMIT_BYTES),
        name=name,
    )(y_lat, y_ctx if ctx_out else y_lat[:, :CTX_LEN], w, hs, mods)


def mla_layer(hs, mods, norm_g3, weights, q_norm, kv_norm, w_o, cos, sin, *, layer, batch, with_ctx_out):
    wd, wuq, wukv = weights
    q, k, v = mla_project(hs, mods, norm_g3, wd, q_norm, kv_norm, wuq, wukv, cos, sin, layer=layer, batch=batch)
    tq = 4096
    att = mla_attention(q, k, v, tq=tq, n_chain=16, q_block0=0, n_q=SEQ // tq, k_rows=S_ALL, k_block=0,
                        name=f"mla_attn_lat_l{layer}")
    att_ctx = None
    if with_ctx_out:
        att_ctx = mla_attention(q, k, v, tq=CTX_LEN, n_chain=1, q_block0=SEQ // CTX_LEN, n_q=1, k_rows=CTX_LEN,
                                k_block=SEQ // CTX_LEN, name=f"mla_attn_ctx_l{layer}")
    return mixer_out_proj(att, att_ctx, w_o, hs, mods, layer=layer, batch=batch, name=f"mla_out_l{layer}")


HG_WIN = 128
HG_NH = 8
N_TBLK = S_ALL // TMX


def _hgrn_proj_kernel(x_ref, mod_ref, g_ref, w_ref, o_ref):
    n = _adanorm(x_ref[...], g_ref[...], mod_ref[3:4, :], mod_ref[4:5, :]).astype(BF16)
    o_ref[...] = jnp.dot(n, w_ref[...], preferred_element_type=F32)


def hgrn_project(hs, mods, norm_g3, w_in, *, layer, batch):
    d = hs.shape[1]
    n_out = w_in.shape[1]
    return pl.pallas_call(
        _hgrn_proj_kernel,
        grid=(n_out // d, _stream_tiles(batch)),
        in_specs=[
            pl.BlockSpec((TMX, d), lambda j, t: (t, 0)),
            pl.BlockSpec((None, None, N_MOD, d), lambda j, t: (layer, _tile_mod_row(t, batch), 0, 0)),
            pl.BlockSpec((None, 1, d), lambda j, t: (layer * N_SUB + 1, 0, 0)),
            pl.BlockSpec((d, d), lambda j, t: (0, j)),
        ],
        out_specs=pl.BlockSpec((None, TMX, d), lambda j, t: (_tile_batch(t, batch), _tile_block(t, batch), j)),
        out_shape=jax.ShapeDtypeStruct((batch, S_ALL, n_out), F32),
        compiler_params=pltpu.CompilerParams(
            dimension_semantics=("arbitrary", "arbitrary"), vmem_limit_bytes=VMEM_LIMIT_BYTES),
        name=f"hgrn_proj_l{layer}",
    )(hs, mods, norm_g3, w_in)


def _split3(x):
    hi = x.astype(BF16)
    r1 = x - hi.astype(F32)
    mid = r1.astype(BF16)
    lo = (r1 - mid.astype(F32)).astype(BF16)
    return hi, mid, lo


def _hgrn_scan_kernel(q_ref, z_ref, v_ref, lb_ref, om_ref, tri_ref, o_ref,
                      st_ref, qd_ref, kin_ref, kte_ref, vb_ref, tot_ref, *, reverse):
    ts = pl.program_id(2)

    @pl.when(ts == 0)
    def _():
        st_ref[...] = jnp.zeros_like(st_ref)

    z = z_ref[...]
    e = jnp.exp(-jnp.abs(z))
    r = 1.0 / (1.0 + e)
    er = e * r
    pos = z >= 0
    om = om_ref[...]
    f = lb_ref[...] + om * jnp.where(pos, r, er)
    log_f = jnp.log(f)
    k = om * jnp.where(pos, er, r)
    tri = tri_ref[...]
    cum = sum(jnp.dot(tri, piece, preferred_element_type=F32) for piece in _split3(log_f))
    b_inc = cum[:TMX]
    b_rest = cum[TMX:]
    qr = q_ref[...]
    qd_ref[...] = (qr * jax.nn.sigmoid(qr) * jnp.exp(b_inc)).astype(BF16)
    kin_ref[...] = (k * jnp.exp(-b_inc)).astype(BF16)
    kte_ref[...] = (k * jnp.exp(b_rest)).astype(BF16)
    vb_ref[...] = v_ref[...].astype(BF16)
    tot = b_inc + b_rest
    dec_all = []
    for h in range(HG_NH):
        tot_ref[h] = tot[:, h * HG_HEAD_DIM:(h + 1) * HG_HEAD_DIM]
        dec_all.append(jnp.exp(tot_ref[h, pl.ds(0, TMX // HG_CHUNK, stride=HG_CHUNK), :]))

    wi = lax.broadcasted_iota(jnp.int32, (HG_WIN, HG_WIN), 0)
    wj = lax.broadcasted_iota(jnp.int32, (HG_WIN, HG_WIN), 1)
    same_chunk = (wi // HG_CHUNK) == (wj // HG_CHUNK)
    causal = same_chunk & ((wj >= wi) if reverse else (wj <= wi))

    n_win = TMX // HG_WIN
    per_win = HG_WIN // HG_CHUNK
    wins = list(range(n_win - 1, -1, -1) if reverse else range(n_win))
    chunks = list(range(per_win - 1, -1, -1) if reverse else range(per_win))
    def only_chunk(x, c):
        parts = []
        if c > 0:
            parts.append(jnp.zeros((c * HG_CHUNK, x.shape[1]), x.dtype))
        parts.append(x[c * HG_CHUNK:(c + 1) * HG_CHUNK, :])
        if c < per_win - 1:
            parts.append(jnp.zeros(((per_win - 1 - c) * HG_CHUNK, x.shape[1]), x.dtype))
        return jnp.concatenate(parts, axis=0)

    states = [st_ref[h] for h in range(HG_NH)]
    heads = range(HG_NH)
    hcols = [slice(h * HG_HEAD_DIM, (h + 1) * HG_HEAD_DIM) for h in heads]
    for w in wins:
        rows = slice(w * HG_WIN, (w + 1) * HG_WIN)
        o_intra, upd = [], []
        for h in heads:
            qd_w, kin_w, kte_w, v_w = (r[rows, hcols[h]] for r in (qd_ref, kin_ref, kte_ref, vb_ref))
            a = lax.dot_general(qd_w, kin_w, (((1,), (1,)), ((), ())), preferred_element_type=F32)
            a = jnp.where(causal, a, 0.0).astype(BF16)
            o_intra.append(jnp.dot(a, v_w, preferred_element_type=F32))
            v_t = v_ref[rows, hcols[h]].T.astype(BF16)
            rhs = jnp.concatenate([only_chunk(kte_w, c) for c in range(per_win)], axis=1)
            upd.append(jnp.dot(v_t, rhs, preferred_element_type=F32))
        entry = [dict() for _ in heads]
        for h in heads:
            st = states[h]
            for c in chunks:
                entry[h][c] = st.astype(BF16)
                ci = w * per_win + c
                st = st * dec_all[h][ci:ci + 1, :] + upd[h][:, c * HG_HEAD_DIM:(c + 1) * HG_HEAD_DIM]
            states[h] = st
        for h in heads:
            qd_w = qd_ref[rows, hcols[h]]
            lhs = jnp.concatenate([only_chunk(qd_w, c) for c in range(per_win)], axis=1)
            ent = jnp.concatenate([entry[h][c] for c in range(per_win)], axis=1)
            o_inter = lax.dot_general(lhs, ent, (((1,), (1,)), ((), ())), preferred_element_type=F32)
            o_ref[rows, hcols[h]] = o_intra[h] + o_inter
    for h in heads:
        st_ref[h] = states[h]


def _hgrn_cumsum_matrix(reverse):
    r = np.arange(TMX)[:, None]
    c = np.arange(TMX)[None, :]
    same = (r // HG_CHUNK) == (c // HG_CHUNK)
    inc, rest = ((c >= r), (c < r)) if reverse else ((c <= r), (c > r))
    return jnp.asarray(np.concatenate([same & inc, same & rest], axis=0), BF16)


def hgrn_scan(p, lb, om, *, reverse, z_col, layer):
    batch, s_all, n_out = p.shape
    d = n_out // 5
    wcols = HG_NH * HG_HEAD_DIM
    groups = d // wcols

    def tblk(ts):
        lat = (N_TBLK - 1 - ts) if reverse else (ts - 1)
        return jnp.where(ts == 0, CTX_BLOCK, lat)

    col = lambda base: (lambda b, g, ts: (b, tblk(ts), base * groups + g))
    row_spec = pl.BlockSpec((1, wcols), lambda b, g, ts: (0, g))
    kern = functools.partial(_hgrn_scan_kernel, reverse=reverse)
    return pl.pallas_call(
        kern,
        grid=(batch, groups, N_TBLK),
        in_specs=[
            pl.BlockSpec((None, TMX, wcols), col(0)),
            pl.BlockSpec((None, TMX, wcols), col(z_col)),
            pl.BlockSpec((None, TMX, wcols), col(3)),
            row_spec, row_spec, _resident((2 * TMX, TMX)),
        ],
        out_specs=pl.BlockSpec((None, TMX, wcols), lambda b, g, ts: (b, tblk(ts), g)),
        out_shape=jax.ShapeDtypeStruct((batch, s_all, d), F32),
        scratch_shapes=[
            pltpu.VMEM((HG_NH, HG_HEAD_DIM, HG_HEAD_DIM), F32),
            pltpu.VMEM((TMX, wcols), BF16), pltpu.VMEM((TMX, wcols), BF16),
            pltpu.VMEM((TMX, wcols), BF16), pltpu.VMEM((TMX, wcols), BF16),
            pltpu.VMEM((HG_NH, TMX, HG_HEAD_DIM), F32),
        ],
        compiler_params=pltpu.CompilerParams(
            dimension_semantics=("arbitrary", "arbitrary", "arbitrary"), vmem_limit_bytes=VMEM_LIMIT_BYTES),
        name=f"hgrn_scan_{'bwd' if reverse else 'fwd'}_l{layer}",
    )(p, p, p, lb, om, _hgrn_cumsum_matrix(reverse))


def _hgrn_readout_kernel(of_ref, ob_ref, gate_ref, gn_ref, w_ref, x_ref, mod_ref, o_ref, *, n_lat_tiles, ctx_out):
    t = pl.program_id(0)

    def update():
        o = of_ref[...] + ob_ref[...]
        gate = gate_ref[...]
        gs = gate * jax.nn.sigmoid(gate)
        parts = []
        for h in range(HG_HEADS):
            cols = slice(h * HG_HEAD_DIM, (h + 1) * HG_HEAD_DIM)
            parts.append((_rms(o[:, cols]) * gn_ref[:, cols] * gs[:, cols]).astype(BF16))
        y = jnp.dot(jnp.concatenate(parts, axis=1), w_ref[...], preferred_element_type=F32)
        o_ref[...] = x_ref[...] + mod_ref[5:6, :] * y

    if ctx_out:
        update()
    else:
        pl.when(t < n_lat_tiles)(update)

        @pl.when(t >= n_lat_tiles)
        def _():
            o_ref[...] = x_ref[...]


def hgrn_readout(o_f, o_b, p, g_norm, w_out, hs, mods, *, layer, batch, ctx_out):
    d = hs.shape[1]
    seq = lambda t: (_tile_batch(t, batch), _tile_block(t, batch), 0)
    kern = functools.partial(_hgrn_readout_kernel, n_lat_tiles=batch * LAT_TILES, ctx_out=ctx_out)
    return pl.pallas_call(
        kern,
        grid=(_stream_tiles(batch),),
        in_specs=[
            pl.BlockSpec((None, TMX, d), seq),
            pl.BlockSpec((None, TMX, d), seq),
            pl.BlockSpec((None, TMX, d), lambda t: (_tile_batch(t, batch), _tile_block(t, batch), 4)),
            _resident((1, d)),
            _resident(w_out.shape),
            pl.BlockSpec((TMX, d), lambda t: (t, 0)),
            pl.BlockSpec((None, None, N_MOD, d), lambda t: (layer, _tile_mod_row(t, batch), 0, 0)),
        ],
        out_specs=pl.BlockSpec((TMX, d), lambda t: (t, 0)),
        out_shape=jax.ShapeDtypeStruct(hs.shape, F32),
        compiler_params=pltpu.CompilerParams(
            dimension_semantics=("arbitrary",), vmem_limit_bytes=VMEM_LIMIT_BYTES),
        name=f"hgrn_out_l{layer}",
    )(o_f, o_b, p, g_norm, w_out, hs, mods)


def hgrn_layer(hs, mods, norm_g3, w_in, lb_fwd, lb_bwd, g_norm, w_out, *, layer, batch, with_ctx_out):
    p = hgrn_project(hs, mods, norm_g3, w_in, layer=layer, batch=batch)
    outs = []
    for reverse, lb, z_col in ((False, lb_fwd, 1), (True, lb_bwd, 2)):
        lb = lb.reshape(1, -1)
        outs.append(hgrn_scan(p, lb, 1.0 - lb, reverse=reverse, z_col=z_col, layer=layer))
    return hgrn_readout(outs[0], outs[1], p, g_norm.reshape(1, -1), w_out, hs, mods,
                        layer=layer, batch=batch, ctx_out=with_ctx_out)


FN_GC = D_MODEL // FOURIER_GROUPS
FN_TM = 512
FN_TK = 1024


def _fnet_tables(n_pos, tm):
    t = np.arange(n_pos, dtype=np.int64)
    ang = lambda k: 2.0 * np.pi * ((k[:, None] * t[None, :]) % n_pos) / n_pos
    phi = ang(np.arange(tm, dtype=np.int64))
    th = ang(np.arange(0, n_pos, tm, dtype=np.int64))
    sc = n_pos ** -0.5
    rows = np.stack([np.stack([np.cos(th), -np.sin(th)], axis=1),
                     np.stack([-np.sin(th), -np.cos(th)], axis=1)], axis=1) * sc
    rows = rows.reshape(-1, 2, n_pos)
    return (jnp.asarray(rows, F32), jnp.asarray(np.cos(phi), F32), jnp.asarray(np.sin(phi), F32))


def _fnet_channel_table():
    c = np.arange(FN_GC, dtype=np.int64)
    ang = 2.0 * np.pi * ((c[:, None] * c[None, :]) % FN_GC) / FN_GC
    return jnp.asarray(np.concatenate([np.cos(ang), np.sin(ang)], axis=1) * FN_GC ** -0.5, BF16)


def _fnet_chan_kernel(x_ref, mod_ref, g_ref, cs_ref, o_ref):
    n = _adanorm(x_ref[...], g_ref[...], mod_ref[3:4, :], mod_ref[4:5, :]).astype(BF16)
    for g in range(FOURIER_GROUPS):
        cols = slice(g * FN_GC, (g + 1) * FN_GC)
        pq = jnp.dot(n[:, cols], cs_ref[...], preferred_element_type=F32)
        o_ref[0, :, cols] = pq[:, :FN_GC].astype(BF16)
        o_ref[1, :, cols] = pq[:, FN_GC:].astype(BF16)


def fnet_channel_dft(hs, mods, norm_g3, cs, *, layer, batch):
    d = hs.shape[1]
    return pl.pallas_call(
        _fnet_chan_kernel,
        grid=(_stream_tiles(batch),),
        in_specs=[
            pl.BlockSpec((TMX, d), lambda t: (t, 0)),
            pl.BlockSpec((None, None, N_MOD, d), lambda t: (layer, _tile_mod_row(t, batch), 0, 0)),
            pl.BlockSpec((None, 1, d), lambda t: (layer * N_SUB + 1, 0, 0)),
            _resident(cs.shape),
        ],
        out_specs=pl.BlockSpec((None, 2, TMX, d), lambda t: (_tile_batch(t, batch), 0, _tile_block(t, batch), 0)),
        out_shape=jax.ShapeDtypeStruct((batch, 2, S_ALL, d), BF16),
        compiler_params=pltpu.CompilerParams(
            dimension_semantics=("arbitrary",), vmem_limit_bytes=VMEM_LIMIT_BYTES),
        name=f"fnet_chan_l{layer}",
    )(hs, mods, norm_g3, cs)


def _fnet_pos_kernel(rt_ref, cphi_ref, sphi_ref, pq_ref, o_ref, acc_ref, *, nk, tk):
    kk = pl.program_id(2)
    cols = pl.ds(pl.multiple_of((kk % (nk // 2)) * tk, tk), tk)
    tile = (rt_ref[0:1, :] * cphi_ref[:, cols] + rt_ref[1:2, :] * sphi_ref[:, cols]).astype(BF16)

    @pl.when(kk == 0)
    def _():
        acc_ref[...] = jnp.zeros_like(acc_ref)

    acc_ref[...] += jnp.dot(tile, pq_ref[...], preferred_element_type=F32)

    @pl.when(kk == nk - 1)
    def _():
        o_ref[...] = acc_ref[...].astype(BF16)


def fnet_position_dft(pq, tables, *, n_pos, tm, tk, row_blk0, name):
    batch, _, _, d = pq.shape
    rows, cphi, sphi = tables
    n_m, nkh = n_pos // tm, n_pos // tk
    nk = 2 * nkh
    kern = functools.partial(_fnet_pos_kernel, nk=nk, tk=tk)
    return pl.pallas_call(
        kern,
        grid=(batch, n_m, nk),
        in_specs=[
            pl.BlockSpec((None, 2, tk), lambda b, m, kk: (m * 2 + kk // nkh, 0, kk % nkh)),
            _resident(cphi.shape),
            _resident(sphi.shape),
            pl.BlockSpec((None, None, tk, d), lambda b, m, kk: (b, kk // nkh, row_blk0 + kk % nkh, 0)),
        ],
        out_specs=pl.BlockSpec((None, tm, d), lambda b, m, kk: (b, m, 0)),
        out_shape=jax.ShapeDtypeStruct((batch, n_pos, d), BF16),
        scratch_shapes=[pltpu.VMEM((tm, d), F32)],
        compiler_params=pltpu.CompilerParams(
            dimension_semantics=("arbitrary", "arbitrary", "arbitrary"), vmem_limit_bytes=VMEM_LIMIT_BYTES),
        name=name,
    )(rows, cphi, sphi, pq)


def fnet_layer(hs, mods, norm_g3, w_out, *, layer, batch, with_ctx_out):
    pq = fnet_channel_dft(hs, mods, norm_g3, _fnet_channel_table(), layer=layer, batch=batch)
    y = fnet_position_dft(pq, _fnet_tables(SEQ, FN_TM), n_pos=SEQ, tm=FN_TM, tk=FN_TK, row_blk0=0,
                          name=f"fnet_pos_lat_l{layer}")
    y_ctx = None
    if with_ctx_out:
        y_ctx = fnet_position_dft(pq, _fnet_tables(CTX_LEN, CTX_LEN), n_pos=CTX_LEN, tm=CTX_LEN, tk=CTX_LEN,
                                  row_blk0=SEQ // CTX_LEN, name=f"fnet_pos_ctx_l{layer}")
    return mixer_out_proj(y, y_ctx, w_out, hs, mods, layer=layer, batch=batch, name=f"fnet_out_l{layer}")


def kernel(x, c, ctx, c_ctx, mod_w, mod_b, norm_g, ffn1_w_gu, ffn1_w_down, ffn2_w_gu, ffn2_w_down,
           hgrn_w_in, hgrn_lb_logits, hgrn_g_norm, hgrn_w_out,
           mla_w_dqkv, mla_q_norm, mla_kv_norm, mla_w_uq, mla_w_ukv, mla_w_o,
           fnet_w_out, final_g):
    B, T, D = x.shape
    n_lat = B * T
    n_ctx = B * CTX_LEN
    tm, tf = 512, 512
    rows_all = n_lat + n_ctx

    lb = jnp.cumsum(jax.nn.softmax(hgrn_lb_logits.astype(jnp.float32), axis=1), axis=1)
    lb = lb - lb[:, :1]
    rope_cos, rope_sin = mla_rope_tables(T)

    cc = jnp.concatenate([c, c_ctx[None, :], jnp.zeros((MOD_ROWS - B - 1, D), F32)], axis=0)
    mods = modulation(cc, mod_w, mod_b).reshape(DEPTH, MOD_ROWS, N_MOD, D)

    w1_gu, w1_down = ffn1_w_gu.astype(BF16), ffn1_w_down.astype(BF16)
    w2_gu, w2_down = ffn2_w_gu.astype(BF16), ffn2_w_down.astype(BF16)
    norm_g3 = norm_g.reshape(DEPTH * N_SUB, 1, D)
    final_g2 = final_g.reshape(1, D)

    hs = jnp.concatenate([x.reshape(n_lat, D), ctx.reshape(n_ctx, D)], axis=0)
    for i in range(DEPTH):
        kind, j = i % N_MIXERS, i // N_MIXERS
        last = i == DEPTH - 1
        ctx_in = not (last and kind == 2)
        hs = ffn_sublayer(hs, mods, norm_g3, w1_gu, w1_down, final_g2, layer=i, s=0,
                          rows=rows_all if ctx_in else n_lat, tm=tm, tf=tf)
        if kind == 0:
            hs = hgrn_layer(hs, mods, norm_g3, hgrn_w_in[j].astype(BF16), lb[0, j], lb[1, j], hgrn_g_norm[j],
                            hgrn_w_out[j].astype(BF16), layer=i, batch=B, with_ctx_out=not last)
        elif kind == 1:
            hs = mla_layer(hs, mods, norm_g3, mla_weights(mla_w_dqkv[j], mla_w_uq[j], mla_w_ukv[j]),
                           mla_q_norm[j].reshape(1, -1), mla_kv_norm[j].reshape(1, -1),
                           mla_w_o[j].astype(BF16), rope_cos, rope_sin, layer=i, batch=B, with_ctx_out=not last)
        else:
            hs = fnet_layer(hs, mods, norm_g3, fnet_w_out[j].astype(BF16), layer=i, batch=B, with_ctx_out=not last)
        hs = ffn_sublayer(hs, mods, norm_g3, w2_gu, w2_down, final_g2, layer=i, s=2,
                          rows=n_lat if last else rows_all, tm=tm, tf=tf, final=last)
    return hs.reshape(B, T, D)
```

```python
import functools

import numpy as np
import jax
import jax.numpy as jnp
from jax import lax
from jax.experimental import pallas as pl
from jax.experimental.pallas import tpu as pltpu

D_MODEL = 2048
SEQ = 4096
DEPTH = 4
GRID_W = 64
CTX_LEN = 256
N_MIXERS = 3
N_SUB = 3
D_FF = 5632
RMS_EPS = 1e-6

HG_HEAD_DIM = 128
HG_HEADS = D_MODEL // HG_HEAD_DIM
HG_QF = HG_HEADS * HG_HEAD_DIM
HG_IV = HG_HEADS * HG_HEAD_DIM
HG_CHUNK = 16

MLA_HEADS = 16
MLA_Q_RANK = 512
MLA_KV_RANK = 512
MLA_NOPE = 128
MLA_ROPE = 64
MLA_V = 128
MLA_SCALE = (MLA_NOPE + MLA_ROPE) ** -0.5
ATTN_BLOCK = 128
ROPE_THETA = 10000.0
ROPE_FREQS = MLA_ROPE // 4

FOURIER_GROUPS = 8

BF16 = jnp.bfloat16
F32 = jnp.float32

VMEM_LIMIT_BYTES = 56 * 1024 * 1024
MOD_ROWS = 8
N_MOD = N_SUB * 3


def _mod_row(t, tm):
    return jnp.minimum(t // (SEQ // tm), 2)


def _mod_kernel(c_ref, w_ref, b_ref, o_ref):
    c = c_ref[...]
    a = (c * jax.nn.sigmoid(c)).astype(BF16)
    o_ref[...] = jnp.dot(a, w_ref[...].astype(BF16), preferred_element_type=F32) + b_ref[...]


def modulation(cc, mod_w, mod_b, *, tn=1024):
    depth, d, n = mod_w.shape
    return pl.pallas_call(
        _mod_kernel,
        grid=(depth, n // tn),
        in_specs=[
            pl.BlockSpec((MOD_ROWS, d), lambda i, j: (0, 0)),
            pl.BlockSpec((None, d, tn), lambda i, j: (i, 0, j)),
            pl.BlockSpec((None, 1, tn), lambda i, j: (i, 0, j)),
        ],
        out_specs=pl.BlockSpec((None, MOD_ROWS, tn), lambda i, j: (i, 0, j)),
        out_shape=jax.ShapeDtypeStruct((depth, MOD_ROWS, n), F32),
        compiler_params=pltpu.CompilerParams(
            dimension_semantics=("arbitrary", "arbitrary"), vmem_limit_bytes=VMEM_LIMIT_BYTES),
        name="modulation",
    )(cc, mod_w, mod_b.reshape(depth, 1, n))


def _adanorm(x, g, shift, scale):
    y = x * lax.rsqrt(jnp.mean(x * x, axis=-1, keepdims=True) + RMS_EPS)
    return (y * g) * (1.0 + scale) + shift


def _ffn_kernel(x_ref, mod_ref, g_ref, wgu_ref, wd_ref, fg_ref, o_ref, n_ref, *, s, nf, final):
    f = pl.program_id(1)

    @pl.when(f == 0)
    def _():
        n = _adanorm(x_ref[...], g_ref[...], mod_ref[3 * s:3 * s + 1, :], mod_ref[3 * s + 1:3 * s + 2, :])
        n_ref[...] = n.astype(BF16)
        o_ref[...] = jnp.zeros_like(o_ref)

    n = n_ref[...]
    gate = jnp.dot(n, wgu_ref[0], preferred_element_type=F32)
    up = jnp.dot(n, wgu_ref[1], preferred_element_type=F32)
    act = (gate * jax.nn.sigmoid(gate) * up).astype(BF16)
    o_ref[...] += jnp.dot(act, wd_ref[...], preferred_element_type=F32)

    @pl.when(f == nf - 1)
    def _():
        h = x_ref[...] + (0.5 * mod_ref[3 * s + 2:3 * s + 3, :]) * o_ref[...]
        if final:
            h = h * lax.rsqrt(jnp.mean(h * h, axis=-1, keepdims=True) + RMS_EPS) * fg_ref[...]
        o_ref[...] = h


def ffn_gu_tiles(w_gu, tf):
    depth, d, _ = w_gu.shape
    w = w_gu.astype(BF16).reshape(depth, d, 2, D_FF // tf, tf)
    return jnp.transpose(w, (0, 3, 2, 1, 4))


def ffn_sublayer(h, mods, g, w_gu, w_down, final_g, *, layer, s, rows, tm, tf, final=False):
    d = h.shape[1]
    nf = D_FF // tf
    kern = functools.partial(_ffn_kernel, s=s, nf=nf, final=final)
    return pl.pallas_call(
        kern,
        grid=(rows // tm, nf),
        in_specs=[
            pl.BlockSpec((tm, d), lambda t, f: (t, 0)),
            pl.BlockSpec((None, None, N_MOD, d), lambda t, f: (layer, _mod_row(t, tm), 0, 0)),
            pl.BlockSpec((None, 1, d), lambda t, f: (layer * N_SUB + s, 0, 0)),
            pl.BlockSpec((None, None, 2, d, tf), lambda t, f: (layer, f, 0, 0, 0)),
            pl.BlockSpec((None, tf, d), lambda t, f: (layer, f, 0)),
            pl.BlockSpec((1, d), lambda t, f: (0, 0)),
        ],
        out_specs=pl.BlockSpec((tm, d), lambda t, f: (t, 0)),
        out_shape=jax.ShapeDtypeStruct((rows if final else h.shape[0], d), F32),
        scratch_shapes=[pltpu.VMEM((tm, d), BF16)],
        compiler_params=pltpu.CompilerParams(
            dimension_semantics=("arbitrary", "arbitrary"), vmem_limit_bytes=VMEM_LIMIT_BYTES),
        name=f"ffn_l{layer}_s{s}",
    )(h, mods, g, w_gu, w_down, final_g)


TMX = 256
S_ALL = SEQ + CTX_LEN
LAT_TILES = SEQ // TMX
CTX_BLOCK = SEQ // TMX


def _resident(shape):
    return pl.BlockSpec(shape, lambda *_: (0,) * len(shape), pipeline_mode=pl.Buffered(1))


def _stream_tiles(batch):
    return batch * (LAT_TILES + CTX_LEN // TMX)


def _tile_batch(t, batch):
    lat = t < batch * LAT_TILES
    return jnp.where(lat, t // LAT_TILES, t - batch * LAT_TILES)


def _tile_block(t, batch):
    return jnp.where(t < batch * LAT_TILES, t % LAT_TILES, CTX_BLOCK)


def _tile_mod_row(t, batch):
    return jnp.where(t < batch * LAT_TILES, t // LAT_TILES, batch)


def _rms(x):
    return x * lax.rsqrt(jnp.mean(x * x, axis=-1, keepdims=True) + RMS_EPS)


MLA_QK = 2 * MLA_NOPE
LOG2_E = 1.4426950408889634
N_DQ = MLA_Q_RANK + MLA_KV_RANK


def _mla_proj_kernel(x_ref, mod_ref, g_ref, wd_ref, qn_ref, kvn_ref, wuq_ref, wukv_ref, cos_ref, sin_ref,
                     q_ref, k_ref, v_ref):
    n = _adanorm(x_ref[...], g_ref[...], mod_ref[3:4, :], mod_ref[4:5, :]).astype(BF16)
    proj = jnp.dot(n, wd_ref[...], preferred_element_type=F32)
    cq = (_rms(proj[:, :MLA_Q_RANK]) * qn_ref[...]).astype(BF16)
    ckv = (_rms(proj[:, MLA_Q_RANK:N_DQ]) * kvn_ref[...]).astype(BF16)
    cos, sin = cos_ref[...], sin_ref[...]
    kr = proj[:, N_DQ:N_DQ + 128] * cos + proj[:, N_DQ + 128:N_DQ + 256] * sin
    kr = kr.astype(BF16)
    q = jnp.dot(cq, wuq_ref[...], preferred_element_type=F32) * (MLA_SCALE * LOG2_E)
    kv = jnp.dot(ckv, wukv_ref[...], preferred_element_type=F32)
    hn = MLA_HEADS * MLA_NOPE
    lane = lax.broadcasted_iota(jnp.int32, (x_ref.shape[0], 128), 1)
    ones_col = jnp.where(lane == 0, 1.0, 0.0).astype(BF16)
    for h in range(MLA_HEADS):
        lo = h * 128
        q_ref[h, :, 0:128] = q[:, lo:lo + 128].astype(BF16)
        qr = q[:, hn + lo:hn + lo + 128] * cos + q[:, 2 * hn + lo:2 * hn + lo + 128] * sin
        q_ref[h, :, 128:256] = qr.astype(BF16)
        k_ref[h, :, 0:128] = kv[:, lo:lo + 128].astype(BF16)
        k_ref[h, :, 128:256] = kr
        v_ref[h, :, 0:128] = kv[:, hn + lo:hn + lo + 128].astype(BF16)
        v_ref[h, :, 128:256] = ones_col


def _rope_partner(width):
    idx = jnp.arange(width)
    return idx ^ ROPE_FREQS


def mla_weights(w_dqkv, w_uq, w_ukv):
    d = w_dqkv.shape[0]
    z = jnp.zeros((d, 128 - MLA_ROPE), w_dqkv.dtype)
    kr = w_dqkv[:, N_DQ:]
    wd = jnp.concatenate([w_dqkv[:, :N_DQ], kr, z, kr[:, _rope_partner(MLA_ROPE)], z], axis=1)
    wq = w_uq.reshape(MLA_Q_RANK, MLA_HEADS, MLA_NOPE + MLA_ROPE)
    qr = wq[:, :, MLA_NOPE:]
    zq = jnp.zeros((MLA_Q_RANK, MLA_HEADS, 128 - MLA_ROPE), w_uq.dtype)
    wuq = jnp.concatenate([
        wq[:, :, :MLA_NOPE].reshape(MLA_Q_RANK, -1),
        jnp.concatenate([qr, zq], axis=2).reshape(MLA_Q_RANK, -1),
        jnp.concatenate([qr[:, :, _rope_partner(MLA_ROPE)], zq], axis=2).reshape(MLA_Q_RANK, -1)], axis=1)
    wkv = w_ukv.reshape(MLA_KV_RANK, MLA_HEADS, MLA_NOPE + MLA_V)
    wukv = jnp.concatenate([wkv[:, :, :MLA_NOPE].reshape(MLA_KV_RANK, -1),
                            wkv[:, :, MLA_NOPE:].reshape(MLA_KV_RANK, -1)], axis=1)
    return wd.astype(BF16), wuq.astype(BF16), wukv.astype(BF16)


def mla_rope_tables(n_tokens):
    rows = n_tokens // GRID_W
    r = jnp.broadcast_to(jnp.arange(rows, dtype=F32)[:, None], (rows, GRID_W)).reshape(-1)
    col = jnp.broadcast_to(jnp.arange(GRID_W, dtype=F32)[None, :], (rows, GRID_W)).reshape(-1)
    inv_freq = ROPE_THETA ** (-jnp.arange(ROPE_FREQS, dtype=F32) / ROPE_FREQS)
    ang = jnp.stack([r, col], axis=-1)[..., None] * inv_freq
    cos = jnp.broadcast_to(jnp.cos(ang)[:, :, None, :], (n_tokens, 2, 2, ROPE_FREQS)).reshape(n_tokens, MLA_ROPE)
    sin = jnp.sin(ang)
    sin = jnp.stack([-sin, sin], axis=2).reshape(n_tokens, MLA_ROPE)
    pad = jnp.zeros((n_tokens, 128 - MLA_ROPE), F32)
    cos = jnp.concatenate([cos, pad], axis=1)
    sin = jnp.concatenate([sin, pad], axis=1)
    ctx_cos = jnp.concatenate([jnp.ones((CTX_LEN, MLA_ROPE), F32), jnp.zeros((CTX_LEN, 128 - MLA_ROPE), F32)], axis=1)
    return (jnp.concatenate([cos, ctx_cos], axis=0),
            jnp.concatenate([sin, jnp.zeros((CTX_LEN, 128), F32)], axis=0))


def mla_project(hs, mods, norm_g3, wd, q_norm, kv_norm, wuq, wukv, cos, sin, *, layer, batch):
    d = hs.shape[1]
    bmap = lambda t: (_tile_batch(t, batch), 0, _tile_block(t, batch), 0)
    return pl.pallas_call(
        _mla_proj_kernel,
        grid=(_stream_tiles(batch),),
        in_specs=[
            pl.BlockSpec((TMX, d), lambda t: (t, 0)),
            pl.BlockSpec((None, None, N_MOD, d), lambda t: (layer, _tile_mod_row(t, batch), 0, 0)),
            pl.BlockSpec((None, 1, d), lambda t: (layer * N_SUB + 1, 0, 0)),
            _resident(wd.shape),
            _resident((1, MLA_Q_RANK)),
            _resident((1, MLA_KV_RANK)),
            _resident(wuq.shape),
            _resident(wukv.shape),
            pl.BlockSpec((TMX, 128), lambda t: (_tile_block(t, batch), 0)),
            pl.BlockSpec((TMX, 128), lambda t: (_tile_block(t, batch), 0)),
        ],
        out_specs=[
            pl.BlockSpec((None, MLA_HEADS, TMX, MLA_QK), bmap),
            pl.BlockSpec((None, MLA_HEADS, TMX, MLA_QK), bmap),
            pl.BlockSpec((None, MLA_HEADS, TMX, MLA_QK), bmap),
        ],
        out_shape=[
            jax.ShapeDtypeStruct((batch, MLA_HEADS, S_ALL, MLA_QK), BF16),
            jax.ShapeDtypeStruct((batch, MLA_HEADS, S_ALL, MLA_QK), BF16),
            jax.ShapeDtypeStruct((batch, MLA_HEADS, S_ALL, MLA_QK), BF16),
        ],
        compiler_params=pltpu.CompilerParams(
            dimension_semantics=("arbitrary",), vmem_limit_bytes=VMEM_LIMIT_BYTES),
        name=f"mla_proj_l{layer}",
    )(hs, mods, norm_g3, wd, q_norm, kv_norm, wuq, wukv, cos, sin)


def _attn_kernel(q_ref, k_ref, v_ref, o_ref, *, n_chain):
    rows = q_ref.shape[0] // n_chain
    for c in range(n_chain):
        r = slice(c * rows, (c + 1) * rows)
        s = lax.dot_general(q_ref[r, :], k_ref[...], (((1,), (1,)), ((), ())), preferred_element_type=F32)
        p = jnp.exp2(s - jnp.max(s, axis=-1, keepdims=True))
        o = jnp.dot(p.astype(BF16), v_ref[...], preferred_element_type=F32)
        o_ref[r, :] = (o[:, :MLA_V] / o[:, MLA_V:MLA_V + 1]).astype(BF16)


def mla_attention(q, k, v, *, tq, n_chain, q_block0, n_q, k_rows, k_block, name):
    batch, heads, _, _ = q.shape
    return pl.pallas_call(
        functools.partial(_attn_kernel, n_chain=n_chain),
        grid=(batch, heads, n_q),
        in_specs=[
            pl.BlockSpec((None, None, tq, MLA_QK), lambda b, h, i: (b, h, q_block0 + i, 0)),
            pl.BlockSpec((None, None, k_rows, MLA_QK), lambda b, h, i: (b, h, k_block, 0)),
            pl.BlockSpec((None, None, k_rows, MLA_QK), lambda b, h, i: (b, h, k_block, 0)),
        ],
        out_specs=pl.BlockSpec((None, tq, MLA_V), lambda b, h, i: (b, i, h)),
        out_shape=jax.ShapeDtypeStruct((batch, n_q * tq, heads * MLA_V), BF16),
        compiler_params=pltpu.CompilerParams(
            dimension_semantics=("arbitrary", "arbitrary", "arbitrary"), vmem_limit_bytes=VMEM_LIMIT_BYTES),
        name=name,
    )(q, k, v)


def _out_proj_kernel(yl_ref, yc_ref, w_ref, x_ref, mod_ref, o_ref, *, n_lat_tiles, ctx_out):
    t = pl.program_id(0)

    def update(y_ref):
        y = jnp.dot(y_ref[...], w_ref[...], preferred_element_type=F32)
        o_ref[...] = x_ref[...] + mod_ref[5:6, :] * y

    @pl.when(t < n_lat_tiles)
    def _():
        update(yl_ref)

    @pl.when(t >= n_lat_tiles)
    def _():
        if ctx_out:
            update(yc_ref)
        else:
            o_ref[...] = x_ref[...]


def mixer_out_proj(y_lat, y_ctx, w, hs, mods, *, layer, batch, name):
    d = hs.shape[1]
    kdim = y_lat.shape[2]
    nl = batch * LAT_TILES
    ctx_out = y_ctx is not None
    kern = functools.partial(_out_proj_kernel, n_lat_tiles=nl, ctx_out=ctx_out)

    def lat_map(t):
        tl = jnp.minimum(t, nl - 1)
        return (tl // LAT_TILES, tl % LAT_TILES, 0)

    return pl.pallas_call(
        kern,
        grid=(_stream_tiles(batch),),
        in_specs=[
            pl.BlockSpec((None, TMX, kdim), lat_map),
            pl.BlockSpec((None, TMX, kdim), lambda t: (jnp.maximum(t - nl, 0), 0, 0)),
            _resident(w.shape),
            pl.BlockSpec((TMX, d), lambda t: (t, 0)),
            pl.BlockSpec((None, None, N_MOD, d), lambda t: (layer, _tile_mod_row(t, batch), 0, 0)),
        ],
        out_specs=pl.BlockSpec((TMX, d), lambda t: (t, 0)),
        out_shape=jax.ShapeDtypeStruct(hs.shape, F32),
        compiler_params=pltpu.CompilerParams(
            dimension_semantics=("arbitrary",), vmem_limit_bytes=VMEM_LIMIT_BYTES),
        name=name,
    )(y_lat, y_ctx if ctx_out else y_lat[:, :CTX_LEN], w, hs, mods)


def mla_layer(hs, mods, norm_g3, weights, q_norm, kv_norm, w_o, cos, sin, *, layer, batch, with_ctx_out):
    wd, wuq, wukv = weights
    q, k, v = mla_project(hs, mods, norm_g3, wd, q_norm, kv_norm, wuq, wukv, cos, sin, layer=layer, batch=batch)
    tq = 4096
    att = mla_attention(q, k, v, tq=tq, n_chain=16, q_block0=0, n_q=SEQ // tq, k_rows=S_ALL, k_block=0,
                        name=f"mla_attn_lat_l{layer}")
    att_ctx = None
    if with_ctx_out:
        att_ctx = mla_attention(q, k, v, tq=CTX_LEN, n_chain=1, q_block0=SEQ // CTX_LEN, n_q=1, k_rows=CTX_LEN,
                                k_block=SEQ // CTX_LEN, name=f"mla_attn_ctx_l{layer}")
    return mixer_out_proj(att, att_ctx, w_o, hs, mods, layer=layer, batch=batch, name=f"mla_out_l{layer}")


HG_WIN = 32
HG_NH = 8
N_TBLK = S_ALL // TMX


def _hgrn_proj_kernel(x_ref, mod_ref, g_ref, w_ref, o_ref):
    n = _adanorm(x_ref[...], g_ref[...], mod_ref[3:4, :], mod_ref[4:5, :]).astype(BF16)
    o_ref[...] = jnp.dot(n, w_ref[...], preferred_element_type=F32)


def hgrn_project(hs, mods, norm_g3, w_in, *, layer, batch):
    d = hs.shape[1]
    n_out = w_in.shape[1]
    return pl.pallas_call(
        _hgrn_proj_kernel,
        grid=(n_out // d, _stream_tiles(batch)),
        in_specs=[
            pl.BlockSpec((TMX, d), lambda j, t: (t, 0)),
            pl.BlockSpec((None, None, N_MOD, d), lambda j, t: (layer, _tile_mod_row(t, batch), 0, 0)),
            pl.BlockSpec((None, 1, d), lambda j, t: (layer * N_SUB + 1, 0, 0)),
            pl.BlockSpec((d, d), lambda j, t: (0, j)),
        ],
        out_specs=pl.BlockSpec((None, TMX, d), lambda j, t: (_tile_batch(t, batch), _tile_block(t, batch), j)),
        out_shape=jax.ShapeDtypeStruct((batch, S_ALL, n_out), F32),
        compiler_params=pltpu.CompilerParams(
            dimension_semantics=("arbitrary", "arbitrary"), vmem_limit_bytes=VMEM_LIMIT_BYTES),
        name=f"hgrn_proj_l{layer}",
    )(hs, mods, norm_g3, w_in)


def _split3(x):
    hi = x.astype(BF16)
    r1 = x - hi.astype(F32)
    mid = r1.astype(BF16)
    lo = (r1 - mid.astype(F32)).astype(BF16)
    return hi, mid, lo


def _hgrn_scan_kernel(q_ref, z_ref, v_ref, lb_ref, om_ref, tri_ref, o_ref,
                      st_ref, qd_ref, kin_ref, kte_ref, vb_ref, tot_ref, *, reverse):
    ts = pl.program_id(2)

    @pl.when(ts == 0)
    def _():
        st_ref[...] = jnp.zeros_like(st_ref)

    z = z_ref[...]
    e = jnp.exp(-jnp.abs(z))
    r = 1.0 / (1.0 + e)
    er = e * r
    pos = z >= 0
    om = om_ref[...]
    f = lb_ref[...] + om * jnp.where(pos, r, er)
    log_f = jnp.log(f)
    k = om * jnp.where(pos, er, r)
    tri = tri_ref[...]
    cum = sum(jnp.dot(tri, piece, preferred_element_type=F32) for piece in _split3(log_f))
    b_inc = cum[:TMX]
    b_rest = cum[TMX:]
    qr = q_ref[...]
    qd_ref[...] = (qr * jax.nn.sigmoid(qr) * jnp.exp(b_inc)).astype(BF16)
    kin_ref[...] = (k * jnp.exp(-b_inc)).astype(BF16)
    kte_ref[...] = (k * jnp.exp(b_rest)).astype(BF16)
    vb_ref[...] = v_ref[...].astype(BF16)
    tot = b_inc + b_rest
    dec_all = []
    for h in range(HG_NH):
        tot_ref[h] = tot[:, h * HG_HEAD_DIM:(h + 1) * HG_HEAD_DIM]
        dec_all.append(jnp.exp(tot_ref[h, pl.ds(0, TMX // HG_CHUNK, stride=HG_CHUNK), :]))

    wi = lax.broadcasted_iota(jnp.int32, (HG_WIN, HG_WIN), 0)
    wj = lax.broadcasted_iota(jnp.int32, (HG_WIN, HG_WIN), 1)
    same_chunk = (wi // HG_CHUNK) == (wj // HG_CHUNK)
    causal = same_chunk & ((wj >= wi) if reverse else (wj <= wi))

    n_win = TMX // HG_WIN
    per_win = HG_WIN // HG_CHUNK
    wins = list(range(n_win - 1, -1, -1) if reverse else range(n_win))
    chunks = list(range(per_win - 1, -1, -1) if reverse else range(per_win))
    def only_chunk(x, c):
        parts = []
        if c > 0:
            parts.append(jnp.zeros((c * HG_CHUNK, x.shape[1]), x.dtype))
        parts.append(x[c * HG_CHUNK:(c + 1) * HG_CHUNK, :])
        if c < per_win - 1:
            parts.append(jnp.zeros(((per_win - 1 - c) * HG_CHUNK, x.shape[1]), x.dtype))
        return jnp.concatenate(parts, axis=0)

    states = [st_ref[h] for h in range(HG_NH)]
    heads = range(HG_NH)
    hcols = [slice(h * HG_HEAD_DIM, (h + 1) * HG_HEAD_DIM) for h in heads]
    for w in wins:
        rows = slice(w * HG_WIN, (w + 1) * HG_WIN)
        o_intra, upd = [], []
        for h in heads:
            qd_w, kin_w, kte_w, v_w = (r[rows, hcols[h]] for r in (qd_ref, kin_ref, kte_ref, vb_ref))
            a = lax.dot_general(qd_w, kin_w, (((1,), (1,)), ((), ())), preferred_element_type=F32)
            a = jnp.where(causal, a, 0.0).astype(BF16)
            o_intra.append(jnp.dot(a, v_w, preferred_element_type=F32))
            v_t = v_ref[rows, hcols[h]].T.astype(BF16)
            rhs = jnp.concatenate([only_chunk(kte_w, c) for c in range(per_win)], axis=1)
            upd.append(jnp.dot(v_t, rhs, preferred_element_type=F32))
        entry = [dict() for _ in heads]
        for h in heads:
            st = states[h]
            for c in chunks:
                entry[h][c] = st.astype(BF16)
                ci = w * per_win + c
                st = st * dec_all[h][ci:ci + 1, :] + upd[h][:, c * HG_HEAD_DIM:(c + 1) * HG_HEAD_DIM]
            states[h] = st
        for h in heads:
            qd_w = qd_ref[rows, hcols[h]]
            lhs = jnp.concatenate([only_chunk(qd_w, c) for c in range(per_win)], axis=1)
            ent = jnp.concatenate([entry[h][c] for c in range(per_win)], axis=1)
            o_inter = lax.dot_general(lhs, ent, (((1,), (1,)), ((), ())), preferred_element_type=F32)
            o_ref[rows, hcols[h]] = o_intra[h] + o_inter
    for h in heads:
        st_ref[h] = states[h]


def _hgrn_cumsum_matrix(reverse):
    r = np.arange(TMX)[:, None]
    c = np.arange(TMX)[None, :]
    same = (r // HG_CHUNK) == (c // HG_CHUNK)
    inc, rest = ((c >= r), (c < r)) if reverse else ((c <= r), (c > r))
    return jnp.asarray(np.concatenate([same & inc, same & rest], axis=0), BF16)


def hgrn_scan(p, lb, om, *, reverse, z_col, layer):
    batch, s_all, n_out = p.shape
    d = n_out // 5
    wcols = HG_NH * HG_HEAD_DIM
    groups = d // wcols

    def tblk(ts):
        lat = (N_TBLK - 1 - ts) if reverse else (ts - 1)
        return jnp.where(ts == 0, CTX_BLOCK, lat)

    col = lambda base: (lambda b, g, ts: (b, tblk(ts), base * groups + g))
    row_spec = pl.BlockSpec((1, wcols), lambda b, g, ts: (0, g))
    kern = functools.partial(_hgrn_scan_kernel, reverse=reverse)
    return pl.pallas_call(
        kern,
        grid=(batch, groups, N_TBLK),
        in_specs=[
            pl.BlockSpec((None, TMX, wcols), col(0)),
            pl.BlockSpec((None, TMX, wcols), col(z_col)),
            pl.BlockSpec((None, TMX, wcols), col(3)),
            row_spec, row_spec, _resident((2 * TMX, TMX)),
        ],
        out_specs=pl.BlockSpec((None, TMX, wcols), lambda b, g, ts: (b, tblk(ts), g)),
        out_shape=jax.ShapeDtypeStruct((batch, s_all, d), F32),
        scratch_shapes=[
            pltpu.VMEM((HG_NH, HG_HEAD_DIM, HG_HEAD_DIM), F32),
            pltpu.VMEM((TMX, wcols), BF16), pltpu.VMEM((TMX, wcols), BF16),
            pltpu.VMEM((TMX, wcols), BF16), pltpu.VMEM((TMX, wcols), BF16),
            pltpu.VMEM((HG_NH, TMX, HG_HEAD_DIM), F32),
        ],
        compiler_params=pltpu.CompilerParams(
            dimension_semantics=("arbitrary", "arbitrary", "arbitrary"), vmem_limit_bytes=VMEM_LIMIT_BYTES),
        name=f"hgrn_scan_{'bwd' if reverse else 'fwd'}_l{layer}",
    )(p, p, p, lb, om, _hgrn_cumsum_matrix(reverse))


def _hgrn_readout_kernel(of_ref, ob_ref, gate_ref, gn_ref, w_ref, x_ref, mod_ref, o_ref, *, n_lat_tiles, ctx_out):
    t = pl.program_id(0)

    def update():
        o = of_ref[...] + ob_ref[...]
        gate = gate_ref[...]
        gs = gate * jax.nn.sigmoid(gate)
        parts = []
        for h in range(HG_HEADS):
            cols = slice(h * HG_HEAD_DIM, (h + 1) * HG_HEAD_DIM)
            parts.append((_rms(o[:, cols]) * gn_ref[:, cols] * gs[:, cols]).astype(BF16))
        y = jnp.dot(jnp.concatenate(parts, axis=1), w_ref[...], preferred_element_type=F32)
        o_ref[...] = x_ref[...] + mod_ref[5:6, :] * y

    if ctx_out:
        update()
    else:
        pl.when(t < n_lat_tiles)(update)

        @pl.when(t >= n_lat_tiles)
        def _():
            o_ref[...] = x_ref[...]


def hgrn_readout(o_f, o_b, p, g_norm, w_out, hs, mods, *, layer, batch, ctx_out):
    d = hs.shape[1]
    seq = lambda t: (_tile_batch(t, batch), _tile_block(t, batch), 0)
    kern = functools.partial(_hgrn_readout_kernel, n_lat_tiles=batch * LAT_TILES, ctx_out=ctx_out)
    return pl.pallas_call(
        kern,
        grid=(_stream_tiles(batch),),
        in_specs=[
            pl.BlockSpec((None, TMX, d), seq),
            pl.BlockSpec((None, TMX, d), seq),
            pl.BlockSpec((None, TMX, d), lambda t: (_tile_batch(t, batch), _tile_block(t, batch), 4)),
            _resident((1, d)),
            _resident(w_out.shape),
            pl.BlockSpec((TMX, d), lambda t: (t, 0)),
            pl.BlockSpec((None, None, N_MOD, d), lambda t: (layer, _tile_mod_row(t, batch), 0, 0)),
        ],
        out_specs=pl.BlockSpec((TMX, d), lambda t: (t, 0)),
        out_shape=jax.ShapeDtypeStruct(hs.shape, F32),
        compiler_params=pltpu.CompilerParams(
            dimension_semantics=("arbitrary",), vmem_limit_bytes=VMEM_LIMIT_BYTES),
        name=f"hgrn_out_l{layer}",
    )(o_f, o_b, p, g_norm, w_out, hs, mods)


def hgrn_layer(hs, mods, norm_g3, w_in, lb_fwd, lb_bwd, g_norm, w_out, *, layer, batch, with_ctx_out):
    p = hgrn_project(hs, mods, norm_g3, w_in, layer=layer, batch=batch)
    outs = []
    for reverse, lb, z_col in ((False, lb_fwd, 1), (True, lb_bwd, 2)):
        lb = lb.reshape(1, -1)
        outs.append(hgrn_scan(p, lb, 1.0 - lb, reverse=reverse, z_col=z_col, layer=layer))
    return hgrn_readout(outs[0], outs[1], p, g_norm.reshape(1, -1), w_out, hs, mods,
                        layer=layer, batch=batch, ctx_out=with_ctx_out)


FN_GC = D_MODEL // FOURIER_GROUPS
FN_TM = 512
FN_TK = 1024


def _fnet_tables(n_pos, tm):
    t = np.arange(n_pos, dtype=np.int64)
    ang = lambda k: 2.0 * np.pi * ((k[:, None] * t[None, :]) % n_pos) / n_pos
    phi = ang(np.arange(tm, dtype=np.int64))
    th = ang(np.arange(0, n_pos, tm, dtype=np.int64))
    sc = n_pos ** -0.5
    rows = np.stack([np.stack([np.cos(th), -np.sin(th)], axis=1),
                     np.stack([-np.sin(th), -np.cos(th)], axis=1)], axis=1) * sc
    rows = rows.reshape(-1, 2, n_pos)
    return (jnp.asarray(rows, F32), jnp.asarray(np.cos(phi), F32), jnp.asarray(np.sin(phi), F32))


def _fnet_channel_table():
    c = np.arange(FN_GC, dtype=np.int64)
    ang = 2.0 * np.pi * ((c[:, None] * c[None, :]) % FN_GC) / FN_GC
    return jnp.asarray(np.concatenate([np.cos(ang), np.sin(ang)], axis=1) * FN_GC ** -0.5, BF16)


def _fnet_chan_kernel(x_ref, mod_ref, g_ref, cs_ref, o_ref):
    n = _adanorm(x_ref[...], g_ref[...], mod_ref[3:4, :], mod_ref[4:5, :]).astype(BF16)
    for g in range(FOURIER_GROUPS):
        cols = slice(g * FN_GC, (g + 1) * FN_GC)
        pq = jnp.dot(n[:, cols], cs_ref[...], preferred_element_type=F32)
        o_ref[0, :, cols] = pq[:, :FN_GC].astype(BF16)
        o_ref[1, :, cols] = pq[:, FN_GC:].astype(BF16)


def fnet_channel_dft(hs, mods, norm_g3, cs, *, layer, batch):
    d = hs.shape[1]
    return pl.pallas_call(
        _fnet_chan_kernel,
        grid=(_stream_tiles(batch),),
        in_specs=[
            pl.BlockSpec((TMX, d), lambda t: (t, 0)),
            pl.BlockSpec((None, None, N_MOD, d), lambda t: (layer, _tile_mod_row(t, batch), 0, 0)),
            pl.BlockSpec((None, 1, d), lambda t: (layer * N_SUB + 1, 0, 0)),
            _resident(cs.shape),
        ],
        out_specs=pl.BlockSpec((None, 2, TMX, d), lambda t: (_tile_batch(t, batch), 0, _tile_block(t, batch), 0)),
        out_shape=jax.ShapeDtypeStruct((batch, 2, S_ALL, d), BF16),
        compiler_params=pltpu.CompilerParams(
            dimension_semantics=("arbitrary",), vmem_limit_bytes=VMEM_LIMIT_BYTES),
        name=f"fnet_chan_l{layer}",
    )(hs, mods, norm_g3, cs)


def _fnet_pos_kernel(rt_ref, cphi_ref, sphi_ref, pq_ref, o_ref, acc_ref, *, nk, tk):
    kk = pl.program_id(2)
    cols = pl.ds(pl.multiple_of((kk % (nk // 2)) * tk, tk), tk)
    tile = (rt_ref[0:1, :] * cphi_ref[:, cols] + rt_ref[1:2, :] * sphi_ref[:, cols]).astype(BF16)

    @pl.when(kk == 0)
    def _():
        acc_ref[...] = jnp.zeros_like(acc_ref)

    acc_ref[...] += jnp.dot(tile, pq_ref[...], preferred_element_type=F32)

    @pl.when(kk == nk - 1)
    def _():
        o_ref[...] = acc_ref[...].astype(BF16)


def fnet_position_dft(pq, tables, *, n_pos, tm, tk, row_blk0, name):
    batch, _, _, d = pq.shape
    rows, cphi, sphi = tables
    n_m, nkh = n_pos // tm, n_pos // tk
    nk = 2 * nkh
    kern = functools.partial(_fnet_pos_kernel, nk=nk, tk=tk)
    return pl.pallas_call(
        kern,
        grid=(batch, n_m, nk),
        in_specs=[
            pl.BlockSpec((None, 2, tk), lambda b, m, kk: (m * 2 + kk // nkh, 0, kk % nkh)),
            _resident(cphi.shape),
            _resident(sphi.shape),
            pl.BlockSpec((None, None, tk, d), lambda b, m, kk: (b, kk // nkh, row_blk0 + kk % nkh, 0)),
        ],
        out_specs=pl.BlockSpec((None, tm, d), lambda b, m, kk: (b, m, 0)),
        out_shape=jax.ShapeDtypeStruct((batch, n_pos, d), BF16),
        scratch_shapes=[pltpu.VMEM((tm, d), F32)],
        compiler_params=pltpu.CompilerParams(
            dimension_semantics=("arbitrary", "arbitrary", "arbitrary"), vmem_limit_bytes=VMEM_LIMIT_BYTES),
        name=name,
    )(rows, cphi, sphi, pq)


def fnet_layer(hs, mods, norm_g3, w_out, *, layer, batch, with_ctx_out):
    pq = fnet_channel_dft(hs, mods, norm_g3, _fnet_channel_table(), layer=layer, batch=batch)
    y = fnet_position_dft(pq, _fnet_tables(SEQ, FN_TM), n_pos=SEQ, tm=FN_TM, tk=FN_TK, row_blk0=0,
                          name=f"fnet_pos_lat_l{layer}")
    y_ctx = None
    if with_ctx_out:
        y_ctx = fnet_position_dft(pq, _fnet_tables(CTX_LEN, CTX_LEN), n_pos=CTX_LEN, tm=CTX_LEN, tk=CTX_LEN,
                                  row_blk0=SEQ // CTX_LEN, name=f"fnet_pos_ctx_l{layer}")
    return mixer_out_proj(y, y_ctx, w_out, hs, mods, layer=layer, batch=batch, name=f"fnet_out_l{layer}")


def kernel(x, c, ctx, c_ctx, mod_w, mod_b, norm_g, ffn1_w_gu, ffn1_w_down, ffn2_w_gu, ffn2_w_down,
           hgrn_w_in, hgrn_lb_logits, hgrn_g_norm, hgrn_w_out,
           mla_w_dqkv, mla_q_norm, mla_kv_norm, mla_w_uq, mla_w_ukv, mla_w_o,
           fnet_w_out, final_g):
    B, T, D = x.shape
    n_lat = B * T
    n_ctx = B * CTX_LEN
    tm, tf = 512, 512
    rows_all = n_lat + n_ctx

    lb = jnp.cumsum(jax.nn.softmax(hgrn_lb_logits.astype(jnp.float32), axis=1), axis=1)
    lb = lb - lb[:, :1]
    rope_cos, rope_sin = mla_rope_tables(T)

    cc = jnp.concatenate([c, c_ctx[None, :], jnp.zeros((MOD_ROWS - B - 1, D), F32)], axis=0)
    mods = modulation(cc, mod_w, mod_b).reshape(DEPTH, MOD_ROWS, N_MOD, D)

    w1_gu, w1_down = ffn_gu_tiles(ffn1_w_gu, tf), ffn1_w_down.astype(BF16)
    w2_gu, w2_down = ffn_gu_tiles(ffn2_w_gu, tf), ffn2_w_down.astype(BF16)
    norm_g3 = norm_g.reshape(DEPTH * N_SUB, 1, D)
    final_g2 = final_g.reshape(1, D)

    hs = jnp.concatenate([x.reshape(n_lat, D), ctx.reshape(n_ctx, D)], axis=0)
    for i in range(DEPTH):
        kind, j = i % N_MIXERS, i // N_MIXERS
        last = i == DEPTH - 1
        ctx_in = not (last and kind == 2)
        hs = ffn_sublayer(hs, mods, norm_g3, w1_gu, w1_down, final_g2, layer=i, s=0,
                          rows=rows_all if ctx_in else n_lat, tm=tm, tf=tf)
        if kind == 0:
            hs = hgrn_layer(hs, mods, norm_g3, hgrn_w_in[j].astype(BF16), lb[0, j], lb[1, j], hgrn_g_norm[j],
                            hgrn_w_out[j].astype(BF16), layer=i, batch=B, with_ctx_out=not last)
        elif kind == 1:
            hs = mla_layer(hs, mods, norm_g3, mla_weights(mla_w_dqkv[j], mla_w_uq[j], mla_w_ukv[j]),
                           mla_q_norm[j].reshape(1, -1), mla_kv_norm[j].reshape(1, -1),
                           mla_w_o[j].astype(BF16), rope_cos, rope_sin, layer=i, batch=B, with_ctx_out=not last)
        else:
            hs = fnet_layer(hs, mods, norm_g3, fnet_w_out[j].astype(BF16), layer=i, batch=B, with_ctx_out=not last)
        hs = ffn_sublayer(hs, mods, norm_g3, w2_gu, w2_down, final_g2, layer=i, s=2,
                          rows=n_lat if last else rows_all, tm=tm, tf=tf, final=last)
    return hs.reshape(B, T, D)
```

```python
import functools

import numpy as np
import jax
import jax.numpy as jnp
from jax import lax
from jax.experimental import pallas as pl
from jax.experimental.pallas import tpu as pltpu

D_MODEL = 2048
SEQ = 4096
DEPTH = 4
GRID_W = 64
CTX_LEN = 256
N_MIXERS = 3
N_SUB = 3
D_FF = 5632
RMS_EPS = 1e-6

HG_HEAD_DIM = 128
HG_HEADS = D_MODEL // HG_HEAD_DIM
HG_QF = HG_HEADS * HG_HEAD_DIM
HG_IV = HG_HEADS * HG_HEAD_DIM
HG_CHUNK = 16

MLA_HEADS = 16
MLA_Q_RANK = 512
MLA_KV_RANK = 512
MLA_NOPE = 128
MLA_ROPE = 64
MLA_V = 128
MLA_SCALE = (MLA_NOPE + MLA_ROPE) ** -0.5
ATTN_BLOCK = 128
ROPE_THETA = 10000.0
ROPE_FREQS = MLA_ROPE // 4

FOURIER_GROUPS = 8

BF16 = jnp.bfloat16
F32 = jnp.float32

VMEM_LIMIT_BYTES = 56 * 1024 * 1024
MOD_ROWS = 8
N_MOD = N_SUB * 3


def _mod_row(t, tm):
    return jnp.minimum(t // (SEQ // tm), 2)


def _mod_kernel(c_ref, w_ref, b_ref, o_ref):
    c = c_ref[...]
    a = (c * jax.nn.sigmoid(c)).astype(BF16)
    o_ref[...] = jnp.dot(a, w_ref[...].astype(BF16), preferred_element_type=F32) + b_ref[...]


def modulation(cc, mod_w, mod_b, *, tn=1024):
    depth, d, n = mod_w.shape
    return pl.pallas_call(
        _mod_kernel,
        grid=(depth, n // tn),
        in_specs=[
            pl.BlockSpec((MOD_ROWS, d), lambda i, j: (0, 0)),
            pl.BlockSpec((None, d, tn), lambda i, j: (i, 0, j)),
            pl.BlockSpec((None, 1, tn), lambda i, j: (i, 0, j)),
        ],
        out_specs=pl.BlockSpec((None, MOD_ROWS, tn), lambda i, j: (i, 0, j)),
        out_shape=jax.ShapeDtypeStruct((depth, MOD_ROWS, n), F32),
        compiler_params=pltpu.CompilerParams(
            dimension_semantics=("arbitrary", "arbitrary"), vmem_limit_bytes=VMEM_LIMIT_BYTES),
        name="modulation",
    )(cc, mod_w, mod_b.reshape(depth, 1, n))


def _adanorm(x, g, shift, scale):
    y = x * lax.rsqrt(jnp.mean(x * x, axis=-1, keepdims=True) + RMS_EPS)
    return (y * g) * (1.0 + scale) + shift


NORM_ROWS = 16


def _ffn_kernel(x_ref, mod_ref, g_ref, wg_ref, wu_ref, wd_ref, fg_ref, o_ref, n_ref, *, s, nf, final):
    f = pl.program_id(1)

    @pl.when(f == 0)
    def _():
        g, shift, scale = g_ref[...], mod_ref[3 * s:3 * s + 1, :], mod_ref[3 * s + 1:3 * s + 2, :]
        for i in range(x_ref.shape[0] // NORM_ROWS):
            r = slice(i * NORM_ROWS, (i + 1) * NORM_ROWS)
            n_ref[r, :] = _adanorm(x_ref[r, :], g, shift, scale).astype(BF16)
        o_ref[...] = jnp.zeros_like(o_ref)

    n = n_ref[...]
    gate = jnp.dot(n, wg_ref[...], preferred_element_type=F32)
    up = jnp.dot(n, wu_ref[...], preferred_element_type=F32)
    act = (gate * jax.nn.sigmoid(gate) * up).astype(BF16)
    o_ref[...] += jnp.dot(act, wd_ref[...], preferred_element_type=F32)

    @pl.when(f == nf - 1)
    def _():
        half_gate = 0.5 * mod_ref[3 * s + 2:3 * s + 3, :]
        for i in range(x_ref.shape[0] // NORM_ROWS):
            r = slice(i * NORM_ROWS, (i + 1) * NORM_ROWS)
            h = x_ref[r, :] + half_gate * o_ref[r, :]
            if final:
                h = _rms(h) * fg_ref[...]
            o_ref[r, :] = h


def ffn_sublayer(h, mods, g, w_gu, w_down, final_g, *, layer, s, rows, tm, tf, final=False):
    d = h.shape[1]
    nf = D_FF // tf
    kern = functools.partial(_ffn_kernel, s=s, nf=nf, final=final)
    return pl.pallas_call(
        kern,
        grid=(rows // tm, nf),
        in_specs=[
            pl.BlockSpec((tm, d), lambda t, f: (t, 0)),
            pl.BlockSpec((None, None, N_MOD, d), lambda t, f: (layer, _mod_row(t, tm), 0, 0)),
            pl.BlockSpec((None, 1, d), lambda t, f: (layer * N_SUB + s, 0, 0)),
            pl.BlockSpec((None, d, tf), lambda t, f: (layer, 0, f)),
            pl.BlockSpec((None, d, tf), lambda t, f: (layer, 0, f + nf)),
            pl.BlockSpec((None, tf, d), lambda t, f: (layer, f, 0)),
            pl.BlockSpec((1, d), lambda t, f: (0, 0)),
        ],
        out_specs=pl.BlockSpec((tm, d), lambda t, f: (t, 0)),
        out_shape=jax.ShapeDtypeStruct((rows if final else h.shape[0], d), F32),
        scratch_shapes=[pltpu.VMEM((tm, d), BF16)],
        compiler_params=pltpu.CompilerParams(
            dimension_semantics=("arbitrary", "arbitrary"), vmem_limit_bytes=VMEM_LIMIT_BYTES),
        name=f"ffn_l{layer}_s{s}",
    )(h, mods, g, w_gu, w_gu, w_down, final_g)


TMX = 256
S_ALL = SEQ + CTX_LEN
LAT_TILES = SEQ // TMX
CTX_BLOCK = SEQ // TMX


def _resident(shape):
    return pl.BlockSpec(shape, lambda *_: (0,) * len(shape), pipeline_mode=pl.Buffered(1))


def _stream_tiles(batch):
    return batch * (LAT_TILES + CTX_LEN // TMX)


def _tile_batch(t, batch):
    lat = t < batch * LAT_TILES
    return jnp.where(lat, t // LAT_TILES, t - batch * LAT_TILES)


def _tile_block(t, batch):
    return jnp.where(t < batch * LAT_TILES, t % LAT_TILES, CTX_BLOCK)


def _tile_mod_row(t, batch):
    return jnp.where(t < batch * LAT_TILES, t // LAT_TILES, batch)


def _rms(x):
    return x * lax.rsqrt(jnp.mean(x * x, axis=-1, keepdims=True) + RMS_EPS)


MLA_QK = 2 * MLA_NOPE
LOG2_E = 1.4426950408889634
N_DQ = MLA_Q_RANK + MLA_KV_RANK


def _mla_proj_kernel(x_ref, mod_ref, g_ref, wd_ref, qn_ref, kvn_ref, wuq_ref, wukv_ref, cos_ref, sin_ref,
                     q_ref, k_ref, v_ref):
    n = _adanorm(x_ref[...], g_ref[...], mod_ref[3:4, :], mod_ref[4:5, :]).astype(BF16)
    proj = jnp.dot(n, wd_ref[...], preferred_element_type=F32)
    cq = (_rms(proj[:, :MLA_Q_RANK]) * qn_ref[...]).astype(BF16)
    ckv = (_rms(proj[:, MLA_Q_RANK:N_DQ]) * kvn_ref[...]).astype(BF16)
    cos, sin = cos_ref[...], sin_ref[...]
    kr = proj[:, N_DQ:N_DQ + 128] * cos + proj[:, N_DQ + 128:N_DQ + 256] * sin
    kr = kr.astype(BF16)
    q = jnp.dot(cq, wuq_ref[...], preferred_element_type=F32) * (MLA_SCALE * LOG2_E)
    kv = jnp.dot(ckv, wukv_ref[...], preferred_element_type=F32)
    hn = MLA_HEADS * MLA_NOPE
    lane = lax.broadcasted_iota(jnp.int32, (x_ref.shape[0], 128), 1)
    ones_col = jnp.where(lane == 0, 1.0, 0.0).astype(BF16)
    for h in range(MLA_HEADS):
        lo = h * 128
        q_ref[h, :, 0:128] = q[:, lo:lo + 128].astype(BF16)
        qr = q[:, hn + lo:hn + lo + 128] * cos + q[:, 2 * hn + lo:2 * hn + lo + 128] * sin
        q_ref[h, :, 128:256] = qr.astype(BF16)
        k_ref[h, :, 0:128] = kv[:, lo:lo + 128].astype(BF16)
        k_ref[h, :, 128:256] = kr
        v_ref[h, :, 0:128] = kv[:, hn + lo:hn + lo + 128].astype(BF16)
        v_ref[h, :, 128:256] = ones_col


def _rope_partner(width):
    idx = jnp.arange(width)
    return idx ^ ROPE_FREQS


def mla_weights(w_dqkv, w_uq, w_ukv):
    d = w_dqkv.shape[0]
    z = jnp.zeros((d, 128 - MLA_ROPE), w_dqkv.dtype)
    kr = w_dqkv[:, N_DQ:]
    wd = jnp.concatenate([w_dqkv[:, :N_DQ], kr, z, kr[:, _rope_partner(MLA_ROPE)], z], axis=1)
    wq = w_uq.reshape(MLA_Q_RANK, MLA_HEADS, MLA_NOPE + MLA_ROPE)
    qr = wq[:, :, MLA_NOPE:]
    zq = jnp.zeros((MLA_Q_RANK, MLA_HEADS, 128 - MLA_ROPE), w_uq.dtype)
    wuq = jnp.concatenate([
        wq[:, :, :MLA_NOPE].reshape(MLA_Q_RANK, -1),
        jnp.concatenate([qr, zq], axis=2).reshape(MLA_Q_RANK, -1),
        jnp.concatenate([qr[:, :, _rope_partner(MLA_ROPE)], zq], axis=2).reshape(MLA_Q_RANK, -1)], axis=1)
    wkv = w_ukv.reshape(MLA_KV_RANK, MLA_HEADS, MLA_NOPE + MLA_V)
    wukv = jnp.concatenate([wkv[:, :, :MLA_NOPE].reshape(MLA_KV_RANK, -1),
                            wkv[:, :, MLA_NOPE:].reshape(MLA_KV_RANK, -1)], axis=1)
    return wd.astype(BF16), wuq.astype(BF16), wukv.astype(BF16)


def mla_rope_tables(n_tokens):
    rows = n_tokens // GRID_W
    r = jnp.broadcast_to(jnp.arange(rows, dtype=F32)[:, None], (rows, GRID_W)).reshape(-1)
    col = jnp.broadcast_to(jnp.arange(GRID_W, dtype=F32)[None, :], (rows, GRID_W)).reshape(-1)
    inv_freq = ROPE_THETA ** (-jnp.arange(ROPE_FREQS, dtype=F32) / ROPE_FREQS)
    ang = jnp.stack([r, col], axis=-1)[..., None] * inv_freq
    cos = jnp.broadcast_to(jnp.cos(ang)[:, :, None, :], (n_tokens, 2, 2, ROPE_FREQS)).reshape(n_tokens, MLA_ROPE)
    sin = jnp.sin(ang)
    sin = jnp.stack([-sin, sin], axis=2).reshape(n_tokens, MLA_ROPE)
    pad = jnp.zeros((n_tokens, 128 - MLA_ROPE), F32)
    cos = jnp.concatenate([cos, pad], axis=1)
    sin = jnp.concatenate([sin, pad], axis=1)
    ctx_cos = jnp.concatenate([jnp.ones((CTX_LEN, MLA_ROPE), F32), jnp.zeros((CTX_LEN, 128 - MLA_ROPE), F32)], axis=1)
    return (jnp.concatenate([cos, ctx_cos], axis=0),
            jnp.concatenate([sin, jnp.zeros((CTX_LEN, 128), F32)], axis=0))


def mla_project(hs, mods, norm_g3, wd, q_norm, kv_norm, wuq, wukv, cos, sin, *, layer, batch):
    d = hs.shape[1]
    bmap = lambda t: (_tile_batch(t, batch), 0, _tile_block(t, batch), 0)
    return pl.pallas_call(
        _mla_proj_kernel,
        grid=(_stream_tiles(batch),),
        in_specs=[
            pl.BlockSpec((TMX, d), lambda t: (t, 0)),
            pl.BlockSpec((None, None, N_MOD, d), lambda t: (layer, _tile_mod_row(t, batch), 0, 0)),
            pl.BlockSpec((None, 1, d), lambda t: (layer * N_SUB + 1, 0, 0)),
            _resident(wd.shape),
            _resident((1, MLA_Q_RANK)),
            _resident((1, MLA_KV_RANK)),
            _resident(wuq.shape),
            _resident(wukv.shape),
            pl.BlockSpec((TMX, 128), lambda t: (_tile_block(t, batch), 0)),
            pl.BlockSpec((TMX, 128), lambda t: (_tile_block(t, batch), 0)),
        ],
        out_specs=[
            pl.BlockSpec((None, MLA_HEADS, TMX, MLA_QK), bmap),
            pl.BlockSpec((None, MLA_HEADS, TMX, MLA_QK), bmap),
            pl.BlockSpec((None, MLA_HEADS, TMX, MLA_QK), bmap),
        ],
        out_shape=[
            jax.ShapeDtypeStruct((batch, MLA_HEADS, S_ALL, MLA_QK), BF16),
            jax.ShapeDtypeStruct((batch, MLA_HEADS, S_ALL, MLA_QK), BF16),
            jax.ShapeDtypeStruct((batch, MLA_HEADS, S_ALL, MLA_QK), BF16),
        ],
        compiler_params=pltpu.CompilerParams(
            dimension_semantics=("arbitrary",), vmem_limit_bytes=VMEM_LIMIT_BYTES),
        name=f"mla_proj_l{layer}",
    )(hs, mods, norm_g3, wd, q_norm, kv_norm, wuq, wukv, cos, sin)


def _attn_kernel(q_ref, k_ref, v_ref, o_ref, *, n_chain):
    rows = q_ref.shape[0] // n_chain
    for c in range(n_chain):
        r = slice(c * rows, (c + 1) * rows)
        s = lax.dot_general(q_ref[r, :], k_ref[...], (((1,), (1,)), ((), ())), preferred_element_type=F32)
        p = jnp.exp2(s - jnp.max(s, axis=-1, keepdims=True))
        o = jnp.dot(p.astype(BF16), v_ref[...], preferred_element_type=F32)
        o_ref[r, :] = (o[:, :MLA_V] / o[:, MLA_V:MLA_V + 1]).astype(BF16)


def mla_attention(q, k, v, *, tq, n_chain, q_block0, n_q, k_rows, k_block, name):
    batch, heads, _, _ = q.shape
    return pl.pallas_call(
        functools.partial(_attn_kernel, n_chain=n_chain),
        grid=(batch, heads, n_q),
        in_specs=[
            pl.BlockSpec((None, None, tq, MLA_QK), lambda b, h, i: (b, h, q_block0 + i, 0)),
            pl.BlockSpec((None, None, k_rows, MLA_QK), lambda b, h, i: (b, h, k_block, 0)),
            pl.BlockSpec((None, None, k_rows, MLA_QK), lambda b, h, i: (b, h, k_block, 0)),
        ],
        out_specs=pl.BlockSpec((None, tq, MLA_V), lambda b, h, i: (b, i, h)),
        out_shape=jax.ShapeDtypeStruct((batch, n_q * tq, heads * MLA_V), BF16),
        compiler_params=pltpu.CompilerParams(
            dimension_semantics=("arbitrary", "arbitrary", "arbitrary"), vmem_limit_bytes=VMEM_LIMIT_BYTES),
        name=name,
    )(q, k, v)


def _out_proj_kernel(yl_ref, yc_ref, w_ref, x_ref, mod_ref, o_ref, *, n_lat_tiles, ctx_out):
    t = pl.program_id(0)

    def update(y_ref):
        y = jnp.dot(y_ref[...], w_ref[...], preferred_element_type=F32)
        o_ref[...] = x_ref[...] + mod_ref[5:6, :] * y

    @pl.when(t < n_lat_tiles)
    def _():
        update(yl_ref)

    @pl.when(t >= n_lat_tiles)
    def _():
        if ctx_out:
            update(yc_ref)
        else:
            o_ref[...] = x_ref[...]


def mixer_out_proj(y_lat, y_ctx, w, hs, mods, *, layer, batch, name):
    d = hs.shape[1]
    kdim = y_lat.shape[2]
    nl = batch * LAT_TILES
    ctx_out = y_ctx is not None
    kern = functools.partial(_out_proj_kernel, n_lat_tiles=nl, ctx_out=ctx_out)

    def lat_map(t):
        tl = jnp.minimum(t, nl - 1)
        return (tl // LAT_TILES, tl % LAT_TILES, 0)

    return pl.pallas_call(
        kern,
        grid=(_stream_tiles(batch),),
        in_specs=[
            pl.BlockSpec((None, TMX, kdim), lat_map),
            pl.BlockSpec((None, TMX, kdim), lambda t: (jnp.maximum(t - nl, 0), 0, 0)),
            _resident(w.shape),
            pl.BlockSpec((TMX, d), lambda t: (t, 0)),
            pl.BlockSpec((None, None, N_MOD, d), lambda t: (layer, _tile_mod_row(t, batch), 0, 0)),
        ],
        out_specs=pl.BlockSpec((TMX, d), lambda t: (t, 0)),
        out_shape=jax.ShapeDtypeStruct(hs.shape, F32),
        compiler_params=pltpu.CompilerParams(
            dimension_semantics=("arbitrary",), vmem_limit_bytes=VMEM_LIMIT_BYTES),
        name=name,
    )(y_lat, y_ctx if ctx_out else y_lat[:, :CTX_LEN], w, hs, mods)


def mla_layer(hs, mods, norm_g3, weights, q_norm, kv_norm, w_o, cos, sin, *, layer, batch, with_ctx_out):
    wd, wuq, wukv = weights
    q, k, v = mla_project(hs, mods, norm_g3, wd, q_norm, kv_norm, wuq, wukv, cos, sin, layer=layer, batch=batch)
    tq = 4096
    att = mla_attention(q, k, v, tq=tq, n_chain=16, q_block0=0, n_q=SEQ // tq, k_rows=S_ALL, k_block=0,
                        name=f"mla_attn_lat_l{layer}")
    att_ctx = None
    if with_ctx_out:
        att_ctx = mla_attention(q, k, v, tq=CTX_LEN, n_chain=1, q_block0=SEQ // CTX_LEN, n_q=1, k_rows=CTX_LEN,
                                k_block=SEQ // CTX_LEN, name=f"mla_attn_ctx_l{layer}")
    return mixer_out_proj(att, att_ctx, w_o, hs, mods, layer=layer, batch=batch, name=f"mla_out_l{layer}")


HG_WIN = 128
HG_NH = 8
N_TBLK = S_ALL // TMX


def _hgrn_proj_kernel(x_ref, mod_ref, g_ref, w_ref, o_ref):
    n = _adanorm(x_ref[...], g_ref[...], mod_ref[3:4, :], mod_ref[4:5, :]).astype(BF16)
    o_ref[...] = jnp.dot(n, w_ref[...], preferred_element_type=F32)


HG_PROJ_TM = 512


def hgrn_project(hs, mods, norm_g3, w_in, *, layer):
    rows, d = hs.shape
    n_out = w_in.shape[1]
    tm = HG_PROJ_TM
    return pl.pallas_call(
        _hgrn_proj_kernel,
        grid=(n_out // d, rows // tm),
        in_specs=[
            pl.BlockSpec((tm, d), lambda j, t: (t, 0)),
            pl.BlockSpec((None, None, N_MOD, d), lambda j, t: (layer, _mod_row(t, tm), 0, 0)),
            pl.BlockSpec((None, 1, d), lambda j, t: (layer * N_SUB + 1, 0, 0)),
            pl.BlockSpec((d, d), lambda j, t: (0, j)),
        ],
        out_specs=pl.BlockSpec((tm, d), lambda j, t: (t, j)),
        out_shape=jax.ShapeDtypeStruct((rows, n_out), F32),
        compiler_params=pltpu.CompilerParams(
            dimension_semantics=("arbitrary", "arbitrary"), vmem_limit_bytes=VMEM_LIMIT_BYTES),
        name=f"hgrn_proj_l{layer}",
    )(hs, mods, norm_g3, w_in)


def _split3(x):
    hi = x.astype(BF16)
    r1 = x - hi.astype(F32)
    mid = r1.astype(BF16)
    lo = (r1 - mid.astype(F32)).astype(BF16)
    return hi, mid, lo


def _hgrn_scan_kernel(q_ref, z_ref, v_ref, lb_ref, om_ref, tri_ref, o_ref,
                      st_ref, qd_ref, kin_ref, kte_ref, vb_ref, tot_ref, *, reverse):
    ts = pl.program_id(2)

    @pl.when(ts == 0)
    def _():
        st_ref[...] = jnp.zeros_like(st_ref)

    z = z_ref[...]
    e = jnp.exp(-jnp.abs(z))
    r = 1.0 / (1.0 + e)
    er = e * r
    pos = z >= 0
    om = om_ref[...]
    f = lb_ref[...] + om * jnp.where(pos, r, er)
    log_f = jnp.log(f)
    k = om * jnp.where(pos, er, r)
    tri = tri_ref[...]
    cum = sum(jnp.dot(tri, piece, preferred_element_type=F32) for piece in _split3(log_f))
    b_inc = cum[:TMX]
    b_rest = cum[TMX:]
    qr = q_ref[...]
    qd_ref[...] = (qr * jax.nn.sigmoid(qr) * jnp.exp(b_inc)).astype(BF16)
    kin_ref[...] = (k * jnp.exp(-b_inc)).astype(BF16)
    kte_ref[...] = (k * jnp.exp(b_rest)).astype(BF16)
    vb_ref[...] = v_ref[...].astype(BF16)
    tot = b_inc + b_rest
    dec_all = []
    for h in range(HG_NH):
        tot_ref[h] = tot[:, h * HG_HEAD_DIM:(h + 1) * HG_HEAD_DIM]
        dec_all.append(jnp.exp(tot_ref[h, pl.ds(0, TMX // HG_CHUNK, stride=HG_CHUNK), :]))

    wi = lax.broadcasted_iota(jnp.int32, (HG_WIN, HG_WIN), 0)
    wj = lax.broadcasted_iota(jnp.int32, (HG_WIN, HG_WIN), 1)
    same_chunk = (wi // HG_CHUNK) == (wj // HG_CHUNK)
    causal = same_chunk & ((wj >= wi) if reverse else (wj <= wi))

    n_win = TMX // HG_WIN
    per_win = HG_WIN // HG_CHUNK
    wins = list(range(n_win - 1, -1, -1) if reverse else range(n_win))
    chunks = list(range(per_win - 1, -1, -1) if reverse else range(per_win))
    def only_chunk(x, c):
        parts = []
        if c > 0:
            parts.append(jnp.zeros((c * HG_CHUNK, x.shape[1]), x.dtype))
        parts.append(x[c * HG_CHUNK:(c + 1) * HG_CHUNK, :])
        if c < per_win - 1:
            parts.append(jnp.zeros(((per_win - 1 - c) * HG_CHUNK, x.shape[1]), x.dtype))
        return jnp.concatenate(parts, axis=0)

    states = [st_ref[h] for h in range(HG_NH)]
    heads = range(HG_NH)
    hcols = [slice(h * HG_HEAD_DIM, (h + 1) * HG_HEAD_DIM) for h in heads]
    for w in wins:
        rows = slice(w * HG_WIN, (w + 1) * HG_WIN)
        o_intra, upd = [], []
        for h in heads:
            qd_w, kin_w, kte_w, v_w = (r[rows, hcols[h]] for r in (qd_ref, kin_ref, kte_ref, vb_ref))
            a = lax.dot_general(qd_w, kin_w, (((1,), (1,)), ((), ())), preferred_element_type=F32)
            a = jnp.where(causal, a, 0.0).astype(BF16)
            o_intra.append(jnp.dot(a, v_w, preferred_element_type=F32))
            v_t = v_ref[rows, hcols[h]].T.astype(BF16)
            rhs = jnp.concatenate([only_chunk(kte_w, c) for c in range(per_win)], axis=1)
            upd.append(jnp.dot(v_t, rhs, preferred_element_type=F32))
        entry = [dict() for _ in heads]
        for h in heads:
            st = states[h]
            for c in chunks:
                entry[h][c] = st.astype(BF16)
                ci = w * per_win + c
                st = st * dec_all[h][ci:ci + 1, :] + upd[h][:, c * HG_HEAD_DIM:(c + 1) * HG_HEAD_DIM]
            states[h] = st
        for h in heads:
            qd_w = qd_ref[rows, hcols[h]]
            lhs = jnp.concatenate([only_chunk(qd_w, c) for c in range(per_win)], axis=1)
            ent = jnp.concatenate([entry[h][c] for c in range(per_win)], axis=1)
            o_inter = lax.dot_general(lhs, ent, (((1,), (1,)), ((), ())), preferred_element_type=F32)
            o_ref[rows, hcols[h]] = o_intra[h] + o_inter
    for h in heads:
        st_ref[h] = states[h]


def _hgrn_cumsum_matrix(reverse):
    r = np.arange(TMX)[:, None]
    c = np.arange(TMX)[None, :]
    same = (r // HG_CHUNK) == (c // HG_CHUNK)
    inc, rest = ((c >= r), (c < r)) if reverse else ((c <= r), (c > r))
    return jnp.asarray(np.concatenate([same & inc, same & rest], axis=0), BF16)


def hgrn_scan(p, lb, om, *, batch, reverse, z_col, layer):
    rows, n_out = p.shape
    d = n_out // 5
    wcols = HG_NH * HG_HEAD_DIM
    groups = d // wcols

    def rblk(b, ts):
        lat = (N_TBLK - 1 - ts) if reverse else (ts - 1)
        return jnp.where(ts == 0, batch * LAT_TILES + b, b * LAT_TILES + lat)

    col = lambda base: (lambda b, g, ts: (rblk(b, ts), base * groups + g))
    row_spec = pl.BlockSpec((1, wcols), lambda b, g, ts: (0, g))
    kern = functools.partial(_hgrn_scan_kernel, reverse=reverse)
    return pl.pallas_call(
        kern,
        grid=(batch, groups, N_TBLK),
        in_specs=[
            pl.BlockSpec((TMX, wcols), col(0)),
            pl.BlockSpec((TMX, wcols), col(z_col)),
            pl.BlockSpec((TMX, wcols), col(3)),
            row_spec, row_spec, _resident((2 * TMX, TMX)),
        ],
        out_specs=pl.BlockSpec((TMX, wcols), lambda b, g, ts: (rblk(b, ts), g)),
        out_shape=jax.ShapeDtypeStruct((rows, d), F32),
        scratch_shapes=[
            pltpu.VMEM((HG_NH, HG_HEAD_DIM, HG_HEAD_DIM), F32),
            pltpu.VMEM((TMX, wcols), BF16), pltpu.VMEM((TMX, wcols), BF16),
            pltpu.VMEM((TMX, wcols), BF16), pltpu.VMEM((TMX, wcols), BF16),
            pltpu.VMEM((HG_NH, TMX, HG_HEAD_DIM), F32),
        ],
        compiler_params=pltpu.CompilerParams(
            dimension_semantics=("arbitrary", "arbitrary", "arbitrary"), vmem_limit_bytes=VMEM_LIMIT_BYTES),
        name=f"hgrn_scan_{'bwd' if reverse else 'fwd'}_l{layer}",
    )(p, p, p, lb, om, _hgrn_cumsum_matrix(reverse))


def _hgrn_readout_kernel(of_ref, ob_ref, gate_ref, gn_ref, w_ref, x_ref, mod_ref, o_ref, *, n_lat_tiles, ctx_out):
    t = pl.program_id(0)

    def update():
        o = of_ref[...] + ob_ref[...]
        gate = gate_ref[...]
        gs = gate * jax.nn.sigmoid(gate)
        parts = []
        for h in range(HG_HEADS):
            cols = slice(h * HG_HEAD_DIM, (h + 1) * HG_HEAD_DIM)
            parts.append((_rms(o[:, cols]) * gn_ref[:, cols] * gs[:, cols]).astype(BF16))
        y = jnp.dot(jnp.concatenate(parts, axis=1), w_ref[...], preferred_element_type=F32)
        o_ref[...] = x_ref[...] + mod_ref[5:6, :] * y

    if ctx_out:
        update()
    else:
        pl.when(t < n_lat_tiles)(update)

        @pl.when(t >= n_lat_tiles)
        def _():
            o_ref[...] = x_ref[...]


def hgrn_readout(o_f, o_b, p, g_norm, w_out, hs, mods, *, layer, batch, ctx_out):
    d = hs.shape[1]
    kern = functools.partial(_hgrn_readout_kernel, n_lat_tiles=batch * LAT_TILES, ctx_out=ctx_out)
    return pl.pallas_call(
        kern,
        grid=(_stream_tiles(batch),),
        in_specs=[
            pl.BlockSpec((TMX, d), lambda t: (t, 0)),
            pl.BlockSpec((TMX, d), lambda t: (t, 0)),
            pl.BlockSpec((TMX, d), lambda t: (t, 4)),
            _resident((1, d)),
            _resident(w_out.shape),
            pl.BlockSpec((TMX, d), lambda t: (t, 0)),
            pl.BlockSpec((None, None, N_MOD, d), lambda t: (layer, _tile_mod_row(t, batch), 0, 0)),
        ],
        out_specs=pl.BlockSpec((TMX, d), lambda t: (t, 0)),
        out_shape=jax.ShapeDtypeStruct(hs.shape, F32),
        compiler_params=pltpu.CompilerParams(
            dimension_semantics=("arbitrary",), vmem_limit_bytes=VMEM_LIMIT_BYTES),
        name=f"hgrn_out_l{layer}",
    )(o_f, o_b, p, g_norm, w_out, hs, mods)


def hgrn_layer(hs, mods, norm_g3, w_in, lb_fwd, lb_bwd, g_norm, w_out, *, layer, batch, with_ctx_out):
    p = hgrn_project(hs, mods, norm_g3, w_in, layer=layer)
    outs = []
    for reverse, lb, z_col in ((False, lb_fwd, 1), (True, lb_bwd, 2)):
        lb = lb.reshape(1, -1)
        outs.append(hgrn_scan(p, lb, 1.0 - lb, batch=batch, reverse=reverse, z_col=z_col, layer=layer))
    return hgrn_readout(outs[0], outs[1], p, g_norm.reshape(1, -1), w_out, hs, mods,
                        layer=layer, batch=batch, ctx_out=with_ctx_out)


FN_GC = D_MODEL // FOURIER_GROUPS
FN_TM = 512
FN_TK = 2048


def _fnet_tables(n_pos, tm):
    t = np.arange(n_pos, dtype=np.int64)
    ang = lambda k: 2.0 * np.pi * ((k[:, None] * t[None, :]) % n_pos) / n_pos
    phi = ang(np.arange(tm, dtype=np.int64))
    th = ang(np.arange(0, n_pos, tm, dtype=np.int64))
    sc = n_pos ** -0.5
    rows = np.stack([np.stack([np.cos(th), -np.sin(th)], axis=1),
                     np.stack([-np.sin(th), -np.cos(th)], axis=1)], axis=1) * sc
    rows = rows.reshape(-1, 2, n_pos)
    return (jnp.asarray(rows, F32), jnp.asarray(np.cos(phi), F32), jnp.asarray(np.sin(phi), F32))


def _fnet_channel_table():
    c = np.arange(FN_GC, dtype=np.int64)
    ang = 2.0 * np.pi * ((c[:, None] * c[None, :]) % FN_GC) / FN_GC
    return jnp.asarray(np.concatenate([np.cos(ang), np.sin(ang)], axis=1) * FN_GC ** -0.5, BF16)


def _fnet_chan_kernel(x_ref, mod_ref, g_ref, cs_ref, o_ref):
    n = _adanorm(x_ref[...], g_ref[...], mod_ref[3:4, :], mod_ref[4:5, :]).astype(BF16)
    for g in range(FOURIER_GROUPS):
        cols = slice(g * FN_GC, (g + 1) * FN_GC)
        pq = jnp.dot(n[:, cols], cs_ref[...], preferred_element_type=F32)
        o_ref[0, :, cols] = pq[:, :FN_GC].astype(BF16)
        o_ref[1, :, cols] = pq[:, FN_GC:].astype(BF16)


def fnet_channel_dft(hs, mods, norm_g3, cs, *, layer, batch):
    d = hs.shape[1]
    return pl.pallas_call(
        _fnet_chan_kernel,
        grid=(_stream_tiles(batch),),
        in_specs=[
            pl.BlockSpec((TMX, d), lambda t: (t, 0)),
            pl.BlockSpec((None, None, N_MOD, d), lambda t: (layer, _tile_mod_row(t, batch), 0, 0)),
            pl.BlockSpec((None, 1, d), lambda t: (layer * N_SUB + 1, 0, 0)),
            _resident(cs.shape),
        ],
        out_specs=pl.BlockSpec((None, 2, TMX, d), lambda t: (_tile_batch(t, batch), 0, _tile_block(t, batch), 0)),
        out_shape=jax.ShapeDtypeStruct((batch, 2, S_ALL, d), BF16),
        compiler_params=pltpu.CompilerParams(
            dimension_semantics=("arbitrary",), vmem_limit_bytes=VMEM_LIMIT_BYTES),
        name=f"fnet_chan_l{layer}",
    )(hs, mods, norm_g3, cs)


def _fnet_pos_kernel(rt_ref, cphi_ref, sphi_ref, pq_ref, o_ref, acc_ref, *, nk, tk):
    kk = pl.program_id(2)
    cols = pl.ds(pl.multiple_of((kk % (nk // 2)) * tk, tk), tk)
    tile = (rt_ref[0:1, :] * cphi_ref[:, cols] + rt_ref[1:2, :] * sphi_ref[:, cols]).astype(BF16)

    @pl.when(kk == 0)
    def _():
        acc_ref[...] = jnp.zeros_like(acc_ref)

    acc_ref[...] += jnp.dot(tile, pq_ref[...], preferred_element_type=F32)

    @pl.when(kk == nk - 1)
    def _():
        o_ref[...] = acc_ref[...].astype(BF16)


def fnet_position_dft(pq, tables, *, n_pos, tm, tk, row_blk0, name):
    batch, _, _, d = pq.shape
    rows, cphi, sphi = tables
    n_m, nkh = n_pos // tm, n_pos // tk
    nk = 2 * nkh
    kern = functools.partial(_fnet_pos_kernel, nk=nk, tk=tk)
    return pl.pallas_call(
        kern,
        grid=(batch, n_m, nk),
        in_specs=[
            pl.BlockSpec((None, 2, tk), lambda b, m, kk: (m * 2 + kk // nkh, 0, kk % nkh)),
            _resident(cphi.shape),
            _resident(sphi.shape),
            pl.BlockSpec((None, None, tk, d), lambda b, m, kk: (b, kk // nkh, row_blk0 + kk % nkh, 0)),
        ],
        out_specs=pl.BlockSpec((None, tm, d), lambda b, m, kk: (b, m, 0)),
        out_shape=jax.ShapeDtypeStruct((batch, n_pos, d), BF16),
        scratch_shapes=[pltpu.VMEM((tm, d), F32)],
        compiler_params=pltpu.CompilerParams(
            dimension_semantics=("arbitrary", "arbitrary", "arbitrary"), vmem_limit_bytes=VMEM_LIMIT_BYTES),
        name=name,
    )(rows, cphi, sphi, pq)


def fnet_layer(hs, mods, norm_g3, w_out, *, layer, batch, with_ctx_out):
    pq = fnet_channel_dft(hs, mods, norm_g3, _fnet_channel_table(), layer=layer, batch=batch)
    y = fnet_position_dft(pq, _fnet_tables(SEQ, FN_TM), n_pos=SEQ, tm=FN_TM, tk=FN_TK, row_blk0=0,
                          name=f"fnet_pos_lat_l{layer}")
    y_ctx = None
    if with_ctx_out:
        y_ctx = fnet_position_dft(pq, _fnet_tables(CTX_LEN, CTX_LEN), n_pos=CTX_LEN, tm=CTX_LEN, tk=CTX_LEN,
                                  row_blk0=SEQ // CTX_LEN, name=f"fnet_pos_ctx_l{layer}")
    return mixer_out_proj(y, y_ctx, w_out, hs, mods, layer=layer, batch=batch, name=f"fnet_out_l{layer}")


def kernel(x, c, ctx, c_ctx, mod_w, mod_b, norm_g, ffn1_w_gu, ffn1_w_down, ffn2_w_gu, ffn2_w_down,
           hgrn_w_in, hgrn_lb_logits, hgrn_g_norm, hgrn_w_out,
           mla_w_dqkv, mla_q_norm, mla_kv_norm, mla_w_uq, mla_w_ukv, mla_w_o,
           fnet_w_out, final_g):
    B, T, D = x.shape
    n_lat = B * T
    n_ctx = B * CTX_LEN
    tm, tf = 512, 512
    rows_all = n_lat + n_ctx

    lb = jnp.cumsum(jax.nn.softmax(hgrn_lb_logits.astype(jnp.float32), axis=1), axis=1)
    lb = lb - lb[:, :1]
    rope_cos, rope_sin = mla_rope_tables(T)

    cc = jnp.concatenate([c, c_ctx[None, :], jnp.zeros((MOD_ROWS - B - 1, D), F32)], axis=0)
    mods = modulation(cc, mod_w, mod_b).reshape(DEPTH, MOD_ROWS, N_MOD, D)

    w1_gu, w1_down = ffn1_w_gu.astype(BF16), ffn1_w_down.astype(BF16)
    w2_gu, w2_down = ffn2_w_gu.astype(BF16), ffn2_w_down.astype(BF16)
    norm_g3 = norm_g.reshape(DEPTH * N_SUB, 1, D)
    final_g2 = final_g.reshape(1, D)

    hs = jnp.concatenate([x.reshape(n_lat, D), ctx.reshape(n_ctx, D)], axis=0)
    for i in range(DEPTH):
        kind, j = i % N_MIXERS, i // N_MIXERS
        last = i == DEPTH - 1
        ctx_in = not (last and kind == 2)
        hs = ffn_sublayer(hs, mods, norm_g3, w1_gu, w1_down, final_g2, layer=i, s=0,
                          rows=rows_all if ctx_in else n_lat, tm=tm, tf=tf)
        if kind == 0:
            hs = hgrn_layer(hs, mods, norm_g3, hgrn_w_in[j].astype(BF16), lb[0, j], lb[1, j], hgrn_g_norm[j],
                            hgrn_w_out[j].astype(BF16), layer=i, batch=B, with_ctx_out=not last)
        elif kind == 1:
            hs = mla_layer(hs, mods, norm_g3, mla_weights(mla_w_dqkv[j], mla_w_uq[j], mla_w_ukv[j]),
                           mla_q_norm[j].reshape(1, -1), mla_kv_norm[j].reshape(1, -1),
                           mla_w_o[j].astype(BF16), rope_cos, rope_sin, layer=i, batch=B, with_ctx_out=not last)
        else:
            hs = fnet_layer(hs, mods, norm_g3, fnet_w_out[j].astype(BF16), layer=i, batch=B, with_ctx_out=not last)
        hs = ffn_sublayer(hs, mods, norm_g3, w2_gu, w2_down, final_g2, layer=i, s=2,
                          rows=n_lat if last else rows_all, tm=tm, tf=tf, final=last)
    return hs.reshape(B, T, D)
```

```python
import functools

import numpy as np
import jax
import jax.numpy as jnp
from jax import lax
from jax.experimental import pallas as pl
from jax.experimental.pallas import tpu as pltpu

D_MODEL = 2048
SEQ = 4096
DEPTH = 4
GRID_W = 64
CTX_LEN = 256
N_MIXERS = 3
N_SUB = 3
D_FF = 5632
RMS_EPS = 1e-6

HG_HEAD_DIM = 128
HG_HEADS = D_MODEL // HG_HEAD_DIM
HG_QF = HG_HEADS * HG_HEAD_DIM
HG_IV = HG_HEADS * HG_HEAD_DIM
HG_CHUNK = 16

MLA_HEADS = 16
MLA_Q_RANK = 512
MLA_KV_RANK = 512
MLA_NOPE = 128
MLA_ROPE = 64
MLA_V = 128
MLA_SCALE = (MLA_NOPE + MLA_ROPE) ** -0.5
ATTN_BLOCK = 128
ROPE_THETA = 10000.0
ROPE_FREQS = MLA_ROPE // 4

FOURIER_GROUPS = 8

BF16 = jnp.bfloat16
F32 = jnp.float32

VMEM_LIMIT_BYTES = 56 * 1024 * 1024
MOD_ROWS = 8
N_MOD = N_SUB * 3


def _mod_row(t, tm):
    return jnp.minimum(t // (SEQ // tm), 2)


def _mod_kernel(c_ref, w_ref, b_ref, o_ref):
    c = c_ref[...]
    a = (c * jax.nn.sigmoid(c)).astype(BF16)
    o_ref[...] = jnp.dot(a, w_ref[...].astype(BF16), preferred_element_type=F32) + b_ref[...]


def modulation(cc, mod_w, mod_b, *, tn=1024):
    depth, d, n = mod_w.shape
    return pl.pallas_call(
        _mod_kernel,
        grid=(depth, n // tn),
        in_specs=[
            pl.BlockSpec((MOD_ROWS, d), lambda i, j: (0, 0)),
            pl.BlockSpec((None, d, tn), lambda i, j: (i, 0, j)),
            pl.BlockSpec((None, 1, tn), lambda i, j: (i, 0, j)),
        ],
        out_specs=pl.BlockSpec((None, MOD_ROWS, tn), lambda i, j: (i, 0, j)),
        out_shape=jax.ShapeDtypeStruct((depth, MOD_ROWS, n), F32),
        compiler_params=pltpu.CompilerParams(
            dimension_semantics=("arbitrary", "arbitrary"), vmem_limit_bytes=VMEM_LIMIT_BYTES),
        name="modulation",
    )(cc, mod_w, mod_b.reshape(depth, 1, n))


def _adanorm(x, g, shift, scale):
    y = x * lax.rsqrt(jnp.mean(x * x, axis=-1, keepdims=True) + RMS_EPS)
    return (y * g) * (1.0 + scale) + shift


NORM_ROWS = 16


def _ffn_kernel(x_ref, mod_ref, g_ref, wg_ref, wu_ref, wd_ref, fg_ref, o_ref, n_ref, *, s, nf, final):
    f = pl.program_id(1)

    @pl.when(f == 0)
    def _():
        g, shift, scale = g_ref[...], mod_ref[3 * s:3 * s + 1, :], mod_ref[3 * s + 1:3 * s + 2, :]
        for i in range(x_ref.shape[0] // NORM_ROWS):
            r = slice(i * NORM_ROWS, (i + 1) * NORM_ROWS)
            n_ref[r, :] = _adanorm(x_ref[r, :], g, shift, scale).astype(BF16)
        o_ref[...] = jnp.zeros_like(o_ref)

    n = n_ref[...]
    gate = jnp.dot(n, wg_ref[...], preferred_element_type=F32)
    up = jnp.dot(n, wu_ref[...], preferred_element_type=F32)
    act = (gate * jax.nn.sigmoid(gate) * up).astype(BF16)
    o_ref[...] += jnp.dot(act, wd_ref[...], preferred_element_type=F32)

    @pl.when(f == nf - 1)
    def _():
        half_gate = 0.5 * mod_ref[3 * s + 2:3 * s + 3, :]
        for i in range(x_ref.shape[0] // NORM_ROWS):
            r = slice(i * NORM_ROWS, (i + 1) * NORM_ROWS)
            h = x_ref[r, :] + half_gate * o_ref[r, :]
            if final:
                h = _rms(h) * fg_ref[...]
            o_ref[r, :] = h


def ffn_sublayer(h, mods, g, w_gu, w_down, final_g, *, layer, s, rows, tm, tf, final=False):
    d = h.shape[1]
    nf = D_FF // tf
    kern = functools.partial(_ffn_kernel, s=s, nf=nf, final=final)
    return pl.pallas_call(
        kern,
        grid=(rows // tm, nf),
        in_specs=[
            pl.BlockSpec((tm, d), lambda t, f: (t, 0)),
            pl.BlockSpec((None, None, N_MOD, d), lambda t, f: (layer, _mod_row(t, tm), 0, 0)),
            pl.BlockSpec((None, 1, d), lambda t, f: (layer * N_SUB + s, 0, 0)),
            pl.BlockSpec((None, d, tf), lambda t, f: (layer, 0, f)),
            pl.BlockSpec((None, d, tf), lambda t, f: (layer, 0, f + nf)),
            pl.BlockSpec((None, tf, d), lambda t, f: (layer, f, 0)),
            pl.BlockSpec((1, d), lambda t, f: (0, 0)),
        ],
        out_specs=pl.BlockSpec((tm, d), lambda t, f: (t, 0)),
        out_shape=jax.ShapeDtypeStruct((rows if final else h.shape[0], d), F32),
        scratch_shapes=[pltpu.VMEM((tm, d), BF16)],
        compiler_params=pltpu.CompilerParams(
            dimension_semantics=("arbitrary", "arbitrary"), vmem_limit_bytes=VMEM_LIMIT_BYTES),
        name=f"ffn_l{layer}_s{s}",
    )(h, mods, g, w_gu, w_gu, w_down, final_g)


TMX = 256
S_ALL = SEQ + CTX_LEN
LAT_TILES = SEQ // TMX
CTX_BLOCK = SEQ // TMX


def _resident(shape):
    return pl.BlockSpec(shape, lambda *_: (0,) * len(shape), pipeline_mode=pl.Buffered(1))


def _stream_tiles(batch):
    return batch * (LAT_TILES + CTX_LEN // TMX)


def _tile_batch(t, batch):
    lat = t < batch * LAT_TILES
    return jnp.where(lat, t // LAT_TILES, t - batch * LAT_TILES)


def _tile_block(t, batch):
    return jnp.where(t < batch * LAT_TILES, t % LAT_TILES, CTX_BLOCK)


def _tile_mod_row(t, batch):
    return jnp.where(t < batch * LAT_TILES, t // LAT_TILES, batch)


def _rms(x):
    return x * lax.rsqrt(jnp.mean(x * x, axis=-1, keepdims=True) + RMS_EPS)


MLA_QK = 2 * MLA_NOPE
LOG2_E = 1.4426950408889634
N_DQ = MLA_Q_RANK + MLA_KV_RANK


def _mla_proj_kernel(x_ref, mod_ref, g_ref, wd_ref, qn_ref, kvn_ref, wuq_ref, wukv_ref, cos_ref, sin_ref,
                     q_ref, k_ref, v_ref):
    n = _adanorm(x_ref[...], g_ref[...], mod_ref[3:4, :], mod_ref[4:5, :]).astype(BF16)
    proj = jnp.dot(n, wd_ref[...], preferred_element_type=F32)
    cq = (_rms(proj[:, :MLA_Q_RANK]) * qn_ref[...]).astype(BF16)
    ckv = (_rms(proj[:, MLA_Q_RANK:N_DQ]) * kvn_ref[...]).astype(BF16)
    cos, sin = cos_ref[...], sin_ref[...]
    kr = proj[:, N_DQ:N_DQ + 128] * cos + proj[:, N_DQ + 128:N_DQ + 256] * sin
    kr = kr.astype(BF16)
    q = jnp.dot(cq, wuq_ref[...], preferred_element_type=F32) * (MLA_SCALE * LOG2_E)
    kv = jnp.dot(ckv, wukv_ref[...], preferred_element_type=F32)
    hn = MLA_HEADS * MLA_NOPE
    lane = lax.broadcasted_iota(jnp.int32, (x_ref.shape[0], 128), 1)
    ones_col = jnp.where(lane == 0, 1.0, 0.0).astype(BF16)
    for h in range(MLA_HEADS):
        lo = h * 128
        q_ref[h, :, 0:128] = q[:, lo:lo + 128].astype(BF16)
        qr = q[:, hn + lo:hn + lo + 128] * cos + q[:, 2 * hn + lo:2 * hn + lo + 128] * sin
        q_ref[h, :, 128:256] = qr.astype(BF16)
        k_ref[h, :, 0:128] = kv[:, lo:lo + 128].astype(BF16)
        k_ref[h, :, 128:256] = kr
        v_ref[h, :, 0:128] = kv[:, hn + lo:hn + lo + 128].astype(BF16)
        v_ref[h, :, 128:256] = ones_col


def _rope_partner(width):
    idx = jnp.arange(width)
    return idx ^ ROPE_FREQS


def mla_weights(w_dqkv, w_uq, w_ukv):
    d = w_dqkv.shape[0]
    z = jnp.zeros((d, 128 - MLA_ROPE), w_dqkv.dtype)
    kr = w_dqkv[:, N_DQ:]
    wd = jnp.concatenate([w_dqkv[:, :N_DQ], kr, z, kr[:, _rope_partner(MLA_ROPE)], z], axis=1)
    wq = w_uq.reshape(MLA_Q_RANK, MLA_HEADS, MLA_NOPE + MLA_ROPE)
    qr = wq[:, :, MLA_NOPE:]
    zq = jnp.zeros((MLA_Q_RANK, MLA_HEADS, 128 - MLA_ROPE), w_uq.dtype)
    wuq = jnp.concatenate([
        wq[:, :, :MLA_NOPE].reshape(MLA_Q_RANK, -1),
        jnp.concatenate([qr, zq], axis=2).reshape(MLA_Q_RANK, -1),
        jnp.concatenate([qr[:, :, _rope_partner(MLA_ROPE)], zq], axis=2).reshape(MLA_Q_RANK, -1)], axis=1)
    wkv = w_ukv.reshape(MLA_KV_RANK, MLA_HEADS, MLA_NOPE + MLA_V)
    wukv = jnp.concatenate([wkv[:, :, :MLA_NOPE].reshape(MLA_KV_RANK, -1),
                            wkv[:, :, MLA_NOPE:].reshape(MLA_KV_RANK, -1)], axis=1)
    return wd.astype(BF16), wuq.astype(BF16), wukv.astype(BF16)


def mla_rope_tables(n_tokens):
    rows = n_tokens // GRID_W
    r = jnp.broadcast_to(jnp.arange(rows, dtype=F32)[:, None], (rows, GRID_W)).reshape(-1)
    col = jnp.broadcast_to(jnp.arange(GRID_W, dtype=F32)[None, :], (rows, GRID_W)).reshape(-1)
    inv_freq = ROPE_THETA ** (-jnp.arange(ROPE_FREQS, dtype=F32) / ROPE_FREQS)
    ang = jnp.stack([r, col], axis=-1)[..., None] * inv_freq
    cos = jnp.broadcast_to(jnp.cos(ang)[:, :, None, :], (n_tokens, 2, 2, ROPE_FREQS)).reshape(n_tokens, MLA_ROPE)
    sin = jnp.sin(ang)
    sin = jnp.stack([-sin, sin], axis=2).reshape(n_tokens, MLA_ROPE)
    pad = jnp.zeros((n_tokens, 128 - MLA_ROPE), F32)
    cos = jnp.concatenate([cos, pad], axis=1)
    sin = jnp.concatenate([sin, pad], axis=1)
    ctx_cos = jnp.concatenate([jnp.ones((CTX_LEN, MLA_ROPE), F32), jnp.zeros((CTX_LEN, 128 - MLA_ROPE), F32)], axis=1)
    return (jnp.concatenate([cos, ctx_cos], axis=0),
            jnp.concatenate([sin, jnp.zeros((CTX_LEN, 128), F32)], axis=0))


def mla_project(hs, mods, norm_g3, wd, q_norm, kv_norm, wuq, wukv, cos, sin, *, layer, batch):
    d = hs.shape[1]
    bmap = lambda t: (_tile_batch(t, batch), 0, _tile_block(t, batch), 0)
    return pl.pallas_call(
        _mla_proj_kernel,
        grid=(_stream_tiles(batch),),
        in_specs=[
            pl.BlockSpec((TMX, d), lambda t: (t, 0)),
            pl.BlockSpec((None, None, N_MOD, d), lambda t: (layer, _tile_mod_row(t, batch), 0, 0)),
            pl.BlockSpec((None, 1, d), lambda t: (layer * N_SUB + 1, 0, 0)),
            _resident(wd.shape),
            _resident((1, MLA_Q_RANK)),
            _resident((1, MLA_KV_RANK)),
            _resident(wuq.shape),
            _resident(wukv.shape),
            pl.BlockSpec((TMX, 128), lambda t: (_tile_block(t, batch), 0)),
            pl.BlockSpec((TMX, 128), lambda t: (_tile_block(t, batch), 0)),
        ],
        out_specs=[
            pl.BlockSpec((None, MLA_HEADS, TMX, MLA_QK), bmap),
            pl.BlockSpec((None, MLA_HEADS, TMX, MLA_QK), bmap),
            pl.BlockSpec((None, MLA_HEADS, TMX, MLA_QK), bmap),
        ],
        out_shape=[
            jax.ShapeDtypeStruct((batch, MLA_HEADS, S_ALL, MLA_QK), BF16),
            jax.ShapeDtypeStruct((batch, MLA_HEADS, S_ALL, MLA_QK), BF16),
            jax.ShapeDtypeStruct((batch, MLA_HEADS, S_ALL, MLA_QK), BF16),
        ],
        compiler_params=pltpu.CompilerParams(
            dimension_semantics=("arbitrary",), vmem_limit_bytes=VMEM_LIMIT_BYTES),
        name=f"mla_proj_l{layer}",
    )(hs, mods, norm_g3, wd, q_norm, kv_norm, wuq, wukv, cos, sin)


def _attn_kernel(q_ref, k_ref, v_ref, o_ref, *, n_chain):
    rows = q_ref.shape[0] // n_chain
    for c in range(n_chain):
        r = slice(c * rows, (c + 1) * rows)
        s = lax.dot_general(q_ref[r, :], k_ref[...], (((1,), (1,)), ((), ())), preferred_element_type=F32)
        p = jnp.exp2(s - jnp.max(s, axis=-1, keepdims=True))
        o = jnp.dot(p.astype(BF16), v_ref[...], preferred_element_type=F32)
        o_ref[r, :] = (o[:, :MLA_V] / o[:, MLA_V:MLA_V + 1]).astype(BF16)


def mla_attention(q, k, v, *, tq, n_chain, q_block0, n_q, k_rows, k_block, name):
    batch, heads, _, _ = q.shape
    return pl.pallas_call(
        functools.partial(_attn_kernel, n_chain=n_chain),
        grid=(batch, heads, n_q),
        in_specs=[
            pl.BlockSpec((None, None, tq, MLA_QK), lambda b, h, i: (b, h, q_block0 + i, 0)),
            pl.BlockSpec((None, None, k_rows, MLA_QK), lambda b, h, i: (b, h, k_block, 0)),
            pl.BlockSpec((None, None, k_rows, MLA_QK), lambda b, h, i: (b, h, k_block, 0)),
        ],
        out_specs=pl.BlockSpec((None, tq, MLA_V), lambda b, h, i: (b, i, h)),
        out_shape=jax.ShapeDtypeStruct((batch, n_q * tq, heads * MLA_V), BF16),
        compiler_params=pltpu.CompilerParams(
            dimension_semantics=("arbitrary", "arbitrary", "arbitrary"), vmem_limit_bytes=VMEM_LIMIT_BYTES),
        name=name,
    )(q, k, v)


def _out_proj_kernel(yl_ref, yc_ref, w_ref, x_ref, mod_ref, o_ref, *, n_lat_tiles, ctx_out):
    t = pl.program_id(0)

    def update(y_ref):
        y = jnp.dot(y_ref[...], w_ref[...], preferred_element_type=F32)
        o_ref[...] = x_ref[...] + mod_ref[5:6, :] * y

    @pl.when(t < n_lat_tiles)
    def _():
        update(yl_ref)

    @pl.when(t >= n_lat_tiles)
    def _():
        if ctx_out:
            update(yc_ref)
        else:
            o_ref[...] = x_ref[...]


def mixer_out_proj(y_lat, y_ctx, w, hs, mods, *, layer, batch, name):
    d = hs.shape[1]
    kdim = y_lat.shape[2]
    nl = batch * LAT_TILES
    ctx_out = y_ctx is not None
    kern = functools.partial(_out_proj_kernel, n_lat_tiles=nl, ctx_out=ctx_out)

    def lat_map(t):
        tl = jnp.minimum(t, nl - 1)
        return (tl // LAT_TILES, tl % LAT_TILES, 0)

    return pl.pallas_call(
        kern,
        grid=(_stream_tiles(batch),),
        in_specs=[
            pl.BlockSpec((None, TMX, kdim), lat_map),
            pl.BlockSpec((None, TMX, kdim), lambda t: (jnp.maximum(t - nl, 0), 0, 0)),
            _resident(w.shape),
            pl.BlockSpec((TMX, d), lambda t: (t, 0)),
            pl.BlockSpec((None, None, N_MOD, d), lambda t: (layer, _tile_mod_row(t, batch), 0, 0)),
        ],
        out_specs=pl.BlockSpec((TMX, d), lambda t: (t, 0)),
        out_shape=jax.ShapeDtypeStruct(hs.shape, F32),
        compiler_params=pltpu.CompilerParams(
            dimension_semantics=("arbitrary",), vmem_limit_bytes=VMEM_LIMIT_BYTES),
        name=name,
    )(y_lat, y_ctx if ctx_out else y_lat[:, :CTX_LEN], w, hs, mods)


def mla_layer(hs, mods, norm_g3, weights, q_norm, kv_norm, w_o, cos, sin, *, layer, batch, with_ctx_out):
    wd, wuq, wukv = weights
    q, k, v = mla_project(hs, mods, norm_g3, wd, q_norm, kv_norm, wuq, wukv, cos, sin, layer=layer, batch=batch)
    tq = 4096
    att = mla_attention(q, k, v, tq=tq, n_chain=16, q_block0=0, n_q=SEQ // tq, k_rows=S_ALL, k_block=0,
                        name=f"mla_attn_lat_l{layer}")
    att_ctx = None
    if with_ctx_out:
        att_ctx = mla_attention(q, k, v, tq=CTX_LEN, n_chain=1, q_block0=SEQ // CTX_LEN, n_q=1, k_rows=CTX_LEN,
                                k_block=SEQ // CTX_LEN, name=f"mla_attn_ctx_l{layer}")
    return mixer_out_proj(att, att_ctx, w_o, hs, mods, layer=layer, batch=batch, name=f"mla_out_l{layer}")


HG_WIN = 128
HG_NH = 8
N_TBLK = S_ALL // TMX


def _hgrn_proj_kernel(x_ref, mod_ref, g_ref, w_ref, o_ref):
    n = _adanorm(x_ref[...], g_ref[...], mod_ref[3:4, :], mod_ref[4:5, :]).astype(BF16)
    o_ref[...] = jnp.dot(n, w_ref[...], preferred_element_type=F32)


HG_PROJ_TM = 512


def hgrn_project(hs, mods, norm_g3, w_in, *, layer):
    rows, d = hs.shape
    n_out = w_in.shape[1]
    tm = HG_PROJ_TM
    return pl.pallas_call(
        _hgrn_proj_kernel,
        grid=(n_out // d, rows // tm),
        in_specs=[
            pl.BlockSpec((tm, d), lambda j, t: (t, 0)),
            pl.BlockSpec((None, None, N_MOD, d), lambda j, t: (layer, _mod_row(t, tm), 0, 0)),
            pl.BlockSpec((None, 1, d), lambda j, t: (layer * N_SUB + 1, 0, 0)),
            pl.BlockSpec((d, d), lambda j, t: (0, j)),
        ],
        out_specs=pl.BlockSpec((tm, d), lambda j, t: (t, j)),
        out_shape=jax.ShapeDtypeStruct((rows, n_out), F32),
        compiler_params=pltpu.CompilerParams(
            dimension_semantics=("arbitrary", "arbitrary"), vmem_limit_bytes=VMEM_LIMIT_BYTES),
        name=f"hgrn_proj_l{layer}",
    )(hs, mods, norm_g3, w_in)


def _split3(x):
    hi = x.astype(BF16)
    r1 = x - hi.astype(F32)
    mid = r1.astype(BF16)
    lo = (r1 - mid.astype(F32)).astype(BF16)
    return hi, mid, lo


def _hgrn_scan_kernel(q_ref, z_ref, v_ref, lb_ref, om_ref, tri_ref, o_ref,
                      st_ref, qd_ref, kin_ref, kte_ref, vb_ref, tot_ref, *, reverse):
    ts = pl.program_id(2)

    @pl.when(ts == 0)
    def _():
        st_ref[...] = jnp.zeros_like(st_ref)

    z = z_ref[...]
    e = jnp.exp(-jnp.abs(z))
    r = 1.0 / (1.0 + e)
    er = e * r
    pos = z >= 0
    om = om_ref[...]
    f = lb_ref[...] + om * jnp.where(pos, r, er)
    log_f = jnp.log(f)
    k = om * jnp.where(pos, er, r)
    tri = tri_ref[...]
    cum = sum(jnp.dot(tri, piece, preferred_element_type=F32) for piece in _split3(log_f))
    b_inc = cum[:TMX]
    b_rest = cum[TMX:]
    qr = q_ref[...]
    qd_ref[...] = (qr * jax.nn.sigmoid(qr) * jnp.exp(b_inc)).astype(BF16)
    kin_ref[...] = (k * jnp.exp(-b_inc)).astype(BF16)
    kte_ref[...] = (k * jnp.exp(b_rest)).astype(BF16)
    vb_ref[...] = v_ref[...].astype(BF16)
    tot = b_inc + b_rest
    dec_all = []
    for h in range(HG_NH):
        tot_ref[h] = tot[:, h * HG_HEAD_DIM:(h + 1) * HG_HEAD_DIM]
        dec_all.append(jnp.exp(tot_ref[h, pl.ds(0, TMX // HG_CHUNK, stride=HG_CHUNK), :]))

    wi = lax.broadcasted_iota(jnp.int32, (HG_WIN, HG_WIN), 0)
    wj = lax.broadcasted_iota(jnp.int32, (HG_WIN, HG_WIN), 1)
    same_chunk = (wi // HG_CHUNK) == (wj // HG_CHUNK)
    causal = same_chunk & ((wj >= wi) if reverse else (wj <= wi))

    n_win = TMX // HG_WIN
    per_win = HG_WIN // HG_CHUNK
    wins = list(range(n_win - 1, -1, -1) if reverse else range(n_win))
    chunks = list(range(per_win - 1, -1, -1) if reverse else range(per_win))
    def only_chunk(x, c):
        parts = []
        if c > 0:
            parts.append(jnp.zeros((c * HG_CHUNK, x.shape[1]), x.dtype))
        parts.append(x[c * HG_CHUNK:(c + 1) * HG_CHUNK, :])
        if c < per_win - 1:
            parts.append(jnp.zeros(((per_win - 1 - c) * HG_CHUNK, x.shape[1]), x.dtype))
        return jnp.concatenate(parts, axis=0)

    states = [st_ref[h] for h in range(HG_NH)]
    heads = range(HG_NH)
    hcols = [slice(h * HG_HEAD_DIM, (h + 1) * HG_HEAD_DIM) for h in heads]
    for w in wins:
        rows = slice(w * HG_WIN, (w + 1) * HG_WIN)
        o_intra, upd = [], []
        for h in heads:
            qd_w, kin_w, kte_w, v_w = (r[rows, hcols[h]] for r in (qd_ref, kin_ref, kte_ref, vb_ref))
            a = lax.dot_general(qd_w, kin_w, (((1,), (1,)), ((), ())), preferred_element_type=F32)
            a = jnp.where(causal, a, 0.0).astype(BF16)
            o_intra.append(jnp.dot(a, v_w, preferred_element_type=F32))
            v_t = v_ref[rows, hcols[h]].T.astype(BF16)
            rhs = jnp.concatenate([only_chunk(kte_w, c) for c in range(per_win)], axis=1)
            upd.append(jnp.dot(v_t, rhs, preferred_element_type=F32))
        entry = [dict() for _ in heads]
        for h in heads:
            st = states[h]
            for c in chunks:
                entry[h][c] = st.astype(BF16)
                ci = w * per_win + c
                st = st * dec_all[h][ci:ci + 1, :] + upd[h][:, c * HG_HEAD_DIM:(c + 1) * HG_HEAD_DIM]
            states[h] = st
        for h in heads:
            qd_w = qd_ref[rows, hcols[h]]
            lhs = jnp.concatenate([only_chunk(qd_w, c) for c in range(per_win)], axis=1)
            ent = jnp.concatenate([entry[h][c] for c in range(per_win)], axis=1)
            o_inter = lax.dot_general(lhs, ent, (((1,), (1,)), ((), ())), preferred_element_type=F32)
            o_ref[rows, hcols[h]] = o_intra[h] + o_inter
    for h in heads:
        st_ref[h] = states[h]


def _hgrn_cumsum_matrix(reverse):
    r = np.arange(TMX)[:, None]
    c = np.arange(TMX)[None, :]
    same = (r // HG_CHUNK) == (c // HG_CHUNK)
    inc, rest = ((c >= r), (c < r)) if reverse else ((c <= r), (c > r))
    return jnp.asarray(np.concatenate([same & inc, same & rest], axis=0), BF16)


def hgrn_scan(p, lb, om, *, batch, reverse, z_col, layer):
    rows, n_out = p.shape
    d = n_out // 5
    wcols = HG_NH * HG_HEAD_DIM
    groups = d // wcols

    def rblk(b, ts):
        lat = (N_TBLK - 1 - ts) if reverse else (ts - 1)
        return jnp.where(ts == 0, batch * LAT_TILES + b, b * LAT_TILES + lat)

    col = lambda base: (lambda b, g, ts: (rblk(b, ts), base * groups + g))
    row_spec = pl.BlockSpec((1, wcols), lambda b, g, ts: (0, g))
    kern = functools.partial(_hgrn_scan_kernel, reverse=reverse)
    return pl.pallas_call(
        kern,
        grid=(batch, groups, N_TBLK),
        in_specs=[
            pl.BlockSpec((TMX, wcols), col(0)),
            pl.BlockSpec((TMX, wcols), col(z_col)),
            pl.BlockSpec((TMX, wcols), col(3)),
            row_spec, row_spec, _resident((2 * TMX, TMX)),
        ],
        out_specs=pl.BlockSpec((TMX, wcols), lambda b, g, ts: (rblk(b, ts), g)),
        out_shape=jax.ShapeDtypeStruct((rows, d), F32),
        scratch_shapes=[
            pltpu.VMEM((HG_NH, HG_HEAD_DIM, HG_HEAD_DIM), F32),
            pltpu.VMEM((TMX, wcols), BF16), pltpu.VMEM((TMX, wcols), BF16),
            pltpu.VMEM((TMX, wcols), BF16), pltpu.VMEM((TMX, wcols), BF16),
            pltpu.VMEM((HG_NH, TMX, HG_HEAD_DIM), F32),
        ],
        compiler_params=pltpu.CompilerParams(
            dimension_semantics=("arbitrary", "arbitrary", "arbitrary"), vmem_limit_bytes=VMEM_LIMIT_BYTES),
        name=f"hgrn_scan_{'bwd' if reverse else 'fwd'}_l{layer}",
    )(p, p, p, lb, om, _hgrn_cumsum_matrix(reverse))


def _hgrn_readout_kernel(of_ref, ob_ref, gate_ref, gn_ref, w_ref, x_ref, mod_ref, o_ref, *, n_lat_tiles, ctx_out):
    t = pl.program_id(0)

    def update():
        o = of_ref[...] + ob_ref[...]
        gate = gate_ref[...]
        gs = gate * jax.nn.sigmoid(gate)
        parts = []
        for h in range(HG_HEADS):
            cols = slice(h * HG_HEAD_DIM, (h + 1) * HG_HEAD_DIM)
            parts.append((_rms(o[:, cols]) * gn_ref[:, cols] * gs[:, cols]).astype(BF16))
        y = jnp.dot(jnp.concatenate(parts, axis=1), w_ref[...], preferred_element_type=F32)
        o_ref[...] = x_ref[...] + mod_ref[5:6, :] * y

    if ctx_out:
        update()
    else:
        pl.when(t < n_lat_tiles)(update)

        @pl.when(t >= n_lat_tiles)
        def _():
            o_ref[...] = x_ref[...]


def hgrn_readout(o_f, o_b, p, g_norm, w_out, hs, mods, *, layer, batch, ctx_out):
    d = hs.shape[1]
    kern = functools.partial(_hgrn_readout_kernel, n_lat_tiles=batch * LAT_TILES, ctx_out=ctx_out)
    return pl.pallas_call(
        kern,
        grid=(_stream_tiles(batch),),
        in_specs=[
            pl.BlockSpec((TMX, d), lambda t: (t, 0)),
            pl.BlockSpec((TMX, d), lambda t: (t, 0)),
            pl.BlockSpec((TMX, d), lambda t: (t, 4)),
            _resident((1, d)),
            _resident(w_out.shape),
            pl.BlockSpec((TMX, d), lambda t: (t, 0)),
            pl.BlockSpec((None, None, N_MOD, d), lambda t: (layer, _tile_mod_row(t, batch), 0, 0)),
        ],
        out_specs=pl.BlockSpec((TMX, d), lambda t: (t, 0)),
        out_shape=jax.ShapeDtypeStruct(hs.shape, F32),
        compiler_params=pltpu.CompilerParams(
            dimension_semantics=("arbitrary",), vmem_limit_bytes=VMEM_LIMIT_BYTES),
        name=f"hgrn_out_l{layer}",
    )(o_f, o_b, p, g_norm, w_out, hs, mods)


def hgrn_layer(hs, mods, norm_g3, w_in, lb_fwd, lb_bwd, g_norm, w_out, *, layer, batch, with_ctx_out):
    p = hgrn_project(hs, mods, norm_g3, w_in, layer=layer)
    outs = []
    for reverse, lb, z_col in ((False, lb_fwd, 1), (True, lb_bwd, 2)):
        lb = lb.reshape(1, -1)
        outs.append(hgrn_scan(p, lb, 1.0 - lb, batch=batch, reverse=reverse, z_col=z_col, layer=layer))
    return hgrn_readout(outs[0], outs[1], p, g_norm.reshape(1, -1), w_out, hs, mods,
                        layer=layer, batch=batch, ctx_out=with_ctx_out)


FN_GC = D_MODEL // FOURIER_GROUPS
FN_TM = 512
FN_TK = 2048


def _fnet_tables(n_pos, tm):
    t = np.arange(n_pos, dtype=np.int64)
    ang = lambda k: 2.0 * np.pi * ((k[:, None] * t[None, :]) % n_pos) / n_pos
    phi = ang(np.arange(tm, dtype=np.int64))
    th = ang(np.arange(0, n_pos, tm, dtype=np.int64))
    sc = n_pos ** -0.5
    rows = np.stack([np.stack([np.cos(th), -np.sin(th)], axis=1),
                     np.stack([-np.sin(th), -np.cos(th)], axis=1)], axis=1) * sc
    rows = rows.reshape(-1, 2, n_pos)
    return (jnp.asarray(rows, F32), jnp.asarray(np.cos(phi), F32), jnp.asarray(np.sin(phi), F32))


def _fnet_channel_table():
    c = np.arange(FN_GC, dtype=np.int64)
    ang = 2.0 * np.pi * ((c[:, None] * c[None, :]) % FN_GC) / FN_GC
    return jnp.asarray(np.concatenate([np.cos(ang), np.sin(ang)], axis=1) * FN_GC ** -0.5, F32).astype(BF16)


def _fnet_chan_kernel(x_ref, mod_ref, g_ref, cs_ref, o_ref):
    n = _adanorm(x_ref[...], g_ref[...], mod_ref[3:4, :], mod_ref[4:5, :]).astype(BF16)
    for g in range(FOURIER_GROUPS):
        cols = slice(g * FN_GC, (g + 1) * FN_GC)
        pq = jnp.dot(n[:, cols], cs_ref[...], preferred_element_type=F32)
        o_ref[0, :, cols] = pq[:, :FN_GC].astype(BF16)
        o_ref[1, :, cols] = pq[:, FN_GC:].astype(BF16)


def fnet_channel_dft(hs, mods, norm_g3, cs, *, layer, batch):
    d = hs.shape[1]
    return pl.pallas_call(
        _fnet_chan_kernel,
        grid=(_stream_tiles(batch),),
        in_specs=[
            pl.BlockSpec((TMX, d), lambda t: (t, 0)),
            pl.BlockSpec((None, None, N_MOD, d), lambda t: (layer, _tile_mod_row(t, batch), 0, 0)),
            pl.BlockSpec((None, 1, d), lambda t: (layer * N_SUB + 1, 0, 0)),
            _resident(cs.shape),
        ],
        out_specs=pl.BlockSpec((None, 2, TMX, d), lambda t: (_tile_batch(t, batch), 0, _tile_block(t, batch), 0)),
        out_shape=jax.ShapeDtypeStruct((batch, 2, S_ALL, d), BF16),
        compiler_params=pltpu.CompilerParams(
            dimension_semantics=("arbitrary",), vmem_limit_bytes=VMEM_LIMIT_BYTES),
        name=f"fnet_chan_l{layer}",
    )(hs, mods, norm_g3, cs)


def _fnet_pos_kernel(rt_ref, cphi_ref, sphi_ref, pq_ref, o_ref, acc_ref, *, nk, tk):
    kk = pl.program_id(2)
    cols = pl.ds(pl.multiple_of((kk % (nk // 2)) * tk, tk), tk)
    tile = (rt_ref[0:1, :] * cphi_ref[:, cols] + rt_ref[1:2, :] * sphi_ref[:, cols]).astype(BF16)

    @pl.when(kk == 0)
    def _():
        acc_ref[...] = jnp.zeros_like(acc_ref)

    acc_ref[...] += jnp.dot(tile, pq_ref[...], preferred_element_type=F32)

    @pl.when(kk == nk - 1)
    def _():
        o_ref[...] = acc_ref[...].astype(BF16)


def fnet_position_dft(pq, tables, *, n_pos, tm, tk, row_blk0, name):
    batch, _, _, d = pq.shape
    rows, cphi, sphi = tables
    n_m, nkh = n_pos // tm, n_pos // tk
    nk = 2 * nkh
    kern = functools.partial(_fnet_pos_kernel, nk=nk, tk=tk)
    return pl.pallas_call(
        kern,
        grid=(batch, n_m, nk),
        in_specs=[
            pl.BlockSpec((None, 2, tk), lambda b, m, kk: (m * 2 + kk // nkh, 0, kk % nkh)),
            _resident(cphi.shape),
            _resident(sphi.shape),
            pl.BlockSpec((None, None, tk, d), lambda b, m, kk: (b, kk // nkh, row_blk0 + kk % nkh, 0)),
        ],
        out_specs=pl.BlockSpec((None, tm, d), lambda b, m, kk: (b, m, 0)),
        out_shape=jax.ShapeDtypeStruct((batch, n_pos, d), BF16),
        scratch_shapes=[pltpu.VMEM((tm, d), F32)],
        compiler_params=pltpu.CompilerParams(
            dimension_semantics=("arbitrary", "arbitrary", "arbitrary"), vmem_limit_bytes=VMEM_LIMIT_BYTES),
        name=name,
    )(rows, cphi, sphi, pq)


def fnet_layer(hs, mods, norm_g3, w_out, *, layer, batch, with_ctx_out):
    pq = fnet_channel_dft(hs, mods, norm_g3, _fnet_channel_table(), layer=layer, batch=batch)
    y = fnet_position_dft(pq, _fnet_tables(SEQ, FN_TM), n_pos=SEQ, tm=FN_TM, tk=FN_TK, row_blk0=0,
                          name=f"fnet_pos_lat_l{layer}")
    y_ctx = None
    if with_ctx_out:
        y_ctx = fnet_position_dft(pq, _fnet_tables(CTX_LEN, CTX_LEN), n_pos=CTX_LEN, tm=CTX_LEN, tk=CTX_LEN,
                                  row_blk0=SEQ // CTX_LEN, name=f"fnet_pos_ctx_l{layer}")
    return mixer_out_proj(y, y_ctx, w_out, hs, mods, layer=layer, batch=batch, name=f"fnet_out_l{layer}")


def kernel(x, c, ctx, c_ctx, mod_w, mod_b, norm_g, ffn1_w_gu, ffn1_w_down, ffn2_w_gu, ffn2_w_down,
           hgrn_w_in, hgrn_lb_logits, hgrn_g_norm, hgrn_w_out,
           mla_w_dqkv, mla_q_norm, mla_kv_norm, mla_w_uq, mla_w_ukv, mla_w_o,
           fnet_w_out, final_g):
    B, T, D = x.shape
    n_lat = B * T
    n_ctx = B * CTX_LEN
    tm, tf = 512, 512
    rows_all = n_lat + n_ctx

    lb = jnp.cumsum(jax.nn.softmax(hgrn_lb_logits.astype(jnp.float32), axis=1), axis=1)
    lb = lb - lb[:, :1]
    rope_cos, rope_sin = mla_rope_tables(T)

    cc = jnp.concatenate([c, c_ctx[None, :], jnp.zeros((MOD_ROWS - B - 1, D), F32)], axis=0)
    mods = modulation(cc, mod_w, mod_b).reshape(DEPTH, MOD_ROWS, N_MOD, D)

    w1_gu, w1_down = ffn1_w_gu.astype(BF16), ffn1_w_down.astype(BF16)
    w2_gu, w2_down = ffn2_w_gu.astype(BF16), ffn2_w_down.astype(BF16)
    norm_g3 = norm_g.reshape(DEPTH * N_SUB, 1, D)
    final_g2 = final_g.reshape(1, D)

    hs = jnp.concatenate([x.reshape(n_lat, D), ctx.reshape(n_ctx, D)], axis=0)
    for i in range(DEPTH):
        kind, j = i % N_MIXERS, i // N_MIXERS
        last = i == DEPTH - 1
        ctx_in = not (last and kind == 2)
        hs = ffn_sublayer(hs, mods, norm_g3, w1_gu, w1_down, final_g2, layer=i, s=0,
                          rows=rows_all if ctx_in else n_lat, tm=tm, tf=tf)
        if kind == 0:
            hs = hgrn_layer(hs, mods, norm_g3, hgrn_w_in[j].astype(BF16), lb[0, j], lb[1, j], hgrn_g_norm[j],
                            hgrn_w_out[j].astype(BF16), layer=i, batch=B, with_ctx_out=not last)
        elif kind == 1:
            hs = mla_layer(hs, mods, norm_g3, mla_weights(mla_w_dqkv[j], mla_w_uq[j], mla_w_ukv[j]),
                           mla_q_norm[j].reshape(1, -1), mla_kv_norm[j].reshape(1, -1),
                           mla_w_o[j].astype(BF16), rope_cos, rope_sin, layer=i, batch=B, with_ctx_out=not last)
        else:
            hs = fnet_layer(hs, mods, norm_g3, fnet_w_out[j].astype(BF16), layer=i, batch=B, with_ctx_out=not last)
        hs = ffn_sublayer(hs, mods, norm_g3, w2_gu, w2_down, final_g2, layer=i, s=2,
                          rows=n_lat if last else rows_all, tm=tm, tf=tf, final=last)
    return hs.reshape(B, T, D)
```

```python
import functools

import numpy as np
import jax
import jax.numpy as jnp
from jax import lax
from jax.experimental import pallas as pl
from jax.experimental.pallas import tpu as pltpu

D_MODEL = 2048
SEQ = 4096
DEPTH = 4
GRID_W = 64
CTX_LEN = 256
N_MIXERS = 3
N_SUB = 3
D_FF = 5632
RMS_EPS = 1e-6

HG_HEAD_DIM = 128
HG_HEADS = D_MODEL // HG_HEAD_DIM
HG_QF = HG_HEADS * HG_HEAD_DIM
HG_IV = HG_HEADS * HG_HEAD_DIM
HG_CHUNK = 16

MLA_HEADS = 16
MLA_Q_RANK = 512
MLA_KV_RANK = 512
MLA_NOPE = 128
MLA_ROPE = 64
MLA_V = 128
MLA_SCALE = (MLA_NOPE + MLA_ROPE) ** -0.5
ATTN_BLOCK = 128
ROPE_THETA = 10000.0
ROPE_FREQS = MLA_ROPE // 4

FOURIER_GROUPS = 8

BF16 = jnp.bfloat16
F32 = jnp.float32

VMEM_LIMIT_BYTES = 56 * 1024 * 1024
MOD_ROWS = 8
N_MOD = N_SUB * 3


def _mod_row(t, tm):
    return jnp.minimum(t // (SEQ // tm), 2)


def _mod_kernel(c_ref, w_ref, b_ref, o_ref):
    c = c_ref[...]
    a = (c * jax.nn.sigmoid(c)).astype(BF16)
    o_ref[...] = jnp.dot(a, w_ref[...].astype(BF16), preferred_element_type=F32) + b_ref[...]


def modulation(cc, mod_w, mod_b, *, tn=1024):
    depth, d, n = mod_w.shape
    return pl.pallas_call(
        _mod_kernel,
        grid=(depth, n // tn),
        in_specs=[
            pl.BlockSpec((MOD_ROWS, d), lambda i, j: (0, 0)),
            pl.BlockSpec((None, d, tn), lambda i, j: (i, 0, j)),
            pl.BlockSpec((None, 1, tn), lambda i, j: (i, 0, j)),
        ],
        out_specs=pl.BlockSpec((None, MOD_ROWS, tn), lambda i, j: (i, 0, j)),
        out_shape=jax.ShapeDtypeStruct((depth, MOD_ROWS, n), F32),
        compiler_params=pltpu.CompilerParams(
            dimension_semantics=("arbitrary", "arbitrary"), vmem_limit_bytes=VMEM_LIMIT_BYTES),
        name="modulation",
    )(cc, mod_w, mod_b.reshape(depth, 1, n))


def _adanorm(x, g, shift, scale):
    y = x * lax.rsqrt(jnp.mean(x * x, axis=-1, keepdims=True) + RMS_EPS)
    return (y * g) * (1.0 + scale) + shift


NORM_ROWS = 16


def _ffn_kernel(x_ref, mod_ref, g_ref, wg_ref, wu_ref, wd_ref, fg_ref, o_ref, n_ref, *, s, nf, final):
    f = pl.program_id(1)

    @pl.when(f == 0)
    def _():
        g, shift, scale = g_ref[...], mod_ref[3 * s:3 * s + 1, :], mod_ref[3 * s + 1:3 * s + 2, :]
        for i in range(x_ref.shape[0] // NORM_ROWS):
            r = slice(i * NORM_ROWS, (i + 1) * NORM_ROWS)
            n_ref[r, :] = _adanorm(x_ref[r, :], g, shift, scale).astype(BF16)
        o_ref[...] = jnp.zeros_like(o_ref)

    n = n_ref[...]
    gate = jnp.dot(n, wg_ref[...], preferred_element_type=F32)
    up = jnp.dot(n, wu_ref[...], preferred_element_type=F32)
    act = (gate * jax.nn.sigmoid(gate) * up).astype(BF16)
    o_ref[...] += jnp.dot(act, wd_ref[...], preferred_element_type=F32)

    @pl.when(f == nf - 1)
    def _():
        half_gate = 0.5 * mod_ref[3 * s + 2:3 * s + 3, :]
        for i in range(x_ref.shape[0] // NORM_ROWS):
            r = slice(i * NORM_ROWS, (i + 1) * NORM_ROWS)
            h = x_ref[r, :] + half_gate * o_ref[r, :]
            if final:
                h = _rms(h) * fg_ref[...]
            o_ref[r, :] = h


def ffn_sublayer(h, mods, g, w_gu, w_down, final_g, *, layer, s, rows, tm, tf, final=False):
    d = h.shape[1]
    nf = D_FF // tf
    kern = functools.partial(_ffn_kernel, s=s, nf=nf, final=final)
    return pl.pallas_call(
        kern,
        grid=(rows // tm, nf),
        in_specs=[
            pl.BlockSpec((tm, d), lambda t, f: (t, 0)),
            pl.BlockSpec((None, None, N_MOD, d), lambda t, f: (layer, _mod_row(t, tm), 0, 0)),
            pl.BlockSpec((None, 1, d), lambda t, f: (layer * N_SUB + s, 0, 0)),
            pl.BlockSpec((None, d, tf), lambda t, f: (layer, 0, f)),
            pl.BlockSpec((None, d, tf), lambda t, f: (layer, 0, f + nf)),
            pl.BlockSpec((None, tf, d), lambda t, f: (layer, f, 0)),
            pl.BlockSpec((1, d), lambda t, f: (0, 0)),
        ],
        out_specs=pl.BlockSpec((tm, d), lambda t, f: (t, 0)),
        out_shape=jax.ShapeDtypeStruct((rows if final else h.shape[0], d), F32),
        scratch_shapes=[pltpu.VMEM((tm, d), BF16)],
        compiler_params=pltpu.CompilerParams(
            dimension_semantics=("arbitrary", "arbitrary"), vmem_limit_bytes=VMEM_LIMIT_BYTES),
        name=f"ffn_l{layer}_s{s}",
    )(h, mods, g, w_gu, w_gu, w_down, final_g)


TMX = 256
S_ALL = SEQ + CTX_LEN
LAT_TILES = SEQ // TMX
CTX_BLOCK = SEQ // TMX


def _resident(shape):
    return pl.BlockSpec(shape, lambda *_: (0,) * len(shape), pipeline_mode=pl.Buffered(1))


def _stream_tiles(batch):
    return batch * (LAT_TILES + CTX_LEN // TMX)


def _tile_batch(t, batch):
    lat = t < batch * LAT_TILES
    return jnp.where(lat, t // LAT_TILES, t - batch * LAT_TILES)


def _tile_block(t, batch):
    return jnp.where(t < batch * LAT_TILES, t % LAT_TILES, CTX_BLOCK)


def _tile_mod_row(t, batch):
    return jnp.where(t < batch * LAT_TILES, t // LAT_TILES, batch)


def _rms(x):
    return x * lax.rsqrt(jnp.mean(x * x, axis=-1, keepdims=True) + RMS_EPS)


MLA_QK = 2 * MLA_NOPE
LOG2_E = 1.4426950408889634
N_DQ = MLA_Q_RANK + MLA_KV_RANK


def _mla_proj_kernel(x_ref, mod_ref, g_ref, wd_ref, qn_ref, kvn_ref, wuq_ref, wukv_ref, cos_ref, sin_ref,
                     q_ref, k_ref, v_ref):
    n = _adanorm(x_ref[...], g_ref[...], mod_ref[3:4, :], mod_ref[4:5, :]).astype(BF16)
    proj = jnp.dot(n, wd_ref[...], preferred_element_type=F32)
    cq = (_rms(proj[:, :MLA_Q_RANK]) * qn_ref[...]).astype(BF16)
    ckv = (_rms(proj[:, MLA_Q_RANK:N_DQ]) * kvn_ref[...]).astype(BF16)
    cos, sin = cos_ref[...], sin_ref[...]
    kr = proj[:, N_DQ:N_DQ + 128] * cos + proj[:, N_DQ + 128:N_DQ + 256] * sin
    kr = kr.astype(BF16)
    q = jnp.dot(cq, wuq_ref[...], preferred_element_type=F32) * (MLA_SCALE * LOG2_E)
    kv = jnp.dot(ckv, wukv_ref[...], preferred_element_type=F32)
    hn = MLA_HEADS * MLA_NOPE
    lane = lax.broadcasted_iota(jnp.int32, (x_ref.shape[0], 128), 1)
    ones_col = jnp.where(lane == 0, 1.0, 0.0).astype(BF16)
    for h in range(MLA_HEADS):
        lo = h * 128
        q_ref[h, :, 0:128] = q[:, lo:lo + 128].astype(BF16)
        qr = q[:, hn + lo:hn + lo + 128] * cos + q[:, 2 * hn + lo:2 * hn + lo + 128] * sin
        q_ref[h, :, 128:256] = qr.astype(BF16)
        k_ref[h, :, 0:128] = kv[:, lo:lo + 128].astype(BF16)
        k_ref[h, :, 128:256] = kr
        v_ref[h, :, 0:128] = kv[:, hn + lo:hn + lo + 128].astype(BF16)
        v_ref[h, :, 128:256] = ones_col


def _rope_partner(width):
    idx = jnp.arange(width)
    return idx ^ ROPE_FREQS


def mla_weights(w_dqkv, w_uq, w_ukv):
    d = w_dqkv.shape[0]
    z = jnp.zeros((d, 128 - MLA_ROPE), w_dqkv.dtype)
    kr = w_dqkv[:, N_DQ:]
    wd = jnp.concatenate([w_dqkv[:, :N_DQ], kr, z, kr[:, _rope_partner(MLA_ROPE)], z], axis=1)
    wq = w_uq.reshape(MLA_Q_RANK, MLA_HEADS, MLA_NOPE + MLA_ROPE)
    qr = wq[:, :, MLA_NOPE:]
    zq = jnp.zeros((MLA_Q_RANK, MLA_HEADS, 128 - MLA_ROPE), w_uq.dtype)
    wuq = jnp.concatenate([
        wq[:, :, :MLA_NOPE].reshape(MLA_Q_RANK, -1),
        jnp.concatenate([qr, zq], axis=2).reshape(MLA_Q_RANK, -1),
        jnp.concatenate([qr[:, :, _rope_partner(MLA_ROPE)], zq], axis=2).reshape(MLA_Q_RANK, -1)], axis=1)
    wkv = w_ukv.reshape(MLA_KV_RANK, MLA_HEADS, MLA_NOPE + MLA_V)
    wukv = jnp.concatenate([wkv[:, :, :MLA_NOPE].reshape(MLA_KV_RANK, -1),
                            wkv[:, :, MLA_NOPE:].reshape(MLA_KV_RANK, -1)], axis=1)
    return wd.astype(BF16), wuq.astype(BF16), wukv.astype(BF16)


def mla_rope_tables(n_tokens):
    rows = n_tokens // GRID_W
    r = jnp.broadcast_to(jnp.arange(rows, dtype=F32)[:, None], (rows, GRID_W)).reshape(-1)
    col = jnp.broadcast_to(jnp.arange(GRID_W, dtype=F32)[None, :], (rows, GRID_W)).reshape(-1)
    inv_freq = ROPE_THETA ** (-jnp.arange(ROPE_FREQS, dtype=F32) / ROPE_FREQS)
    ang = jnp.stack([r, col], axis=-1)[..., None] * inv_freq
    cos = jnp.broadcast_to(jnp.cos(ang)[:, :, None, :], (n_tokens, 2, 2, ROPE_FREQS)).reshape(n_tokens, MLA_ROPE)
    sin = jnp.sin(ang)
    sin = jnp.stack([-sin, sin], axis=2).reshape(n_tokens, MLA_ROPE)
    pad = jnp.zeros((n_tokens, 128 - MLA_ROPE), F32)
    cos = jnp.concatenate([cos, pad], axis=1)
    sin = jnp.concatenate([sin, pad], axis=1)
    ctx_cos = jnp.concatenate([jnp.ones((CTX_LEN, MLA_ROPE), F32), jnp.zeros((CTX_LEN, 128 - MLA_ROPE), F32)], axis=1)
    return (jnp.concatenate([cos, ctx_cos], axis=0),
            jnp.concatenate([sin, jnp.zeros((CTX_LEN, 128), F32)], axis=0))


def mla_project(hs, mods, norm_g3, wd, q_norm, kv_norm, wuq, wukv, cos, sin, *, layer, batch):
    d = hs.shape[1]
    bmap = lambda t: (_tile_batch(t, batch), 0, _tile_block(t, batch), 0)
    return pl.pallas_call(
        _mla_proj_kernel,
        grid=(_stream_tiles(batch),),
        in_specs=[
            pl.BlockSpec((TMX, d), lambda t: (t, 0)),
            pl.BlockSpec((None, None, N_MOD, d), lambda t: (layer, _tile_mod_row(t, batch), 0, 0)),
            pl.BlockSpec((None, 1, d), lambda t: (layer * N_SUB + 1, 0, 0)),
            _resident(wd.shape),
            _resident((1, MLA_Q_RANK)),
            _resident((1, MLA_KV_RANK)),
            _resident(wuq.shape),
            _resident(wukv.shape),
            pl.BlockSpec((TMX, 128), lambda t: (_tile_block(t, batch), 0)),
            pl.BlockSpec((TMX, 128), lambda t: (_tile_block(t, batch), 0)),
        ],
        out_specs=[
            pl.BlockSpec((None, MLA_HEADS, TMX, MLA_QK), bmap),
            pl.BlockSpec((None, MLA_HEADS, TMX, MLA_QK), bmap),
            pl.BlockSpec((None, MLA_HEADS, TMX, MLA_QK), bmap),
        ],
        out_shape=[
            jax.ShapeDtypeStruct((batch, MLA_HEADS, S_ALL, MLA_QK), BF16),
            jax.ShapeDtypeStruct((batch, MLA_HEADS, S_ALL, MLA_QK), BF16),
            jax.ShapeDtypeStruct((batch, MLA_HEADS, S_ALL, MLA_QK), BF16),
        ],
        compiler_params=pltpu.CompilerParams(
            dimension_semantics=("arbitrary",), vmem_limit_bytes=VMEM_LIMIT_BYTES),
        name=f"mla_proj_l{layer}",
    )(hs, mods, norm_g3, wd, q_norm, kv_norm, wuq, wukv, cos, sin)


def _attn_kernel(q_ref, k_ref, v_ref, o_ref, *, n_chain):
    rows = q_ref.shape[0] // n_chain
    for c in range(n_chain):
        r = slice(c * rows, (c + 1) * rows)
        s = lax.dot_general(q_ref[r, :], k_ref[...], (((1,), (1,)), ((), ())), preferred_element_type=F32)
        p = jnp.exp2(s - jnp.max(s, axis=-1, keepdims=True))
        o = jnp.dot(p.astype(BF16), v_ref[...], preferred_element_type=F32)
        o_ref[r, :] = (o[:, :MLA_V] / o[:, MLA_V:MLA_V + 1]).astype(BF16)


def mla_attention(q, k, v, *, tq, n_chain, q_block0, n_q, k_rows, k_block, name):
    batch, heads, _, _ = q.shape
    return pl.pallas_call(
        functools.partial(_attn_kernel, n_chain=n_chain),
        grid=(batch, heads, n_q),
        in_specs=[
            pl.BlockSpec((None, None, tq, MLA_QK), lambda b, h, i: (b, h, q_block0 + i, 0)),
            pl.BlockSpec((None, None, k_rows, MLA_QK), lambda b, h, i: (b, h, k_block, 0)),
            pl.BlockSpec((None, None, k_rows, MLA_QK), lambda b, h, i: (b, h, k_block, 0)),
        ],
        out_specs=pl.BlockSpec((None, tq, MLA_V), lambda b, h, i: (b, i, h)),
        out_shape=jax.ShapeDtypeStruct((batch, n_q * tq, heads * MLA_V), BF16),
        compiler_params=pltpu.CompilerParams(
            dimension_semantics=("arbitrary", "arbitrary", "arbitrary"), vmem_limit_bytes=VMEM_LIMIT_BYTES),
        name=name,
    )(q, k, v)


def _out_proj_kernel(yl_ref, yc_ref, w_ref, x_ref, mod_ref, o_ref, *, n_lat_tiles, ctx_out):
    t = pl.program_id(0)

    def update(y_ref):
        y = jnp.dot(y_ref[...], w_ref[...], preferred_element_type=F32)
        o_ref[...] = x_ref[...] + mod_ref[5:6, :] * y

    @pl.when(t < n_lat_tiles)
    def _():
        update(yl_ref)

    @pl.when(t >= n_lat_tiles)
    def _():
        if ctx_out:
            update(yc_ref)
        else:
            o_ref[...] = x_ref[...]


def mixer_out_proj(y_lat, y_ctx, w, hs, mods, *, layer, batch, name):
    d = hs.shape[1]
    kdim = y_lat.shape[2]
    nl = batch * LAT_TILES
    ctx_out = y_ctx is not None
    kern = functools.partial(_out_proj_kernel, n_lat_tiles=nl, ctx_out=ctx_out)

    def lat_map(t):
        tl = jnp.minimum(t, nl - 1)
        return (tl // LAT_TILES, tl % LAT_TILES, 0)

    return pl.pallas_call(
        kern,
        grid=(_stream_tiles(batch),),
        in_specs=[
            pl.BlockSpec((None, TMX, kdim), lat_map),
            pl.BlockSpec((None, TMX, kdim), lambda t: (jnp.maximum(t - nl, 0), 0, 0)),
            _resident(w.shape),
            pl.BlockSpec((TMX, d), lambda t: (t, 0)),
            pl.BlockSpec((None, None, N_MOD, d), lambda t: (layer, _tile_mod_row(t, batch), 0, 0)),
        ],
        out_specs=pl.BlockSpec((TMX, d), lambda t: (t, 0)),
        out_shape=jax.ShapeDtypeStruct(hs.shape, F32),
        compiler_params=pltpu.CompilerParams(
            dimension_semantics=("arbitrary",), vmem_limit_bytes=VMEM_LIMIT_BYTES),
        name=name,
    )(y_lat, y_ctx if ctx_out else y_lat[:, :CTX_LEN], w, hs, mods)


def mla_layer(hs, mods, norm_g3, weights, q_norm, kv_norm, w_o, cos, sin, *, layer, batch, with_ctx_out):
    wd, wuq, wukv = weights
    q, k, v = mla_project(hs, mods, norm_g3, wd, q_norm, kv_norm, wuq, wukv, cos, sin, layer=layer, batch=batch)
    tq = 4096
    att = mla_attention(q, k, v, tq=tq, n_chain=16, q_block0=0, n_q=SEQ // tq, k_rows=S_ALL, k_block=0,
                        name=f"mla_attn_lat_l{layer}")
    att_ctx = None
    if with_ctx_out:
        att_ctx = mla_attention(q, k, v, tq=CTX_LEN, n_chain=1, q_block0=SEQ // CTX_LEN, n_q=1, k_rows=CTX_LEN,
                                k_block=SEQ // CTX_LEN, name=f"mla_attn_ctx_l{layer}")
    return mixer_out_proj(att, att_ctx, w_o, hs, mods, layer=layer, batch=batch, name=f"mla_out_l{layer}")


HG_WIN = 128
HG_NH = 8
N_TBLK = S_ALL // TMX


def _hgrn_proj_kernel(x_ref, mod_ref, g_ref, w_ref, o_ref):
    n = _adanorm(x_ref[...], g_ref[...], mod_ref[3:4, :], mod_ref[4:5, :]).astype(BF16)
    o_ref[...] = jnp.dot(n, w_ref[...], preferred_element_type=F32)


HG_PROJ_TM = 512


def hgrn_project(hs, mods, norm_g3, w_in, *, layer, j):
    rows, d = hs.shape
    n_out = w_in.shape[2]
    tm = HG_PROJ_TM
    return pl.pallas_call(
        _hgrn_proj_kernel,
        grid=(n_out // d, rows // tm),
        in_specs=[
            pl.BlockSpec((tm, d), lambda c, t: (t, 0)),
            pl.BlockSpec((None, None, N_MOD, d), lambda c, t: (layer, _mod_row(t, tm), 0, 0)),
            pl.BlockSpec((None, 1, d), lambda c, t: (layer * N_SUB + 1, 0, 0)),
            pl.BlockSpec((None, d, d), lambda c, t: (j, 0, c)),
        ],
        out_specs=pl.BlockSpec((tm, d), lambda c, t: (t, c)),
        out_shape=jax.ShapeDtypeStruct((rows, n_out), F32),
        compiler_params=pltpu.CompilerParams(
            dimension_semantics=("arbitrary", "arbitrary"), vmem_limit_bytes=VMEM_LIMIT_BYTES),
        name=f"hgrn_proj_l{layer}",
    )(hs, mods, norm_g3, w_in)


def _split3(x):
    hi = x.astype(BF16)
    r1 = x - hi.astype(F32)
    mid = r1.astype(BF16)
    lo = (r1 - mid.astype(F32)).astype(BF16)
    return hi, mid, lo


def _hgrn_scan_kernel(q_ref, z_ref, v_ref, lb_ref, om_ref, tri_ref, o_ref,
                      st_ref, qd_ref, kin_ref, kte_ref, vb_ref, tot_ref, *, reverse):
    ts = pl.program_id(2)

    @pl.when(ts == 0)
    def _():
        st_ref[...] = jnp.zeros_like(st_ref)

    z = z_ref[...]
    e = jnp.exp(-jnp.abs(z))
    r = 1.0 / (1.0 + e)
    er = e * r
    pos = z >= 0
    om = om_ref[...]
    f = lb_ref[...] + om * jnp.where(pos, r, er)
    log_f = jnp.log(f)
    k = om * jnp.where(pos, er, r)
    tri = tri_ref[...]
    cum = sum(jnp.dot(tri, piece, preferred_element_type=F32) for piece in _split3(log_f))
    b_inc = cum[:TMX]
    b_rest = cum[TMX:]
    qr = q_ref[...]
    qd_ref[...] = (qr * jax.nn.sigmoid(qr) * jnp.exp(b_inc)).astype(BF16)
    kin_ref[...] = (k * jnp.exp(-b_inc)).astype(BF16)
    kte_ref[...] = (k * jnp.exp(b_rest)).astype(BF16)
    vb_ref[...] = v_ref[...].astype(BF16)
    tot = b_inc + b_rest
    dec_all = []
    for h in range(HG_NH):
        tot_ref[h] = tot[:, h * HG_HEAD_DIM:(h + 1) * HG_HEAD_DIM]
        dec_all.append(jnp.exp(tot_ref[h, pl.ds(0, TMX // HG_CHUNK, stride=HG_CHUNK), :]))

    wi = lax.broadcasted_iota(jnp.int32, (HG_WIN, HG_WIN), 0)
    wj = lax.broadcasted_iota(jnp.int32, (HG_WIN, HG_WIN), 1)
    same_chunk = (wi // HG_CHUNK) == (wj // HG_CHUNK)
    causal = same_chunk & ((wj >= wi) if reverse else (wj <= wi))

    n_win = TMX // HG_WIN
    per_win = HG_WIN // HG_CHUNK
    wins = list(range(n_win - 1, -1, -1) if reverse else range(n_win))
    chunks = list(range(per_win - 1, -1, -1) if reverse else range(per_win))
    def only_chunk(x, c):
        parts = []
        if c > 0:
            parts.append(jnp.zeros((c * HG_CHUNK, x.shape[1]), x.dtype))
        parts.append(x[c * HG_CHUNK:(c + 1) * HG_CHUNK, :])
        if c < per_win - 1:
            parts.append(jnp.zeros(((per_win - 1 - c) * HG_CHUNK, x.shape[1]), x.dtype))
        return jnp.concatenate(parts, axis=0)

    states = [st_ref[h] for h in range(HG_NH)]
    heads = range(HG_NH)
    hcols = [slice(h * HG_HEAD_DIM, (h + 1) * HG_HEAD_DIM) for h in heads]
    for w in wins:
        rows = slice(w * HG_WIN, (w + 1) * HG_WIN)
        o_intra, upd = [], []
        for h in heads:
            qd_w, kin_w, kte_w, v_w = (r[rows, hcols[h]] for r in (qd_ref, kin_ref, kte_ref, vb_ref))
            a = lax.dot_general(qd_w, kin_w, (((1,), (1,)), ((), ())), preferred_element_type=F32)
            a = jnp.where(causal, a, 0.0).astype(BF16)
            o_intra.append(jnp.dot(a, v_w, preferred_element_type=F32))
            v_t = v_ref[rows, hcols[h]].T.astype(BF16)
            rhs = jnp.concatenate([only_chunk(kte_w, c) for c in range(per_win)], axis=1)
            upd.append(jnp.dot(v_t, rhs, preferred_element_type=F32))
        entry = [dict() for _ in heads]
        for h in heads:
            st = states[h]
            for c in chunks:
                entry[h][c] = st.astype(BF16)
                ci = w * per_win + c
                st = st * dec_all[h][ci:ci + 1, :] + upd[h][:, c * HG_HEAD_DIM:(c + 1) * HG_HEAD_DIM]
            states[h] = st
        for h in heads:
            qd_w = qd_ref[rows, hcols[h]]
            lhs = jnp.concatenate([only_chunk(qd_w, c) for c in range(per_win)], axis=1)
            ent = jnp.concatenate([entry[h][c] for c in range(per_win)], axis=1)
            o_inter = lax.dot_general(lhs, ent, (((1,), (1,)), ((), ())), preferred_element_type=F32)
            o_ref[rows, hcols[h]] = o_intra[h] + o_inter
    for h in heads:
        st_ref[h] = states[h]


def _hgrn_cumsum_matrix(reverse):
    r = np.arange(TMX)[:, None]
    c = np.arange(TMX)[None, :]
    same = (r // HG_CHUNK) == (c // HG_CHUNK)
    inc, rest = ((c >= r), (c < r)) if reverse else ((c <= r), (c > r))
    return jnp.asarray(np.concatenate([same & inc, same & rest], axis=0), BF16)


def hgrn_scan(p, lb, om, *, batch, reverse, z_col, layer):
    rows, n_out = p.shape
    d = n_out // 5
    wcols = HG_NH * HG_HEAD_DIM
    groups = d // wcols

    def rblk(b, ts):
        lat = (N_TBLK - 1 - ts) if reverse else (ts - 1)
        return jnp.where(ts == 0, batch * LAT_TILES + b, b * LAT_TILES + lat)

    col = lambda base: (lambda b, g, ts: (rblk(b, ts), base * groups + g))
    row_spec = pl.BlockSpec((1, wcols), lambda b, g, ts: (0, g))
    kern = functools.partial(_hgrn_scan_kernel, reverse=reverse)
    return pl.pallas_call(
        kern,
        grid=(batch, groups, N_TBLK),
        in_specs=[
            pl.BlockSpec((TMX, wcols), col(0)),
            pl.BlockSpec((TMX, wcols), col(z_col)),
            pl.BlockSpec((TMX, wcols), col(3)),
            row_spec, row_spec, _resident((2 * TMX, TMX)),
        ],
        out_specs=pl.BlockSpec((TMX, wcols), lambda b, g, ts: (rblk(b, ts), g)),
        out_shape=jax.ShapeDtypeStruct((rows, d), F32),
        scratch_shapes=[
            pltpu.VMEM((HG_NH, HG_HEAD_DIM, HG_HEAD_DIM), F32),
            pltpu.VMEM((TMX, wcols), BF16), pltpu.VMEM((TMX, wcols), BF16),
            pltpu.VMEM((TMX, wcols), BF16), pltpu.VMEM((TMX, wcols), BF16),
            pltpu.VMEM((HG_NH, TMX, HG_HEAD_DIM), F32),
        ],
        compiler_params=pltpu.CompilerParams(
            dimension_semantics=("arbitrary", "arbitrary", "arbitrary"), vmem_limit_bytes=VMEM_LIMIT_BYTES),
        name=f"hgrn_scan_{'bwd' if reverse else 'fwd'}_l{layer}",
    )(p, p, p, lb, om, _hgrn_cumsum_matrix(reverse))


def _hgrn_readout_kernel(of_ref, ob_ref, gate_ref, gn_ref, w_ref, x_ref, mod_ref, o_ref, *, n_lat_tiles, ctx_out):
    t = pl.program_id(0)

    def update():
        o = of_ref[...] + ob_ref[...]
        gate = gate_ref[...]
        gs = gate * jax.nn.sigmoid(gate)
        parts = []
        for h in range(HG_HEADS):
            cols = slice(h * HG_HEAD_DIM, (h + 1) * HG_HEAD_DIM)
            parts.append((_rms(o[:, cols]) * gn_ref[:, cols] * gs[:, cols]).astype(BF16))
        y = jnp.dot(jnp.concatenate(parts, axis=1), w_ref[...], preferred_element_type=F32)
        o_ref[...] = x_ref[...] + mod_ref[5:6, :] * y

    if ctx_out:
        update()
    else:
        pl.when(t < n_lat_tiles)(update)

        @pl.when(t >= n_lat_tiles)
        def _():
            o_ref[...] = x_ref[...]


def hgrn_readout(o_f, o_b, p, g_norm, w_out, hs, mods, *, layer, batch, ctx_out):
    d = hs.shape[1]
    kern = functools.partial(_hgrn_readout_kernel, n_lat_tiles=batch * LAT_TILES, ctx_out=ctx_out)
    return pl.pallas_call(
        kern,
        grid=(_stream_tiles(batch),),
        in_specs=[
            pl.BlockSpec((TMX, d), lambda t: (t, 0)),
            pl.BlockSpec((TMX, d), lambda t: (t, 0)),
            pl.BlockSpec((TMX, d), lambda t: (t, 4)),
            _resident((1, d)),
            _resident(w_out.shape),
            pl.BlockSpec((TMX, d), lambda t: (t, 0)),
            pl.BlockSpec((None, None, N_MOD, d), lambda t: (layer, _tile_mod_row(t, batch), 0, 0)),
        ],
        out_specs=pl.BlockSpec((TMX, d), lambda t: (t, 0)),
        out_shape=jax.ShapeDtypeStruct(hs.shape, F32),
        compiler_params=pltpu.CompilerParams(
            dimension_semantics=("arbitrary",), vmem_limit_bytes=VMEM_LIMIT_BYTES),
        name=f"hgrn_out_l{layer}",
    )(o_f, o_b, p, g_norm, w_out, hs, mods)


def hgrn_layer(hs, mods, norm_g3, w_in, lb_fwd, lb_bwd, g_norm, w_out, *, layer, j, batch, with_ctx_out):
    p = hgrn_project(hs, mods, norm_g3, w_in, layer=layer, j=j)
    outs = []
    for reverse, lb, z_col in ((False, lb_fwd, 1), (True, lb_bwd, 2)):
        lb = lb.reshape(1, -1)
        outs.append(hgrn_scan(p, lb, 1.0 - lb, batch=batch, reverse=reverse, z_col=z_col, layer=layer))
    return hgrn_readout(outs[0], outs[1], p, g_norm.reshape(1, -1), w_out, hs, mods,
                        layer=layer, batch=batch, ctx_out=with_ctx_out)


FN_GC = D_MODEL // FOURIER_GROUPS
FN_TM = 512
FN_TK = 2048


def _fnet_tables(n_pos, tm):
    t = np.arange(n_pos, dtype=np.int64)
    ang = lambda k: 2.0 * np.pi * ((k[:, None] * t[None, :]) % n_pos) / n_pos
    phi = ang(np.arange(tm, dtype=np.int64))
    th = ang(np.arange(0, n_pos, tm, dtype=np.int64))
    sc = n_pos ** -0.5
    rows = np.stack([np.stack([np.cos(th), -np.sin(th)], axis=1),
                     np.stack([-np.sin(th), -np.cos(th)], axis=1)], axis=1) * sc
    rows = rows.reshape(-1, 2, n_pos)
    return (jnp.asarray(rows, F32), jnp.asarray(np.cos(phi), F32), jnp.asarray(np.sin(phi), F32))


def _fnet_channel_table():
    c = np.arange(FN_GC, dtype=np.int64)
    ang = 2.0 * np.pi * ((c[:, None] * c[None, :]) % FN_GC) / FN_GC
    return jnp.asarray(np.concatenate([np.cos(ang), np.sin(ang)], axis=1) * FN_GC ** -0.5, F32).astype(BF16)


def _fnet_chan_kernel(x_ref, mod_ref, g_ref, cs_ref, o_ref):
    n = _adanorm(x_ref[...], g_ref[...], mod_ref[3:4, :], mod_ref[4:5, :]).astype(BF16)
    for g in range(FOURIER_GROUPS):
        cols = slice(g * FN_GC, (g + 1) * FN_GC)
        pq = jnp.dot(n[:, cols], cs_ref[...], preferred_element_type=F32)
        o_ref[0, :, cols] = pq[:, :FN_GC].astype(BF16)
        o_ref[1, :, cols] = pq[:, FN_GC:].astype(BF16)


def fnet_channel_dft(hs, mods, norm_g3, cs, *, layer, batch):
    d = hs.shape[1]
    return pl.pallas_call(
        _fnet_chan_kernel,
        grid=(_stream_tiles(batch),),
        in_specs=[
            pl.BlockSpec((TMX, d), lambda t: (t, 0)),
            pl.BlockSpec((None, None, N_MOD, d), lambda t: (layer, _tile_mod_row(t, batch), 0, 0)),
            pl.BlockSpec((None, 1, d), lambda t: (layer * N_SUB + 1, 0, 0)),
            _resident(cs.shape),
        ],
        out_specs=pl.BlockSpec((None, 2, TMX, d), lambda t: (_tile_batch(t, batch), 0, _tile_block(t, batch), 0)),
        out_shape=jax.ShapeDtypeStruct((batch, 2, S_ALL, d), BF16),
        compiler_params=pltpu.CompilerParams(
            dimension_semantics=("arbitrary",), vmem_limit_bytes=VMEM_LIMIT_BYTES),
        name=f"fnet_chan_l{layer}",
    )(hs, mods, norm_g3, cs)


def _fnet_pos_kernel(rt_ref, cphi_ref, sphi_ref, pq_ref, o_ref, acc_ref, *, nk, tk):
    kk = pl.program_id(2)
    cols = pl.ds(pl.multiple_of((kk % (nk // 2)) * tk, tk), tk)
    tile = (rt_ref[0:1, :] * cphi_ref[:, cols] + rt_ref[1:2, :] * sphi_ref[:, cols]).astype(BF16)

    @pl.when(kk == 0)
    def _():
        acc_ref[...] = jnp.zeros_like(acc_ref)

    acc_ref[...] += jnp.dot(tile, pq_ref[...], preferred_element_type=F32)

    @pl.when(kk == nk - 1)
    def _():
        o_ref[...] = acc_ref[...].astype(BF16)


def fnet_position_dft(pq, tables, *, n_pos, tm, tk, row_blk0, name):
    batch, _, _, d = pq.shape
    rows, cphi, sphi = tables
    n_m, nkh = n_pos // tm, n_pos // tk
    nk = 2 * nkh
    kern = functools.partial(_fnet_pos_kernel, nk=nk, tk=tk)
    return pl.pallas_call(
        kern,
        grid=(batch, n_m, nk),
        in_specs=[
            pl.BlockSpec((None, 2, tk), lambda b, m, kk: (m * 2 + kk // nkh, 0, kk % nkh)),
            _resident(cphi.shape),
            _resident(sphi.shape),
            pl.BlockSpec((None, None, tk, d), lambda b, m, kk: (b, kk // nkh, row_blk0 + kk % nkh, 0)),
        ],
        out_specs=pl.BlockSpec((None, tm, d), lambda b, m, kk: (b, m, 0)),
        out_shape=jax.ShapeDtypeStruct((batch, n_pos, d), BF16),
        scratch_shapes=[pltpu.VMEM((tm, d), F32)],
        compiler_params=pltpu.CompilerParams(
            dimension_semantics=("arbitrary", "arbitrary", "arbitrary"), vmem_limit_bytes=VMEM_LIMIT_BYTES),
        name=name,
    )(rows, cphi, sphi, pq)


def fnet_layer(hs, mods, norm_g3, w_out, *, layer, batch, with_ctx_out):
    pq = fnet_channel_dft(hs, mods, norm_g3, _fnet_channel_table(), layer=layer, batch=batch)
    y = fnet_position_dft(pq, _fnet_tables(SEQ, FN_TM), n_pos=SEQ, tm=FN_TM, tk=FN_TK, row_blk0=0,
                          name=f"fnet_pos_lat_l{layer}")
    y_ctx = None
    if with_ctx_out:
        y_ctx = fnet_position_dft(pq, _fnet_tables(CTX_LEN, CTX_LEN), n_pos=CTX_LEN, tm=CTX_LEN, tk=CTX_LEN,
                                  row_blk0=SEQ // CTX_LEN, name=f"fnet_pos_ctx_l{layer}")
    return mixer_out_proj(y, y_ctx, w_out, hs, mods, layer=layer, batch=batch, name=f"fnet_out_l{layer}")


def kernel(x, c, ctx, c_ctx, mod_w, mod_b, norm_g, ffn1_w_gu, ffn1_w_down, ffn2_w_gu, ffn2_w_down,
           hgrn_w_in, hgrn_lb_logits, hgrn_g_norm, hgrn_w_out,
           mla_w_dqkv, mla_q_norm, mla_kv_norm, mla_w_uq, mla_w_ukv, mla_w_o,
           fnet_w_out, final_g):
    B, T, D = x.shape
    n_lat = B * T
    n_ctx = B * CTX_LEN
    tm, tf = 512, 512
    rows_all = n_lat + n_ctx

    lb = jnp.cumsum(jax.nn.softmax(hgrn_lb_logits.astype(jnp.float32), axis=1), axis=1)
    lb = lb - lb[:, :1]
    rope_cos, rope_sin = mla_rope_tables(T)

    cc = jnp.concatenate([c, c_ctx[None, :], jnp.zeros((MOD_ROWS - B - 1, D), F32)], axis=0)
    mods = modulation(cc, mod_w, mod_b).reshape(DEPTH, MOD_ROWS, N_MOD, D)

    w1_gu, w1_down = ffn1_w_gu.astype(BF16), ffn1_w_down.astype(BF16)
    w2_gu, w2_down = ffn2_w_gu.astype(BF16), ffn2_w_down.astype(BF16)
    hg_w_in = hgrn_w_in.astype(BF16)
    norm_g3 = norm_g.reshape(DEPTH * N_SUB, 1, D)
    final_g2 = final_g.reshape(1, D)

    hs = jnp.concatenate([x.reshape(n_lat, D), ctx.reshape(n_ctx, D)], axis=0)
    for i in range(DEPTH):
        kind, j = i % N_MIXERS, i // N_MIXERS
        last = i == DEPTH - 1
        ctx_in = not (last and kind == 2)
        hs = ffn_sublayer(hs, mods, norm_g3, w1_gu, w1_down, final_g2, layer=i, s=0,
                          rows=rows_all if ctx_in else n_lat, tm=tm, tf=tf)
        if kind == 0:
            hs = hgrn_layer(hs, mods, norm_g3, hg_w_in, lb[0, j], lb[1, j], hgrn_g_norm[j],
                            hgrn_w_out[j].astype(BF16), layer=i, j=j, batch=B, with_ctx_out=not last)
        elif kind == 1:
            hs = mla_layer(hs, mods, norm_g3, mla_weights(mla_w_dqkv[j], mla_w_uq[j], mla_w_ukv[j]),
                           mla_q_norm[j].reshape(1, -1), mla_kv_norm[j].reshape(1, -1),
                           mla_w_o[j].astype(BF16), rope_cos, rope_sin, layer=i, batch=B, with_ctx_out=not last)
        else:
            hs = fnet_layer(hs, mods, norm_g3, fnet_w_out[j].astype(BF16), layer=i, batch=B, with_ctx_out=not last)
        hs = ffn_sublayer(hs, mods, norm_g3, w2_gu, w2_down, final_g2, layer=i, s=2,
                          rows=n_lat if last else rows_all, tm=tm, tf=tf, final=last)
    return hs.reshape(B, T, D)
```

```python
import functools

import numpy as np
import jax
import jax.numpy as jnp
from jax import lax
from jax.experimental import pallas as pl
from jax.experimental.pallas import tpu as pltpu

D_MODEL = 2048
BATCH = 2
SEQ = 4096
DEPTH = 4
GRID_W = 64
CTX_LEN = 256
N_MIXERS = 3
N_SUB = 3
D_FF = 5632
RMS_EPS = 1e-6

HG_HEAD_DIM = 128
HG_HEADS = D_MODEL // HG_HEAD_DIM
HG_CHUNK = 16

MLA_HEADS = 16
MLA_Q_RANK = 512
MLA_KV_RANK = 512
MLA_NOPE = 128
MLA_ROPE = 64
MLA_V = 128
MLA_SCALE = (MLA_NOPE + MLA_ROPE) ** -0.5
ROPE_THETA = 10000.0
ROPE_FREQS = MLA_ROPE // 4

FOURIER_GROUPS = 8

BF16 = jnp.bfloat16
F32 = jnp.float32

VMEM_LIMIT_BYTES = 56 * 1024 * 1024
LANES = 128
MOD_ROWS = 8
N_MOD = N_SUB * 3
MIX_SHIFT, MIX_SCALE, MIX_GATE = 3, 4, 5


def _mod_row(t, tm):
    return jnp.minimum(t // (SEQ // tm), BATCH)


def _mod_kernel(c_ref, w_ref, b_ref, o_ref):
    c = c_ref[...]
    a = (c * jax.nn.sigmoid(c)).astype(BF16)
    o_ref[...] = jnp.dot(a, w_ref[...].astype(BF16), preferred_element_type=F32) + b_ref[...]


def modulation(cc, mod_w, mod_b, *, tn=1024):
    depth, d, n = mod_w.shape
    return pl.pallas_call(
        _mod_kernel,
        grid=(depth, n // tn),
        in_specs=[
            pl.BlockSpec((MOD_ROWS, d), lambda i, j: (0, 0)),
            pl.BlockSpec((None, d, tn), lambda i, j: (i, 0, j)),
            pl.BlockSpec((None, 1, tn), lambda i, j: (i, 0, j)),
        ],
        out_specs=pl.BlockSpec((None, MOD_ROWS, tn), lambda i, j: (i, 0, j)),
        out_shape=jax.ShapeDtypeStruct((depth, MOD_ROWS, n), F32),
        compiler_params=pltpu.CompilerParams(
            dimension_semantics=("arbitrary", "arbitrary"), vmem_limit_bytes=VMEM_LIMIT_BYTES),
        name="modulation",
    )(cc, mod_w, mod_b.reshape(depth, 1, n))


def _adanorm(x, g, shift, scale):
    y = x * lax.rsqrt(jnp.mean(x * x, axis=-1, keepdims=True) + RMS_EPS)
    return (y * g) * (1.0 + scale) + shift


NORM_ROWS = 16


def _ffn_kernel(x_ref, mod_ref, g_ref, wg_ref, wu_ref, wd_ref, fg_ref, o_ref, n_ref, *, s, nf, final):
    f = pl.program_id(1)

    @pl.when(f == 0)
    def _():
        g, shift, scale = g_ref[...], mod_ref[3 * s:3 * s + 1, :], mod_ref[3 * s + 1:3 * s + 2, :]
        for i in range(x_ref.shape[0] // NORM_ROWS):
            r = slice(i * NORM_ROWS, (i + 1) * NORM_ROWS)
            n_ref[r, :] = _adanorm(x_ref[r, :], g, shift, scale).astype(BF16)
        o_ref[...] = jnp.zeros_like(o_ref)

    n = n_ref[...]
    gate = jnp.dot(n, wg_ref[...], preferred_element_type=F32)
    up = jnp.dot(n, wu_ref[...], preferred_element_type=F32)
    act = (gate * jax.nn.sigmoid(gate) * up).astype(BF16)
    o_ref[...] += jnp.dot(act, wd_ref[...], preferred_element_type=F32)

    @pl.when(f == nf - 1)
    def _():
        half_gate = 0.5 * mod_ref[3 * s + 2:3 * s + 3, :]
        for i in range(x_ref.shape[0] // NORM_ROWS):
            r = slice(i * NORM_ROWS, (i + 1) * NORM_ROWS)
            h = x_ref[r, :] + half_gate * o_ref[r, :]
            if final:
                h = _rms(h) * fg_ref[...]
            o_ref[r, :] = h


def ffn_sublayer(h, mods, g, w_gu, w_down, final_g, *, layer, s, rows, tm, tf, final=False):
    d = h.shape[1]
    nf = D_FF // tf
    kern = functools.partial(_ffn_kernel, s=s, nf=nf, final=final)
    return pl.pallas_call(
        kern,
        grid=(rows // tm, nf),
        in_specs=[
            pl.BlockSpec((tm, d), lambda t, f: (t, 0)),
            pl.BlockSpec((None, None, N_MOD, d), lambda t, f: (layer, _mod_row(t, tm), 0, 0)),
            pl.BlockSpec((None, 1, d), lambda t, f: (layer * N_SUB + s, 0, 0)),
            pl.BlockSpec((None, d, tf), lambda t, f: (layer, 0, f)),
            pl.BlockSpec((None, d, tf), lambda t, f: (layer, 0, f + nf)),
            pl.BlockSpec((None, tf, d), lambda t, f: (layer, f, 0)),
            pl.BlockSpec((1, d), lambda t, f: (0, 0)),
        ],
        out_specs=pl.BlockSpec((tm, d), lambda t, f: (t, 0)),
        out_shape=jax.ShapeDtypeStruct((rows if final else h.shape[0], d), F32),
        scratch_shapes=[pltpu.VMEM((tm, d), BF16)],
        compiler_params=pltpu.CompilerParams(
            dimension_semantics=("arbitrary", "arbitrary"), vmem_limit_bytes=VMEM_LIMIT_BYTES),
        name=f"ffn_l{layer}_s{s}",
    )(h, mods, g, w_gu, w_gu, w_down, final_g)


TMX = 256
S_ALL = SEQ + CTX_LEN
LAT_TILES = SEQ // TMX
CTX_BLOCK = SEQ // TMX


def _resident(shape):
    return pl.BlockSpec(shape, lambda *_: (0,) * len(shape), pipeline_mode=pl.Buffered(1))


def _stream_tiles(batch):
    return batch * (LAT_TILES + CTX_LEN // TMX)


def _tile_batch(t, batch):
    lat = t < batch * LAT_TILES
    return jnp.where(lat, t // LAT_TILES, t - batch * LAT_TILES)


def _tile_block(t, batch):
    return jnp.where(t < batch * LAT_TILES, t % LAT_TILES, CTX_BLOCK)


def _tile_mod_row(t, batch):
    return jnp.where(t < batch * LAT_TILES, t // LAT_TILES, batch)


def _rms(x):
    return x * lax.rsqrt(jnp.mean(x * x, axis=-1, keepdims=True) + RMS_EPS)


def _mixer_input(x_ref, g_ref, mod_ref):
    return _adanorm(x_ref[...], g_ref[...], mod_ref[MIX_SHIFT:MIX_SHIFT + 1, :], mod_ref[MIX_SCALE:MIX_SCALE + 1, :])


MLA_QK = 2 * MLA_NOPE
LOG2_E = 1.4426950408889634
N_DQ = MLA_Q_RANK + MLA_KV_RANK


def _mla_proj_kernel(x_ref, mod_ref, g_ref, wd_ref, qn_ref, kvn_ref, wuq_ref, wukv_ref, cos_ref, sin_ref,
                     q_ref, k_ref, v_ref):
    n = _mixer_input(x_ref, g_ref, mod_ref).astype(BF16)
    proj = jnp.dot(n, wd_ref[...], preferred_element_type=F32)
    cq = (_rms(proj[:, :MLA_Q_RANK]) * qn_ref[...]).astype(BF16)
    ckv = (_rms(proj[:, MLA_Q_RANK:N_DQ]) * kvn_ref[...]).astype(BF16)
    cos, sin = cos_ref[...], sin_ref[...]
    kr = proj[:, N_DQ:N_DQ + LANES] * cos + proj[:, N_DQ + LANES:N_DQ + 2 * LANES] * sin
    kr = kr.astype(BF16)
    q = jnp.dot(cq, wuq_ref[...], preferred_element_type=F32) * (MLA_SCALE * LOG2_E)
    kv = jnp.dot(ckv, wukv_ref[...], preferred_element_type=F32)
    hn = MLA_HEADS * MLA_NOPE
    lane = lax.broadcasted_iota(jnp.int32, (x_ref.shape[0], LANES), 1)
    ones_col = jnp.where(lane == 0, 1.0, 0.0).astype(BF16)
    for h in range(MLA_HEADS):
        lo = h * LANES
        q_ref[h, :, 0:LANES] = q[:, lo:lo + LANES].astype(BF16)
        qr = q[:, hn + lo:hn + lo + LANES] * cos + q[:, 2 * hn + lo:2 * hn + lo + LANES] * sin
        q_ref[h, :, LANES:2 * LANES] = qr.astype(BF16)
        k_ref[h, :, 0:LANES] = kv[:, lo:lo + LANES].astype(BF16)
        k_ref[h, :, LANES:2 * LANES] = kr
        v_ref[h, :, 0:LANES] = kv[:, hn + lo:hn + lo + LANES].astype(BF16)
        v_ref[h, :, LANES:2 * LANES] = ones_col


def _rope_partner(width):
    idx = jnp.arange(width)
    return idx ^ ROPE_FREQS


def mla_weights(w_dqkv, w_uq, w_ukv):
    d = w_dqkv.shape[0]
    z = jnp.zeros((d, LANES - MLA_ROPE), w_dqkv.dtype)
    kr = w_dqkv[:, N_DQ:]
    wd = jnp.concatenate([w_dqkv[:, :N_DQ], kr, z, kr[:, _rope_partner(MLA_ROPE)], z], axis=1)
    wq = w_uq.reshape(MLA_Q_RANK, MLA_HEADS, MLA_NOPE + MLA_ROPE)
    qr = wq[:, :, MLA_NOPE:]
    zq = jnp.zeros((MLA_Q_RANK, MLA_HEADS, LANES - MLA_ROPE), w_uq.dtype)
    wuq = jnp.concatenate([
        wq[:, :, :MLA_NOPE].reshape(MLA_Q_RANK, -1),
        jnp.concatenate([qr, zq], axis=2).reshape(MLA_Q_RANK, -1),
        jnp.concatenate([qr[:, :, _rope_partner(MLA_ROPE)], zq], axis=2).reshape(MLA_Q_RANK, -1)], axis=1)
    wkv = w_ukv.reshape(MLA_KV_RANK, MLA_HEADS, MLA_NOPE + MLA_V)
    wukv = jnp.concatenate([wkv[:, :, :MLA_NOPE].reshape(MLA_KV_RANK, -1),
                            wkv[:, :, MLA_NOPE:].reshape(MLA_KV_RANK, -1)], axis=1)
    return wd.astype(BF16), wuq.astype(BF16), wukv.astype(BF16)


def mla_rope_tables(n_tokens):
    rows = n_tokens // GRID_W
    r = jnp.broadcast_to(jnp.arange(rows, dtype=F32)[:, None], (rows, GRID_W)).reshape(-1)
    col = jnp.broadcast_to(jnp.arange(GRID_W, dtype=F32)[None, :], (rows, GRID_W)).reshape(-1)
    inv_freq = ROPE_THETA ** (-jnp.arange(ROPE_FREQS, dtype=F32) / ROPE_FREQS)
    ang = jnp.stack([r, col], axis=-1)[..., None] * inv_freq
    cos = jnp.broadcast_to(jnp.cos(ang)[:, :, None, :], (n_tokens, 2, 2, ROPE_FREQS)).reshape(n_tokens, MLA_ROPE)
    sin = jnp.sin(ang)
    sin = jnp.stack([-sin, sin], axis=2).reshape(n_tokens, MLA_ROPE)
    pad = jnp.zeros((n_tokens, LANES - MLA_ROPE), F32)
    cos = jnp.concatenate([cos, pad], axis=1)
    sin = jnp.concatenate([sin, pad], axis=1)
    ctx_cos = jnp.concatenate([jnp.ones((CTX_LEN, MLA_ROPE), F32), jnp.zeros((CTX_LEN, LANES - MLA_ROPE), F32)], axis=1)
    return (jnp.concatenate([cos, ctx_cos], axis=0),
            jnp.concatenate([sin, jnp.zeros((CTX_LEN, LANES), F32)], axis=0))


def mla_project(hs, mods, norm_g3, wd, q_norm, kv_norm, wuq, wukv, cos, sin, *, layer, batch):
    d = hs.shape[1]
    bmap = lambda t: (_tile_batch(t, batch), 0, _tile_block(t, batch), 0)
    return pl.pallas_call(
        _mla_proj_kernel,
        grid=(_stream_tiles(batch),),
        in_specs=[
            pl.BlockSpec((TMX, d), lambda t: (t, 0)),
            pl.BlockSpec((None, None, N_MOD, d), lambda t: (layer, _tile_mod_row(t, batch), 0, 0)),
            pl.BlockSpec((None, 1, d), lambda t: (layer * N_SUB + 1, 0, 0)),
            _resident(wd.shape),
            _resident((1, MLA_Q_RANK)),
            _resident((1, MLA_KV_RANK)),
            _resident(wuq.shape),
            _resident(wukv.shape),
            pl.BlockSpec((TMX, LANES), lambda t: (_tile_block(t, batch), 0)),
            pl.BlockSpec((TMX, LANES), lambda t: (_tile_block(t, batch), 0)),
        ],
        out_specs=[
            pl.BlockSpec((None, MLA_HEADS, TMX, MLA_QK), bmap),
            pl.BlockSpec((None, MLA_HEADS, TMX, MLA_QK), bmap),
            pl.BlockSpec((None, MLA_HEADS, TMX, MLA_QK), bmap),
        ],
        out_shape=[
            jax.ShapeDtypeStruct((batch, MLA_HEADS, S_ALL, MLA_QK), BF16),
            jax.ShapeDtypeStruct((batch, MLA_HEADS, S_ALL, MLA_QK), BF16),
            jax.ShapeDtypeStruct((batch, MLA_HEADS, S_ALL, MLA_QK), BF16),
        ],
        compiler_params=pltpu.CompilerParams(
            dimension_semantics=("arbitrary",), vmem_limit_bytes=VMEM_LIMIT_BYTES),
        name=f"mla_proj_l{layer}",
    )(hs, mods, norm_g3, wd, q_norm, kv_norm, wuq, wukv, cos, sin)


def _attn_kernel(q_ref, k_ref, v_ref, o_ref, *, n_chain):
    rows = q_ref.shape[0] // n_chain
    for c in range(n_chain):
        r = slice(c * rows, (c + 1) * rows)
        s = lax.dot_general(q_ref[r, :], k_ref[...], (((1,), (1,)), ((), ())), preferred_element_type=F32)
        p = jnp.exp2(s - jnp.max(s, axis=-1, keepdims=True))
        o = jnp.dot(p.astype(BF16), v_ref[...], preferred_element_type=F32)
        o_ref[r, :] = (o[:, :MLA_V] / o[:, MLA_V:MLA_V + 1]).astype(BF16)


def mla_attention(q, k, v, *, tq, n_chain, q_block0, n_q, k_rows, k_block, name):
    batch, heads, _, _ = q.shape
    return pl.pallas_call(
        functools.partial(_attn_kernel, n_chain=n_chain),
        grid=(batch, heads, n_q),
        in_specs=[
            pl.BlockSpec((None, None, tq, MLA_QK), lambda b, h, i: (b, h, q_block0 + i, 0)),
            pl.BlockSpec((None, None, k_rows, MLA_QK), lambda b, h, i: (b, h, k_block, 0)),
            pl.BlockSpec((None, None, k_rows, MLA_QK), lambda b, h, i: (b, h, k_block, 0)),
        ],
        out_specs=pl.BlockSpec((None, tq, MLA_V), lambda b, h, i: (b, i, h)),
        out_shape=jax.ShapeDtypeStruct((batch, n_q * tq, heads * MLA_V), BF16),
        compiler_params=pltpu.CompilerParams(
            dimension_semantics=("arbitrary", "arbitrary", "arbitrary"), vmem_limit_bytes=VMEM_LIMIT_BYTES),
        name=name,
    )(q, k, v)


def _out_proj_kernel(yl_ref, yc_ref, w_ref, x_ref, mod_ref, o_ref, *, n_lat_tiles, ctx_out):
    t = pl.program_id(0)

    def update(y_ref):
        y = jnp.dot(y_ref[...], w_ref[...], preferred_element_type=F32)
        o_ref[...] = x_ref[...] + mod_ref[MIX_GATE:MIX_GATE + 1, :] * y

    @pl.when(t < n_lat_tiles)
    def _():
        update(yl_ref)

    @pl.when(t >= n_lat_tiles)
    def _():
        if ctx_out:
            update(yc_ref)
        else:
            o_ref[...] = x_ref[...]


def mixer_out_proj(y_lat, y_ctx, w, hs, mods, *, layer, batch, name):
    d = hs.shape[1]
    kdim = y_lat.shape[2]
    nl = batch * LAT_TILES
    ctx_out = y_ctx is not None
    kern = functools.partial(_out_proj_kernel, n_lat_tiles=nl, ctx_out=ctx_out)

    def lat_map(t):
        tl = jnp.minimum(t, nl - 1)
        return (tl // LAT_TILES, tl % LAT_TILES, 0)

    return pl.pallas_call(
        kern,
        grid=(_stream_tiles(batch),),
        in_specs=[
            pl.BlockSpec((None, TMX, kdim), lat_map),
            pl.BlockSpec((None, TMX, kdim), lambda t: (jnp.maximum(t - nl, 0), 0, 0)),
            _resident(w.shape),
            pl.BlockSpec((TMX, d), lambda t: (t, 0)),
            pl.BlockSpec((None, None, N_MOD, d), lambda t: (layer, _tile_mod_row(t, batch), 0, 0)),
        ],
        out_specs=pl.BlockSpec((TMX, d), lambda t: (t, 0)),
        out_shape=jax.ShapeDtypeStruct(hs.shape, F32),
        compiler_params=pltpu.CompilerParams(
            dimension_semantics=("arbitrary",), vmem_limit_bytes=VMEM_LIMIT_BYTES),
        name=name,
    )(y_lat, y_ctx if ctx_out else y_lat[:, :CTX_LEN], w, hs, mods)


def mla_layer(hs, mods, norm_g3, weights, q_norm, kv_norm, w_o, cos, sin, *, layer, batch, with_ctx_out):
    wd, wuq, wukv = weights
    q, k, v = mla_project(hs, mods, norm_g3, wd, q_norm, kv_norm, wuq, wukv, cos, sin, layer=layer, batch=batch)
    tq = 4096
    att = mla_attention(q, k, v, tq=tq, n_chain=16, q_block0=0, n_q=SEQ // tq, k_rows=S_ALL, k_block=0,
                        name=f"mla_attn_lat_l{layer}")
    att_ctx = None
    if with_ctx_out:
        att_ctx = mla_attention(q, k, v, tq=CTX_LEN, n_chain=1, q_block0=SEQ // CTX_LEN, n_q=1, k_rows=CTX_LEN,
                                k_block=SEQ // CTX_LEN, name=f"mla_attn_ctx_l{layer}")
    return mixer_out_proj(att, att_ctx, w_o, hs, mods, layer=layer, batch=batch, name=f"mla_out_l{layer}")


HG_WIN = 128
HG_NH = 8
N_TBLK = S_ALL // TMX
HG_PROJ_TM = 512


def _hgrn_proj_kernel(x_ref, mod_ref, g_ref, w_ref, o_ref):
    n = _mixer_input(x_ref, g_ref, mod_ref).astype(BF16)
    o_ref[...] = jnp.dot(n, w_ref[...], preferred_element_type=F32)


def hgrn_project(hs, mods, norm_g3, w_in, *, layer, j):
    rows, d = hs.shape
    n_out = w_in.shape[2]
    tm = HG_PROJ_TM
    return pl.pallas_call(
        _hgrn_proj_kernel,
        grid=(n_out // d, rows // tm),
        in_specs=[
            pl.BlockSpec((tm, d), lambda c, t: (t, 0)),
            pl.BlockSpec((None, None, N_MOD, d), lambda c, t: (layer, _mod_row(t, tm), 0, 0)),
            pl.BlockSpec((None, 1, d), lambda c, t: (layer * N_SUB + 1, 0, 0)),
            pl.BlockSpec((None, d, d), lambda c, t: (j, 0, c)),
        ],
        out_specs=pl.BlockSpec((tm, d), lambda c, t: (t, c)),
        out_shape=jax.ShapeDtypeStruct((rows, n_out), F32),
        compiler_params=pltpu.CompilerParams(
            dimension_semantics=("arbitrary", "arbitrary"), vmem_limit_bytes=VMEM_LIMIT_BYTES),
        name=f"hgrn_proj_l{layer}",
    )(hs, mods, norm_g3, w_in)


def _split3(x):
    hi = x.astype(BF16)
    r1 = x - hi.astype(F32)
    mid = r1.astype(BF16)
    lo = (r1 - mid.astype(F32)).astype(BF16)
    return hi, mid, lo


def _hgrn_scan_kernel(q_ref, z_ref, v_ref, lb_ref, om_ref, tri_ref, o_ref,
                      st_ref, qd_ref, kin_ref, kte_ref, vb_ref, tot_ref, *, reverse):
    ts = pl.program_id(2)

    @pl.when(ts == 0)
    def _():
        st_ref[...] = jnp.zeros_like(st_ref)

    z = z_ref[...]
    e = jnp.exp(-jnp.abs(z))
    r = 1.0 / (1.0 + e)
    er = e * r
    pos = z >= 0
    om = om_ref[...]
    f = lb_ref[...] + om * jnp.where(pos, r, er)
    log_f = jnp.log(f)
    k = om * jnp.where(pos, er, r)
    tri = tri_ref[...]
    cum = sum(jnp.dot(tri, piece, preferred_element_type=F32) for piece in _split3(log_f))
    b_inc = cum[:TMX]
    b_rest = cum[TMX:]
    qr = q_ref[...]
    qd_ref[...] = (qr * jax.nn.sigmoid(qr) * jnp.exp(b_inc)).astype(BF16)
    kin_ref[...] = (k * jnp.exp(-b_inc)).astype(BF16)
    kte_ref[...] = (k * jnp.exp(b_rest)).astype(BF16)
    vb_ref[...] = v_ref[...].astype(BF16)
    tot = b_inc + b_rest
    dec_all = []
    for h in range(HG_NH):
        tot_ref[h] = tot[:, h * HG_HEAD_DIM:(h + 1) * HG_HEAD_DIM]
        dec_all.append(jnp.exp(tot_ref[h, pl.ds(0, TMX // HG_CHUNK, stride=HG_CHUNK), :]))

    wi = lax.broadcasted_iota(jnp.int32, (HG_WIN, HG_WIN), 0)
    wj = lax.broadcasted_iota(jnp.int32, (HG_WIN, HG_WIN), 1)
    same_chunk = (wi // HG_CHUNK) == (wj // HG_CHUNK)
    causal = same_chunk & ((wj >= wi) if reverse else (wj <= wi))

    n_win = TMX // HG_WIN
    per_win = HG_WIN // HG_CHUNK
    wins = list(range(n_win - 1, -1, -1) if reverse else range(n_win))
    chunks = list(range(per_win - 1, -1, -1) if reverse else range(per_win))
    def only_chunk(x, c):
        parts = []
        if c > 0:
            parts.append(jnp.zeros((c * HG_CHUNK, x.shape[1]), x.dtype))
        parts.append(x[c * HG_CHUNK:(c + 1) * HG_CHUNK, :])
        if c < per_win - 1:
            parts.append(jnp.zeros(((per_win - 1 - c) * HG_CHUNK, x.shape[1]), x.dtype))
        return jnp.concatenate(parts, axis=0)

    states = [st_ref[h] for h in range(HG_NH)]
    heads = range(HG_NH)
    hcols = [slice(h * HG_HEAD_DIM, (h + 1) * HG_HEAD_DIM) for h in heads]
    for w in wins:
        rows = slice(w * HG_WIN, (w + 1) * HG_WIN)
        o_intra, upd = [], []
        for h in heads:
            qd_w, kin_w, kte_w, v_w = (r[rows, hcols[h]] for r in (qd_ref, kin_ref, kte_ref, vb_ref))
            a = lax.dot_general(qd_w, kin_w, (((1,), (1,)), ((), ())), preferred_element_type=F32)
            a = jnp.where(causal, a, 0.0).astype(BF16)
            o_intra.append(jnp.dot(a, v_w, preferred_element_type=F32))
            v_t = v_ref[rows, hcols[h]].T.astype(BF16)
            rhs = jnp.concatenate([only_chunk(kte_w, c) for c in range(per_win)], axis=1)
            upd.append(jnp.dot(v_t, rhs, preferred_element_type=F32))
        entry = [dict() for _ in heads]
        for h in heads:
            st = states[h]
            for c in chunks:
                entry[h][c] = st.astype(BF16)
                ci = w * per_win + c
                st = st * dec_all[h][ci:ci + 1, :] + upd[h][:, c * HG_HEAD_DIM:(c + 1) * HG_HEAD_DIM]
            states[h] = st
        for h in heads:
            qd_w = qd_ref[rows, hcols[h]]
            lhs = jnp.concatenate([only_chunk(qd_w, c) for c in range(per_win)], axis=1)
            ent = jnp.concatenate([entry[h][c] for c in range(per_win)], axis=1)
            o_inter = lax.dot_general(lhs, ent, (((1,), (1,)), ((), ())), preferred_element_type=F32)
            o_ref[rows, hcols[h]] = o_intra[h] + o_inter
    for h in heads:
        st_ref[h] = states[h]


def _hgrn_cumsum_matrix(reverse):
    r = np.arange(TMX)[:, None]
    c = np.arange(TMX)[None, :]
    same = (r // HG_CHUNK) == (c // HG_CHUNK)
    inc, rest = ((c >= r), (c < r)) if reverse else ((c <= r), (c > r))
    return jnp.asarray(np.concatenate([same & inc, same & rest], axis=0), BF16)


def hgrn_scan(p, lb, om, *, batch, reverse, z_col, layer):
    rows, n_out = p.shape
    d = n_out // 5
    wcols = HG_NH * HG_HEAD_DIM
    groups = d // wcols

    def rblk(b, ts):
        lat = (N_TBLK - 1 - ts) if reverse else (ts - 1)
        return jnp.where(ts == 0, batch * LAT_TILES + b, b * LAT_TILES + lat)

    col = lambda base: (lambda b, g, ts: (rblk(b, ts), base * groups + g))
    row_spec = pl.BlockSpec((1, wcols), lambda b, g, ts: (0, g))
    kern = functools.partial(_hgrn_scan_kernel, reverse=reverse)
    return pl.pallas_call(
        kern,
        grid=(batch, groups, N_TBLK),
        in_specs=[
            pl.BlockSpec((TMX, wcols), col(0)),
            pl.BlockSpec((TMX, wcols), col(z_col)),
            pl.BlockSpec((TMX, wcols), col(3)),
            row_spec, row_spec, _resident((2 * TMX, TMX)),
        ],
        out_specs=pl.BlockSpec((TMX, wcols), lambda b, g, ts: (rblk(b, ts), g)),
        out_shape=jax.ShapeDtypeStruct((rows, d), F32),
        scratch_shapes=[
            pltpu.VMEM((HG_NH, HG_HEAD_DIM, HG_HEAD_DIM), F32),
            pltpu.VMEM((TMX, wcols), BF16), pltpu.VMEM((TMX, wcols), BF16),
            pltpu.VMEM((TMX, wcols), BF16), pltpu.VMEM((TMX, wcols), BF16),
            pltpu.VMEM((HG_NH, TMX, HG_HEAD_DIM), F32),
        ],
        compiler_params=pltpu.CompilerParams(
            dimension_semantics=("arbitrary", "arbitrary", "arbitrary"), vmem_limit_bytes=VMEM_LIMIT_BYTES),
        name=f"hgrn_scan_{'bwd' if reverse else 'fwd'}_l{layer}",
    )(p, p, p, lb, om, _hgrn_cumsum_matrix(reverse))


def _hgrn_readout_kernel(of_ref, ob_ref, gate_ref, gn_ref, w_ref, x_ref, mod_ref, o_ref, *, n_lat_tiles, ctx_out):
    t = pl.program_id(0)

    def update():
        o = of_ref[...] + ob_ref[...]
        gate = gate_ref[...]
        gs = gate * jax.nn.sigmoid(gate)
        parts = []
        for h in range(HG_HEADS):
            cols = slice(h * HG_HEAD_DIM, (h + 1) * HG_HEAD_DIM)
            parts.append((_rms(o[:, cols]) * gn_ref[:, cols] * gs[:, cols]).astype(BF16))
        y = jnp.dot(jnp.concatenate(parts, axis=1), w_ref[...], preferred_element_type=F32)
        o_ref[...] = x_ref[...] + mod_ref[MIX_GATE:MIX_GATE + 1, :] * y

    if ctx_out:
        update()
    else:
        pl.when(t < n_lat_tiles)(update)

        @pl.when(t >= n_lat_tiles)
        def _():
            o_ref[...] = x_ref[...]


def hgrn_readout(o_f, o_b, p, g_norm, w_out, hs, mods, *, layer, batch, ctx_out):
    d = hs.shape[1]
    kern = functools.partial(_hgrn_readout_kernel, n_lat_tiles=batch * LAT_TILES, ctx_out=ctx_out)
    return pl.pallas_call(
        kern,
        grid=(_stream_tiles(batch),),
        in_specs=[
            pl.BlockSpec((TMX, d), lambda t: (t, 0)),
            pl.BlockSpec((TMX, d), lambda t: (t, 0)),
            pl.BlockSpec((TMX, d), lambda t: (t, 4)),
            _resident((1, d)),
            _resident(w_out.shape),
            pl.BlockSpec((TMX, d), lambda t: (t, 0)),
            pl.BlockSpec((None, None, N_MOD, d), lambda t: (layer, _tile_mod_row(t, batch), 0, 0)),
        ],
        out_specs=pl.BlockSpec((TMX, d), lambda t: (t, 0)),
        out_shape=jax.ShapeDtypeStruct(hs.shape, F32),
        compiler_params=pltpu.CompilerParams(
            dimension_semantics=("arbitrary",), vmem_limit_bytes=VMEM_LIMIT_BYTES),
        name=f"hgrn_out_l{layer}",
    )(o_f, o_b, p, g_norm, w_out, hs, mods)


def hgrn_layer(hs, mods, norm_g3, w_in, lb_fwd, lb_bwd, g_norm, w_out, *, layer, j, batch, with_ctx_out):
    p = hgrn_project(hs, mods, norm_g3, w_in, layer=layer, j=j)
    outs = []
    for reverse, lb, z_col in ((False, lb_fwd, 1), (True, lb_bwd, 2)):
        lb = lb.reshape(1, -1)
        outs.append(hgrn_scan(p, lb, 1.0 - lb, batch=batch, reverse=reverse, z_col=z_col, layer=layer))
    return hgrn_readout(outs[0], outs[1], p, g_norm.reshape(1, -1), w_out, hs, mods,
                        layer=layer, batch=batch, ctx_out=with_ctx_out)


FN_GC = D_MODEL // FOURIER_GROUPS
FN_TM = 512
FN_TK = 2048


def _fnet_tables(n_pos, tm):
    t = np.arange(n_pos, dtype=np.int64)
    ang = lambda k: 2.0 * np.pi * ((k[:, None] * t[None, :]) % n_pos) / n_pos
    phi = ang(np.arange(tm, dtype=np.int64))
    th = ang(np.arange(0, n_pos, tm, dtype=np.int64))
    sc = n_pos ** -0.5
    rows = np.stack([np.stack([np.cos(th), -np.sin(th)], axis=1),
                     np.stack([-np.sin(th), -np.cos(th)], axis=1)], axis=1) * sc
    rows = rows.reshape(-1, 2, n_pos)
    return (jnp.asarray(rows, F32), jnp.asarray(np.cos(phi), F32), jnp.asarray(np.sin(phi), F32))


def _fnet_channel_table():
    c = np.arange(FN_GC, dtype=np.int64)
    ang = 2.0 * np.pi * ((c[:, None] * c[None, :]) % FN_GC) / FN_GC
    return jnp.asarray(np.concatenate([np.cos(ang), np.sin(ang)], axis=1) * FN_GC ** -0.5, F32).astype(BF16)


def _fnet_chan_kernel(x_ref, mod_ref, g_ref, cs_ref, o_ref):
    n = _mixer_input(x_ref, g_ref, mod_ref).astype(BF16)
    for g in range(FOURIER_GROUPS):
        cols = slice(g * FN_GC, (g + 1) * FN_GC)
        pq = jnp.dot(n[:, cols], cs_ref[...], preferred_element_type=F32)
        o_ref[0, :, cols] = pq[:, :FN_GC].astype(BF16)
        o_ref[1, :, cols] = pq[:, FN_GC:].astype(BF16)


def fnet_channel_dft(hs, mods, norm_g3, cs, *, layer, batch):
    d = hs.shape[1]
    return pl.pallas_call(
        _fnet_chan_kernel,
        grid=(_stream_tiles(batch),),
        in_specs=[
            pl.BlockSpec((TMX, d), lambda t: (t, 0)),
            pl.BlockSpec((None, None, N_MOD, d), lambda t: (layer, _tile_mod_row(t, batch), 0, 0)),
            pl.BlockSpec((None, 1, d), lambda t: (layer * N_SUB + 1, 0, 0)),
            _resident(cs.shape),
        ],
        out_specs=pl.BlockSpec((None, 2, TMX, d), lambda t: (_tile_batch(t, batch), 0, _tile_block(t, batch), 0)),
        out_shape=jax.ShapeDtypeStruct((batch, 2, S_ALL, d), BF16),
        compiler_params=pltpu.CompilerParams(
            dimension_semantics=("arbitrary",), vmem_limit_bytes=VMEM_LIMIT_BYTES),
        name=f"fnet_chan_l{layer}",
    )(hs, mods, norm_g3, cs)


def _fnet_pos_kernel(rt_ref, cphi_ref, sphi_ref, pq_ref, o_ref, acc_ref, *, nk, tk):
    kk = pl.program_id(2)
    cols = pl.ds(pl.multiple_of((kk % (nk // 2)) * tk, tk), tk)
    tile = (rt_ref[0:1, :] * cphi_ref[:, cols] + rt_ref[1:2, :] * sphi_ref[:, cols]).astype(BF16)

    @pl.when(kk == 0)
    def _():
        acc_ref[...] = jnp.zeros_like(acc_ref)

    acc_ref[...] += jnp.dot(tile, pq_ref[...], preferred_element_type=F32)

    @pl.when(kk == nk - 1)
    def _():
        o_ref[...] = acc_ref[...].astype(BF16)


def fnet_position_dft(pq, tables, *, n_pos, tm, tk, row_blk0, name):
    batch, _, _, d = pq.shape
    rows, cphi, sphi = tables
    n_m, nkh = n_pos // tm, n_pos // tk
    nk = 2 * nkh
    kern = functools.partial(_fnet_pos_kernel, nk=nk, tk=tk)
    return pl.pallas_call(
        kern,
        grid=(batch, n_m, nk),
        in_specs=[
            pl.BlockSpec((None, 2, tk), lambda b, m, kk: (m * 2 + kk // nkh, 0, kk % nkh)),
            _resident(cphi.shape),
            _resident(sphi.shape),
            pl.BlockSpec((None, None, tk, d), lambda b, m, kk: (b, kk // nkh, row_blk0 + kk % nkh, 0)),
        ],
        out_specs=pl.BlockSpec((None, tm, d), lambda b, m, kk: (b, m, 0)),
        out_shape=jax.ShapeDtypeStruct((batch, n_pos, d), BF16),
        scratch_shapes=[pltpu.VMEM((tm, d), F32)],
        compiler_params=pltpu.CompilerParams(
            dimension_semantics=("arbitrary", "arbitrary", "arbitrary"), vmem_limit_bytes=VMEM_LIMIT_BYTES),
        name=name,
    )(rows, cphi, sphi, pq)


def fnet_layer(hs, mods, norm_g3, w_out, *, layer, batch, with_ctx_out):
    pq = fnet_channel_dft(hs, mods, norm_g3, _fnet_channel_table(), layer=layer, batch=batch)
    y = fnet_position_dft(pq, _fnet_tables(SEQ, FN_TM), n_pos=SEQ, tm=FN_TM, tk=FN_TK, row_blk0=0,
                          name=f"fnet_pos_lat_l{layer}")
    y_ctx = None
    if with_ctx_out:
        y_ctx = fnet_position_dft(pq, _fnet_tables(CTX_LEN, CTX_LEN), n_pos=CTX_LEN, tm=CTX_LEN, tk=CTX_LEN,
                                  row_blk0=SEQ // CTX_LEN, name=f"fnet_pos_ctx_l{layer}")
    return mixer_out_proj(y, y_ctx, w_out, hs, mods, layer=layer, batch=batch, name=f"fnet_out_l{layer}")


def kernel(x, c, ctx, c_ctx, mod_w, mod_b, norm_g, ffn1_w_gu, ffn1_w_down, ffn2_w_gu, ffn2_w_down,
           hgrn_w_in, hgrn_lb_logits, hgrn_g_norm, hgrn_w_out,
           mla_w_dqkv, mla_q_norm, mla_kv_norm, mla_w_uq, mla_w_ukv, mla_w_o,
           fnet_w_out, final_g):
    B, T, D = x.shape
    assert (B, T, D) == (BATCH, SEQ, D_MODEL) and ctx.shape == (BATCH, CTX_LEN, D_MODEL)
    n_lat = B * T
    n_ctx = B * CTX_LEN
    tm, tf = 512, 512
    rows_all = n_lat + n_ctx

    lb = jnp.cumsum(jax.nn.softmax(hgrn_lb_logits.astype(jnp.float32), axis=1), axis=1)
    lb = lb - lb[:, :1]
    rope_cos, rope_sin = mla_rope_tables(T)

    cc = jnp.concatenate([c, c_ctx[None, :], jnp.zeros((MOD_ROWS - B - 1, D), F32)], axis=0)
    mods = modulation(cc, mod_w, mod_b).reshape(DEPTH, MOD_ROWS, N_MOD, D)

    w1_gu, w1_down = ffn1_w_gu.astype(BF16), ffn1_w_down.astype(BF16)
    w2_gu, w2_down = ffn2_w_gu.astype(BF16), ffn2_w_down.astype(BF16)
    hg_w_in = hgrn_w_in.astype(BF16)
    norm_g3 = norm_g.reshape(DEPTH * N_SUB, 1, D)
    final_g2 = final_g.reshape(1, D)

    hs = jnp.concatenate([x.reshape(n_lat, D), ctx.reshape(n_ctx, D)], axis=0)
    for i in range(DEPTH):
        kind, j = i % N_MIXERS, i // N_MIXERS
        last = i == DEPTH - 1
        ctx_in = not (last and kind == 2)
        hs = ffn_sublayer(hs, mods, norm_g3, w1_gu, w1_down, final_g2, layer=i, s=0,
                          rows=rows_all if ctx_in else n_lat, tm=tm, tf=tf)
        if kind == 0:
            hs = hgrn_layer(hs, mods, norm_g3, hg_w_in, lb[0, j], lb[1, j], hgrn_g_norm[j],
                            hgrn_w_out[j].astype(BF16), layer=i, j=j, batch=B, with_ctx_out=not last)
        elif kind == 1:
            hs = mla_layer(hs, mods, norm_g3, mla_weights(mla_w_dqkv[j], mla_w_uq[j], mla_w_ukv[j]),
                           mla_q_norm[j].reshape(1, -1), mla_kv_norm[j].reshape(1, -1),
                           mla_w_o[j].astype(BF16), rope_cos, rope_sin, layer=i, batch=B, with_ctx_out=not last)
        else:
            hs = fnet_layer(hs, mods, norm_g3, fnet_w_out[j].astype(BF16), layer=i, batch=B, with_ctx_out=not last)
        hs = ffn_sublayer(hs, mods, norm_g3, w2_gu, w2_down, final_g2, layer=i, s=2,
                          rows=n_lat if last else rows_all, tm=tm, tf=tf, final=last)
    return hs.reshape(B, T, D)
```

```python
import functools

import numpy as np
import jax
import jax.numpy as jnp
from jax import lax
from jax.experimental import pallas as pl
from jax.experimental.pallas import tpu as pltpu

D_MODEL = 2048
BATCH = 2
SEQ = 4096
DEPTH = 4
GRID_W = 64
CTX_LEN = 256
N_MIXERS = 3
N_SUB = 3
D_FF = 5632
RMS_EPS = 1e-6

HG_HEAD_DIM = 128
HG_HEADS = D_MODEL // HG_HEAD_DIM
HG_CHUNK = 16

MLA_HEADS = 16
MLA_Q_RANK = 512
MLA_KV_RANK = 512
MLA_NOPE = 128
MLA_ROPE = 64
MLA_V = 128
MLA_SCALE = (MLA_NOPE + MLA_ROPE) ** -0.5
ROPE_THETA = 10000.0
ROPE_FREQS = MLA_ROPE // 4

FOURIER_GROUPS = 8

BF16 = jnp.bfloat16
F32 = jnp.float32

VMEM_LIMIT_BYTES = 56 * 1024 * 1024
LANES = 128
MOD_ROWS = 8
N_MOD = N_SUB * 3
MIX_SHIFT, MIX_SCALE, MIX_GATE = 3, 4, 5


def _mod_row(t, tm):
    return jnp.minimum(t // (SEQ // tm), BATCH)


def _mod_kernel(c_ref, w_ref, b_ref, o_ref):
    c = c_ref[...]
    a = (c * jax.nn.sigmoid(c)).astype(BF16)
    o_ref[...] = jnp.dot(a, w_ref[...].astype(BF16), preferred_element_type=F32) + b_ref[...]


def modulation(cc, mod_w, mod_b, *, tn=1024):
    depth, d, n = mod_w.shape
    return pl.pallas_call(
        _mod_kernel,
        grid=(depth, n // tn),
        in_specs=[
            pl.BlockSpec((MOD_ROWS, d), lambda i, j: (0, 0)),
            pl.BlockSpec((None, d, tn), lambda i, j: (i, 0, j)),
            pl.BlockSpec((None, 1, tn), lambda i, j: (i, 0, j)),
        ],
        out_specs=pl.BlockSpec((None, MOD_ROWS, tn), lambda i, j: (i, 0, j)),
        out_shape=jax.ShapeDtypeStruct((depth, MOD_ROWS, n), F32),
        compiler_params=pltpu.CompilerParams(
            dimension_semantics=("arbitrary", "arbitrary"), vmem_limit_bytes=VMEM_LIMIT_BYTES),
        name="modulation",
    )(cc, mod_w, mod_b.reshape(depth, 1, n))


def _adanorm(x, g, shift, scale):
    y = x * lax.rsqrt(jnp.mean(x * x, axis=-1, keepdims=True) + RMS_EPS)
    return (y * g) * (1.0 + scale) + shift


NORM_ROWS = 16


def _ffn_kernel(x_ref, mod_ref, g_ref, wg_ref, wu_ref, wd_ref, fg_ref, o_ref, n_ref, *, s, nf, final):
    f = pl.program_id(1)

    @pl.when(f == 0)
    def _():
        g, shift, scale = g_ref[...], mod_ref[3 * s:3 * s + 1, :], mod_ref[3 * s + 1:3 * s + 2, :]
        for i in range(x_ref.shape[0] // NORM_ROWS):
            r = slice(i * NORM_ROWS, (i + 1) * NORM_ROWS)
            n_ref[r, :] = _adanorm(x_ref[r, :], g, shift, scale).astype(BF16)
        o_ref[...] = jnp.zeros_like(o_ref)

    n = n_ref[...]
    gate = jnp.dot(n, wg_ref[...], preferred_element_type=F32)
    up = jnp.dot(n, wu_ref[...], preferred_element_type=F32)
    act = (gate * jax.nn.sigmoid(gate) * up).astype(BF16)
    o_ref[...] += jnp.dot(act, wd_ref[...].astype(BF16), preferred_element_type=F32)

    @pl.when(f == nf - 1)
    def _():
        half_gate = 0.5 * mod_ref[3 * s + 2:3 * s + 3, :]
        for i in range(x_ref.shape[0] // NORM_ROWS):
            r = slice(i * NORM_ROWS, (i + 1) * NORM_ROWS)
            h = x_ref[r, :] + half_gate * o_ref[r, :]
            if final:
                h = _rms(h) * fg_ref[...]
            o_ref[r, :] = h


def ffn_sublayer(h, mods, g, w_gu, w_down, final_g, *, layer, s, rows, tm, tf, final=False):
    d = h.shape[1]
    nf = D_FF // tf
    kern = functools.partial(_ffn_kernel, s=s, nf=nf, final=final)
    return pl.pallas_call(
        kern,
        grid=(rows // tm, nf),
        in_specs=[
            pl.BlockSpec((tm, d), lambda t, f: (t, 0)),
            pl.BlockSpec((None, None, N_MOD, d), lambda t, f: (layer, _mod_row(t, tm), 0, 0)),
            pl.BlockSpec((None, 1, d), lambda t, f: (layer * N_SUB + s, 0, 0)),
            pl.BlockSpec((None, d, tf), lambda t, f: (layer, 0, f)),
            pl.BlockSpec((None, d, tf), lambda t, f: (layer, 0, f + nf)),
            pl.BlockSpec((None, tf, d), lambda t, f: (layer, f, 0)),
            pl.BlockSpec((1, d), lambda t, f: (0, 0)),
        ],
        out_specs=pl.BlockSpec((tm, d), lambda t, f: (t, 0)),
        out_shape=jax.ShapeDtypeStruct((rows if final else h.shape[0], d), F32),
        scratch_shapes=[pltpu.VMEM((tm, d), BF16)],
        compiler_params=pltpu.CompilerParams(
            dimension_semantics=("arbitrary", "arbitrary"), vmem_limit_bytes=VMEM_LIMIT_BYTES),
        name=f"ffn_l{layer}_s{s}",
    )(h, mods, g, w_gu, w_gu, w_down, final_g)


TMX = 256
S_ALL = SEQ + CTX_LEN
LAT_TILES = SEQ // TMX
CTX_BLOCK = SEQ // TMX


def _resident(shape):
    return pl.BlockSpec(shape, lambda *_: (0,) * len(shape), pipeline_mode=pl.Buffered(1))


def _stream_tiles(batch):
    return batch * (LAT_TILES + CTX_LEN // TMX)


def _tile_batch(t, batch):
    lat = t < batch * LAT_TILES
    return jnp.where(lat, t // LAT_TILES, t - batch * LAT_TILES)


def _tile_block(t, batch):
    return jnp.where(t < batch * LAT_TILES, t % LAT_TILES, CTX_BLOCK)


def _tile_mod_row(t, batch):
    return jnp.where(t < batch * LAT_TILES, t // LAT_TILES, batch)


def _rms(x):
    return x * lax.rsqrt(jnp.mean(x * x, axis=-1, keepdims=True) + RMS_EPS)


def _mixer_input(x_ref, g_ref, mod_ref):
    return _adanorm(x_ref[...], g_ref[...], mod_ref[MIX_SHIFT:MIX_SHIFT + 1, :], mod_ref[MIX_SCALE:MIX_SCALE + 1, :])


MLA_QK = 2 * MLA_NOPE
LOG2_E = 1.4426950408889634
N_DQ = MLA_Q_RANK + MLA_KV_RANK


def _mla_proj_kernel(x_ref, mod_ref, g_ref, wd_ref, qn_ref, kvn_ref, wuq_ref, wukv_ref, cos_ref, sin_ref,
                     q_ref, k_ref, v_ref):
    n = _mixer_input(x_ref, g_ref, mod_ref).astype(BF16)
    proj = jnp.dot(n, wd_ref[...], preferred_element_type=F32)
    cq = (_rms(proj[:, :MLA_Q_RANK]) * qn_ref[...]).astype(BF16)
    ckv = (_rms(proj[:, MLA_Q_RANK:N_DQ]) * kvn_ref[...]).astype(BF16)
    cos, sin = cos_ref[...], sin_ref[...]
    kr = proj[:, N_DQ:N_DQ + LANES] * cos + proj[:, N_DQ + LANES:N_DQ + 2 * LANES] * sin
    kr = kr.astype(BF16)
    q = jnp.dot(cq, wuq_ref[...], preferred_element_type=F32) * (MLA_SCALE * LOG2_E)
    kv = jnp.dot(ckv, wukv_ref[...], preferred_element_type=F32)
    hn = MLA_HEADS * MLA_NOPE
    lane = lax.broadcasted_iota(jnp.int32, (x_ref.shape[0], LANES), 1)
    ones_col = jnp.where(lane == 0, 1.0, 0.0).astype(BF16)
    for h in range(MLA_HEADS):
        lo = h * LANES
        q_ref[h, :, 0:LANES] = q[:, lo:lo + LANES].astype(BF16)
        qr = q[:, hn + lo:hn + lo + LANES] * cos + q[:, 2 * hn + lo:2 * hn + lo + LANES] * sin
        q_ref[h, :, LANES:2 * LANES] = qr.astype(BF16)
        k_ref[h, :, 0:LANES] = kv[:, lo:lo + LANES].astype(BF16)
        k_ref[h, :, LANES:2 * LANES] = kr
        v_ref[h, :, 0:LANES] = kv[:, hn + lo:hn + lo + LANES].astype(BF16)
        v_ref[h, :, LANES:2 * LANES] = ones_col


def _rope_partner(width):
    idx = jnp.arange(width)
    return idx ^ ROPE_FREQS


def mla_weights(w_dqkv, w_uq, w_ukv):
    d = w_dqkv.shape[0]
    z = jnp.zeros((d, LANES - MLA_ROPE), w_dqkv.dtype)
    kr = w_dqkv[:, N_DQ:]
    wd = jnp.concatenate([w_dqkv[:, :N_DQ], kr, z, kr[:, _rope_partner(MLA_ROPE)], z], axis=1)
    wq = w_uq.reshape(MLA_Q_RANK, MLA_HEADS, MLA_NOPE + MLA_ROPE)
    qr = wq[:, :, MLA_NOPE:]
    zq = jnp.zeros((MLA_Q_RANK, MLA_HEADS, LANES - MLA_ROPE), w_uq.dtype)
    wuq = jnp.concatenate([
        wq[:, :, :MLA_NOPE].reshape(MLA_Q_RANK, -1),
        jnp.concatenate([qr, zq], axis=2).reshape(MLA_Q_RANK, -1),
        jnp.concatenate([qr[:, :, _rope_partner(MLA_ROPE)], zq], axis=2).reshape(MLA_Q_RANK, -1)], axis=1)
    wkv = w_ukv.reshape(MLA_KV_RANK, MLA_HEADS, MLA_NOPE + MLA_V)
    wukv = jnp.concatenate([wkv[:, :, :MLA_NOPE].reshape(MLA_KV_RANK, -1),
                            wkv[:, :, MLA_NOPE:].reshape(MLA_KV_RANK, -1)], axis=1)
    return wd.astype(BF16), wuq.astype(BF16), wukv.astype(BF16)


def mla_rope_tables(n_tokens):
    rows = n_tokens // GRID_W
    r = jnp.broadcast_to(jnp.arange(rows, dtype=F32)[:, None], (rows, GRID_W)).reshape(-1)
    col = jnp.broadcast_to(jnp.arange(GRID_W, dtype=F32)[None, :], (rows, GRID_W)).reshape(-1)
    inv_freq = ROPE_THETA ** (-jnp.arange(ROPE_FREQS, dtype=F32) / ROPE_FREQS)
    ang = jnp.stack([r, col], axis=-1)[..., None] * inv_freq
    cos = jnp.broadcast_to(jnp.cos(ang)[:, :, None, :], (n_tokens, 2, 2, ROPE_FREQS)).reshape(n_tokens, MLA_ROPE)
    sin = jnp.sin(ang)
    sin = jnp.stack([-sin, sin], axis=2).reshape(n_tokens, MLA_ROPE)
    pad = jnp.zeros((n_tokens, LANES - MLA_ROPE), F32)
    cos = jnp.concatenate([cos, pad], axis=1)
    sin = jnp.concatenate([sin, pad], axis=1)
    ctx_cos = jnp.concatenate([jnp.ones((CTX_LEN, MLA_ROPE), F32), jnp.zeros((CTX_LEN, LANES - MLA_ROPE), F32)], axis=1)
    return (jnp.concatenate([cos, ctx_cos], axis=0),
            jnp.concatenate([sin, jnp.zeros((CTX_LEN, LANES), F32)], axis=0))


def mla_project(hs, mods, norm_g3, wd, q_norm, kv_norm, wuq, wukv, cos, sin, *, layer, batch):
    d = hs.shape[1]
    bmap = lambda t: (_tile_batch(t, batch), 0, _tile_block(t, batch), 0)
    return pl.pallas_call(
        _mla_proj_kernel,
        grid=(_stream_tiles(batch),),
        in_specs=[
            pl.BlockSpec((TMX, d), lambda t: (t, 0)),
            pl.BlockSpec((None, None, N_MOD, d), lambda t: (layer, _tile_mod_row(t, batch), 0, 0)),
            pl.BlockSpec((None, 1, d), lambda t: (layer * N_SUB + 1, 0, 0)),
            _resident(wd.shape),
            _resident((1, MLA_Q_RANK)),
            _resident((1, MLA_KV_RANK)),
            _resident(wuq.shape),
            _resident(wukv.shape),
            pl.BlockSpec((TMX, LANES), lambda t: (_tile_block(t, batch), 0)),
            pl.BlockSpec((TMX, LANES), lambda t: (_tile_block(t, batch), 0)),
        ],
        out_specs=[
            pl.BlockSpec((None, MLA_HEADS, TMX, MLA_QK), bmap),
            pl.BlockSpec((None, MLA_HEADS, TMX, MLA_QK), bmap),
            pl.BlockSpec((None, MLA_HEADS, TMX, MLA_QK), bmap),
        ],
        out_shape=[
            jax.ShapeDtypeStruct((batch, MLA_HEADS, S_ALL, MLA_QK), BF16),
            jax.ShapeDtypeStruct((batch, MLA_HEADS, S_ALL, MLA_QK), BF16),
            jax.ShapeDtypeStruct((batch, MLA_HEADS, S_ALL, MLA_QK), BF16),
        ],
        compiler_params=pltpu.CompilerParams(
            dimension_semantics=("arbitrary",), vmem_limit_bytes=VMEM_LIMIT_BYTES),
        name=f"mla_proj_l{layer}",
    )(hs, mods, norm_g3, wd, q_norm, kv_norm, wuq, wukv, cos, sin)


def _attn_kernel(q_ref, k_ref, v_ref, o_ref, *, n_chain):
    rows = q_ref.shape[0] // n_chain
    for c in range(n_chain):
        r = slice(c * rows, (c + 1) * rows)
        s = lax.dot_general(q_ref[r, :], k_ref[...], (((1,), (1,)), ((), ())), preferred_element_type=F32)
        p = jnp.exp2(s - jnp.max(s, axis=-1, keepdims=True))
        o = jnp.dot(p.astype(BF16), v_ref[...], preferred_element_type=F32)
        o_ref[r, :] = (o[:, :MLA_V] / o[:, MLA_V:MLA_V + 1]).astype(BF16)


def mla_attention(q, k, v, *, tq, n_chain, q_block0, n_q, k_rows, k_block, name):
    batch, heads, _, _ = q.shape
    return pl.pallas_call(
        functools.partial(_attn_kernel, n_chain=n_chain),
        grid=(batch, heads, n_q),
        in_specs=[
            pl.BlockSpec((None, None, tq, MLA_QK), lambda b, h, i: (b, h, q_block0 + i, 0)),
            pl.BlockSpec((None, None, k_rows, MLA_QK), lambda b, h, i: (b, h, k_block, 0)),
            pl.BlockSpec((None, None, k_rows, MLA_QK), lambda b, h, i: (b, h, k_block, 0)),
        ],
        out_specs=pl.BlockSpec((None, tq, MLA_V), lambda b, h, i: (b, i, h)),
        out_shape=jax.ShapeDtypeStruct((batch, n_q * tq, heads * MLA_V), BF16),
        compiler_params=pltpu.CompilerParams(
            dimension_semantics=("arbitrary", "arbitrary", "arbitrary"), vmem_limit_bytes=VMEM_LIMIT_BYTES),
        name=name,
    )(q, k, v)


def _out_proj_kernel(yl_ref, yc_ref, w_ref, x_ref, mod_ref, o_ref, *, n_lat_tiles, ctx_out):
    t = pl.program_id(0)

    def update(y_ref):
        y = jnp.dot(y_ref[...], w_ref[...], preferred_element_type=F32)
        o_ref[...] = x_ref[...] + mod_ref[MIX_GATE:MIX_GATE + 1, :] * y

    @pl.when(t < n_lat_tiles)
    def _():
        update(yl_ref)

    @pl.when(t >= n_lat_tiles)
    def _():
        if ctx_out:
            update(yc_ref)
        else:
            o_ref[...] = x_ref[...]


def mixer_out_proj(y_lat, y_ctx, w, hs, mods, *, layer, batch, name):
    d = hs.shape[1]
    kdim = y_lat.shape[2]
    nl = batch * LAT_TILES
    ctx_out = y_ctx is not None
    kern = functools.partial(_out_proj_kernel, n_lat_tiles=nl, ctx_out=ctx_out)

    def lat_map(t):
        tl = jnp.minimum(t, nl - 1)
        return (tl // LAT_TILES, tl % LAT_TILES, 0)

    return pl.pallas_call(
        kern,
        grid=(_stream_tiles(batch),),
        in_specs=[
            pl.BlockSpec((None, TMX, kdim), lat_map),
            pl.BlockSpec((None, TMX, kdim), lambda t: (jnp.maximum(t - nl, 0), 0, 0)),
            _resident(w.shape),
            pl.BlockSpec((TMX, d), lambda t: (t, 0)),
            pl.BlockSpec((None, None, N_MOD, d), lambda t: (layer, _tile_mod_row(t, batch), 0, 0)),
        ],
        out_specs=pl.BlockSpec((TMX, d), lambda t: (t, 0)),
        out_shape=jax.ShapeDtypeStruct(hs.shape, F32),
        compiler_params=pltpu.CompilerParams(
            dimension_semantics=("arbitrary",), vmem_limit_bytes=VMEM_LIMIT_BYTES),
        name=name,
    )(y_lat, y_ctx if ctx_out else y_lat[:, :CTX_LEN], w, hs, mods)


def mla_layer(hs, mods, norm_g3, weights, q_norm, kv_norm, w_o, cos, sin, *, layer, batch, with_ctx_out):
    wd, wuq, wukv = weights
    q, k, v = mla_project(hs, mods, norm_g3, wd, q_norm, kv_norm, wuq, wukv, cos, sin, layer=layer, batch=batch)
    tq = 4096
    att = mla_attention(q, k, v, tq=tq, n_chain=16, q_block0=0, n_q=SEQ // tq, k_rows=S_ALL, k_block=0,
                        name=f"mla_attn_lat_l{layer}")
    att_ctx = None
    if with_ctx_out:
        att_ctx = mla_attention(q, k, v, tq=CTX_LEN, n_chain=1, q_block0=SEQ // CTX_LEN, n_q=1, k_rows=CTX_LEN,
                                k_block=SEQ // CTX_LEN, name=f"mla_attn_ctx_l{layer}")
    return mixer_out_proj(att, att_ctx, w_o, hs, mods, layer=layer, batch=batch, name=f"mla_out_l{layer}")


HG_WIN = 128
HG_NH = 8
N_TBLK = S_ALL // TMX
HG_PROJ_TM = 512


def _hgrn_proj_kernel(x_ref, mod_ref, g_ref, w_ref, o_ref):
    n = _mixer_input(x_ref, g_ref, mod_ref).astype(BF16)
    o_ref[...] = jnp.dot(n, w_ref[...], preferred_element_type=F32)


def hgrn_project(hs, mods, norm_g3, w_in, *, layer, j):
    rows, d = hs.shape
    n_out = w_in.shape[2]
    tm = HG_PROJ_TM
    return pl.pallas_call(
        _hgrn_proj_kernel,
        grid=(n_out // d, rows // tm),
        in_specs=[
            pl.BlockSpec((tm, d), lambda c, t: (t, 0)),
            pl.BlockSpec((None, None, N_MOD, d), lambda c, t: (layer, _mod_row(t, tm), 0, 0)),
            pl.BlockSpec((None, 1, d), lambda c, t: (layer * N_SUB + 1, 0, 0)),
            pl.BlockSpec((None, d, d), lambda c, t: (j, 0, c)),
        ],
        out_specs=pl.BlockSpec((tm, d), lambda c, t: (t, c)),
        out_shape=jax.ShapeDtypeStruct((rows, n_out), F32),
        compiler_params=pltpu.CompilerParams(
            dimension_semantics=("arbitrary", "arbitrary"), vmem_limit_bytes=VMEM_LIMIT_BYTES),
        name=f"hgrn_proj_l{layer}",
    )(hs, mods, norm_g3, w_in)


def _split3(x):
    hi = x.astype(BF16)
    r1 = x - hi.astype(F32)
    mid = r1.astype(BF16)
    lo = (r1 - mid.astype(F32)).astype(BF16)
    return hi, mid, lo


def _hgrn_scan_kernel(q_ref, z_ref, v_ref, lb_ref, om_ref, tri_ref, o_ref,
                      st_ref, qd_ref, kin_ref, kte_ref, vb_ref, tot_ref, *, reverse):
    ts = pl.program_id(2)

    @pl.when(ts == 0)
    def _():
        st_ref[...] = jnp.zeros_like(st_ref)

    z = z_ref[...]
    e = jnp.exp(-jnp.abs(z))
    r = 1.0 / (1.0 + e)
    er = e * r
    pos = z >= 0
    om = om_ref[...]
    f = lb_ref[...] + om * jnp.where(pos, r, er)
    log_f = jnp.log(f)
    k = om * jnp.where(pos, er, r)
    tri = tri_ref[...]
    cum = sum(jnp.dot(tri, piece, preferred_element_type=F32) for piece in _split3(log_f))
    b_inc = cum[:TMX]
    b_rest = cum[TMX:]
    qr = q_ref[...]
    qd_ref[...] = (qr * jax.nn.sigmoid(qr) * jnp.exp(b_inc)).astype(BF16)
    kin_ref[...] = (k * jnp.exp(-b_inc)).astype(BF16)
    kte_ref[...] = (k * jnp.exp(b_rest)).astype(BF16)
    vb_ref[...] = v_ref[...].astype(BF16)
    tot = b_inc + b_rest
    dec_all = []
    for h in range(HG_NH):
        tot_ref[h] = tot[:, h * HG_HEAD_DIM:(h + 1) * HG_HEAD_DIM]
        dec_all.append(jnp.exp(tot_ref[h, pl.ds(0, TMX // HG_CHUNK, stride=HG_CHUNK), :]))

    wi = lax.broadcasted_iota(jnp.int32, (HG_WIN, HG_WIN), 0)
    wj = lax.broadcasted_iota(jnp.int32, (HG_WIN, HG_WIN), 1)
    same_chunk = (wi // HG_CHUNK) == (wj // HG_CHUNK)
    causal = same_chunk & ((wj >= wi) if reverse else (wj <= wi))

    n_win = TMX // HG_WIN
    per_win = HG_WIN // HG_CHUNK
    wins = list(range(n_win - 1, -1, -1) if reverse else range(n_win))
    chunks = list(range(per_win - 1, -1, -1) if reverse else range(per_win))
    def only_chunk(x, c):
        parts = []
        if c > 0:
            parts.append(jnp.zeros((c * HG_CHUNK, x.shape[1]), x.dtype))
        parts.append(x[c * HG_CHUNK:(c + 1) * HG_CHUNK, :])
        if c < per_win - 1:
            parts.append(jnp.zeros(((per_win - 1 - c) * HG_CHUNK, x.shape[1]), x.dtype))
        return jnp.concatenate(parts, axis=0)

    states = [st_ref[h] for h in range(HG_NH)]
    heads = range(HG_NH)
    hcols = [slice(h * HG_HEAD_DIM, (h + 1) * HG_HEAD_DIM) for h in heads]
    for w in wins:
        rows = slice(w * HG_WIN, (w + 1) * HG_WIN)
        o_intra, upd = [], []
        for h in heads:
            qd_w, kin_w, kte_w, v_w = (r[rows, hcols[h]] for r in (qd_ref, kin_ref, kte_ref, vb_ref))
            a = lax.dot_general(qd_w, kin_w, (((1,), (1,)), ((), ())), preferred_element_type=F32)
            a = jnp.where(causal, a, 0.0).astype(BF16)
            o_intra.append(jnp.dot(a, v_w, preferred_element_type=F32))
            v_t = v_ref[rows, hcols[h]].T.astype(BF16)
            rhs = jnp.concatenate([only_chunk(kte_w, c) for c in range(per_win)], axis=1)
            upd.append(jnp.dot(v_t, rhs, preferred_element_type=F32))
        entry = [dict() for _ in heads]
        for h in heads:
            st = states[h]
            for c in chunks:
                entry[h][c] = st.astype(BF16)
                ci = w * per_win + c
                st = st * dec_all[h][ci:ci + 1, :] + upd[h][:, c * HG_HEAD_DIM:(c + 1) * HG_HEAD_DIM]
            states[h] = st
        for h in heads:
            qd_w = qd_ref[rows, hcols[h]]
            lhs = jnp.concatenate([only_chunk(qd_w, c) for c in range(per_win)], axis=1)
            ent = jnp.concatenate([entry[h][c] for c in range(per_win)], axis=1)
            o_inter = lax.dot_general(lhs, ent, (((1,), (1,)), ((), ())), preferred_element_type=F32)
            o_ref[rows, hcols[h]] = o_intra[h] + o_inter
    for h in heads:
        st_ref[h] = states[h]


def _hgrn_cumsum_matrix(reverse):
    r = np.arange(TMX)[:, None]
    c = np.arange(TMX)[None, :]
    same = (r // HG_CHUNK) == (c // HG_CHUNK)
    inc, rest = ((c >= r), (c < r)) if reverse else ((c <= r), (c > r))
    return jnp.asarray(np.concatenate([same & inc, same & rest], axis=0), BF16)


def hgrn_scan(p, lb, om, *, batch, reverse, z_col, layer):
    rows, n_out = p.shape
    d = n_out // 5
    wcols = HG_NH * HG_HEAD_DIM
    groups = d // wcols

    def rblk(b, ts):
        lat = (N_TBLK - 1 - ts) if reverse else (ts - 1)
        return jnp.where(ts == 0, batch * LAT_TILES + b, b * LAT_TILES + lat)

    col = lambda base: (lambda b, g, ts: (rblk(b, ts), base * groups + g))
    row_spec = pl.BlockSpec((1, wcols), lambda b, g, ts: (0, g))
    kern = functools.partial(_hgrn_scan_kernel, reverse=reverse)
    return pl.pallas_call(
        kern,
        grid=(batch, groups, N_TBLK),
        in_specs=[
            pl.BlockSpec((TMX, wcols), col(0)),
            pl.BlockSpec((TMX, wcols), col(z_col)),
            pl.BlockSpec((TMX, wcols), col(3)),
            row_spec, row_spec, _resident((2 * TMX, TMX)),
        ],
        out_specs=pl.BlockSpec((TMX, wcols), lambda b, g, ts: (rblk(b, ts), g)),
        out_shape=jax.ShapeDtypeStruct((rows, d), F32),
        scratch_shapes=[
            pltpu.VMEM((HG_NH, HG_HEAD_DIM, HG_HEAD_DIM), F32),
            pltpu.VMEM((TMX, wcols), BF16), pltpu.VMEM((TMX, wcols), BF16),
            pltpu.VMEM((TMX, wcols), BF16), pltpu.VMEM((TMX, wcols), BF16),
            pltpu.VMEM((HG_NH, TMX, HG_HEAD_DIM), F32),
        ],
        compiler_params=pltpu.CompilerParams(
            dimension_semantics=("arbitrary", "arbitrary", "arbitrary"), vmem_limit_bytes=VMEM_LIMIT_BYTES),
        name=f"hgrn_scan_{'bwd' if reverse else 'fwd'}_l{layer}",
    )(p, p, p, lb, om, _hgrn_cumsum_matrix(reverse))


def _hgrn_readout_kernel(of_ref, ob_ref, gate_ref, gn_ref, w_ref, x_ref, mod_ref, o_ref, *, n_lat_tiles, ctx_out):
    t = pl.program_id(0)

    def update():
        o = of_ref[...] + ob_ref[...]
        gate = gate_ref[...]
        gs = gate * jax.nn.sigmoid(gate)
        parts = []
        for h in range(HG_HEADS):
            cols = slice(h * HG_HEAD_DIM, (h + 1) * HG_HEAD_DIM)
            parts.append((_rms(o[:, cols]) * gn_ref[:, cols] * gs[:, cols]).astype(BF16))
        y = jnp.dot(jnp.concatenate(parts, axis=1), w_ref[...], preferred_element_type=F32)
        o_ref[...] = x_ref[...] + mod_ref[MIX_GATE:MIX_GATE + 1, :] * y

    if ctx_out:
        update()
    else:
        pl.when(t < n_lat_tiles)(update)

        @pl.when(t >= n_lat_tiles)
        def _():
            o_ref[...] = x_ref[...]


def hgrn_readout(o_f, o_b, p, g_norm, w_out, hs, mods, *, layer, batch, ctx_out):
    d = hs.shape[1]
    kern = functools.partial(_hgrn_readout_kernel, n_lat_tiles=batch * LAT_TILES, ctx_out=ctx_out)
    return pl.pallas_call(
        kern,
        grid=(_stream_tiles(batch),),
        in_specs=[
            pl.BlockSpec((TMX, d), lambda t: (t, 0)),
            pl.BlockSpec((TMX, d), lambda t: (t, 0)),
            pl.BlockSpec((TMX, d), lambda t: (t, 4)),
            _resident((1, d)),
            _resident(w_out.shape),
            pl.BlockSpec((TMX, d), lambda t: (t, 0)),
            pl.BlockSpec((None, None, N_MOD, d), lambda t: (layer, _tile_mod_row(t, batch), 0, 0)),
        ],
        out_specs=pl.BlockSpec((TMX, d), lambda t: (t, 0)),
        out_shape=jax.ShapeDtypeStruct(hs.shape, F32),
        compiler_params=pltpu.CompilerParams(
            dimension_semantics=("arbitrary",), vmem_limit_bytes=VMEM_LIMIT_BYTES),
        name=f"hgrn_out_l{layer}",
    )(o_f, o_b, p, g_norm, w_out, hs, mods)


def hgrn_layer(hs, mods, norm_g3, w_in, lb_fwd, lb_bwd, g_norm, w_out, *, layer, j, batch, with_ctx_out):
    p = hgrn_project(hs, mods, norm_g3, w_in, layer=layer, j=j)
    outs = []
    for reverse, lb, z_col in ((False, lb_fwd, 1), (True, lb_bwd, 2)):
        lb = lb.reshape(1, -1)
        outs.append(hgrn_scan(p, lb, 1.0 - lb, batch=batch, reverse=reverse, z_col=z_col, layer=layer))
    return hgrn_readout(outs[0], outs[1], p, g_norm.reshape(1, -1), w_out, hs, mods,
                        layer=layer, batch=batch, ctx_out=with_ctx_out)


FN_GC = D_MODEL // FOURIER_GROUPS
FN_TM = 512
FN_TK = 2048


def _fnet_tables(n_pos, tm):
    t = np.arange(n_pos, dtype=np.int64)
    ang = lambda k: 2.0 * np.pi * ((k[:, None] * t[None, :]) % n_pos) / n_pos
    phi = ang(np.arange(tm, dtype=np.int64))
    th = ang(np.arange(0, n_pos, tm, dtype=np.int64))
    sc = n_pos ** -0.5
    rows = np.stack([np.stack([np.cos(th), -np.sin(th)], axis=1),
                     np.stack([-np.sin(th), -np.cos(th)], axis=1)], axis=1) * sc
    rows = rows.reshape(-1, 2, n_pos)
    return (jnp.asarray(rows, F32), jnp.asarray(np.cos(phi), F32), jnp.asarray(np.sin(phi), F32))


def _fnet_channel_table():
    c = np.arange(FN_GC, dtype=np.int64)
    ang = 2.0 * np.pi * ((c[:, None] * c[None, :]) % FN_GC) / FN_GC
    return jnp.asarray(np.concatenate([np.cos(ang), np.sin(ang)], axis=1) * FN_GC ** -0.5, F32).astype(BF16)


def _fnet_chan_kernel(x_ref, mod_ref, g_ref, cs_ref, o_ref):
    n = _mixer_input(x_ref, g_ref, mod_ref).astype(BF16)
    for g in range(FOURIER_GROUPS):
        cols = slice(g * FN_GC, (g + 1) * FN_GC)
        pq = jnp.dot(n[:, cols], cs_ref[...], preferred_element_type=F32)
        o_ref[0, :, cols] = pq[:, :FN_GC].astype(BF16)
        o_ref[1, :, cols] = pq[:, FN_GC:].astype(BF16)


def fnet_channel_dft(hs, mods, norm_g3, cs, *, layer, batch):
    d = hs.shape[1]
    return pl.pallas_call(
        _fnet_chan_kernel,
        grid=(_stream_tiles(batch),),
        in_specs=[
            pl.BlockSpec((TMX, d), lambda t: (t, 0)),
            pl.BlockSpec((None, None, N_MOD, d), lambda t: (layer, _tile_mod_row(t, batch), 0, 0)),
            pl.BlockSpec((None, 1, d), lambda t: (layer * N_SUB + 1, 0, 0)),
            _resident(cs.shape),
        ],
        out_specs=pl.BlockSpec((None, 2, TMX, d), lambda t: (_tile_batch(t, batch), 0, _tile_block(t, batch), 0)),
        out_shape=jax.ShapeDtypeStruct((batch, 2, S_ALL, d), BF16),
        compiler_params=pltpu.CompilerParams(
            dimension_semantics=("arbitrary",), vmem_limit_bytes=VMEM_LIMIT_BYTES),
        name=f"fnet_chan_l{layer}",
    )(hs, mods, norm_g3, cs)


def _fnet_pos_kernel(rt_ref, cphi_ref, sphi_ref, pq_ref, o_ref, acc_ref, *, nk, tk):
    kk = pl.program_id(2)
    cols = pl.ds(pl.multiple_of((kk % (nk // 2)) * tk, tk), tk)
    tile = (rt_ref[0:1, :] * cphi_ref[:, cols] + rt_ref[1:2, :] * sphi_ref[:, cols]).astype(BF16)

    @pl.when(kk == 0)
    def _():
        acc_ref[...] = jnp.zeros_like(acc_ref)

    acc_ref[...] += jnp.dot(tile, pq_ref[...], preferred_element_type=F32)

    @pl.when(kk == nk - 1)
    def _():
        o_ref[...] = acc_ref[...].astype(BF16)


def fnet_position_dft(pq, tables, *, n_pos, tm, tk, row_blk0, name):
    batch, _, _, d = pq.shape
    rows, cphi, sphi = tables
    n_m, nkh = n_pos // tm, n_pos // tk
    nk = 2 * nkh
    kern = functools.partial(_fnet_pos_kernel, nk=nk, tk=tk)
    return pl.pallas_call(
        kern,
        grid=(batch, n_m, nk),
        in_specs=[
            pl.BlockSpec((None, 2, tk), lambda b, m, kk: (m * 2 + kk // nkh, 0, kk % nkh)),
            _resident(cphi.shape),
            _resident(sphi.shape),
            pl.BlockSpec((None, None, tk, d), lambda b, m, kk: (b, kk // nkh, row_blk0 + kk % nkh, 0)),
        ],
        out_specs=pl.BlockSpec((None, tm, d), lambda b, m, kk: (b, m, 0)),
        out_shape=jax.ShapeDtypeStruct((batch, n_pos, d), BF16),
        scratch_shapes=[pltpu.VMEM((tm, d), F32)],
        compiler_params=pltpu.CompilerParams(
            dimension_semantics=("arbitrary", "arbitrary", "arbitrary"), vmem_limit_bytes=VMEM_LIMIT_BYTES),
        name=name,
    )(rows, cphi, sphi, pq)


def fnet_layer(hs, mods, norm_g3, w_out, *, layer, batch, with_ctx_out):
    pq = fnet_channel_dft(hs, mods, norm_g3, _fnet_channel_table(), layer=layer, batch=batch)
    y = fnet_position_dft(pq, _fnet_tables(SEQ, FN_TM), n_pos=SEQ, tm=FN_TM, tk=FN_TK, row_blk0=0,
                          name=f"fnet_pos_lat_l{layer}")
    y_ctx = None
    if with_ctx_out:
        y_ctx = fnet_position_dft(pq, _fnet_tables(CTX_LEN, CTX_LEN), n_pos=CTX_LEN, tm=CTX_LEN, tk=CTX_LEN,
                                  row_blk0=SEQ // CTX_LEN, name=f"fnet_pos_ctx_l{layer}")
    return mixer_out_proj(y, y_ctx, w_out, hs, mods, layer=layer, batch=batch, name=f"fnet_out_l{layer}")


def kernel(x, c, ctx, c_ctx, mod_w, mod_b, norm_g, ffn1_w_gu, ffn1_w_down, ffn2_w_gu, ffn2_w_down,
           hgrn_w_in, hgrn_lb_logits, hgrn_g_norm, hgrn_w_out,
           mla_w_dqkv, mla_q_norm, mla_kv_norm, mla_w_uq, mla_w_ukv, mla_w_o,
           fnet_w_out, final_g):
    B, T, D = x.shape
    assert (B, T, D) == (BATCH, SEQ, D_MODEL) and ctx.shape == (BATCH, CTX_LEN, D_MODEL)
    n_lat = B * T
    n_ctx = B * CTX_LEN
    tm, tf = 512, 512
    rows_all = n_lat + n_ctx

    lb = jnp.cumsum(jax.nn.softmax(hgrn_lb_logits.astype(jnp.float32), axis=1), axis=1)
    lb = lb - lb[:, :1]
    rope_cos, rope_sin = mla_rope_tables(T)

    cc = jnp.concatenate([c, c_ctx[None, :], jnp.zeros((MOD_ROWS - B - 1, D), F32)], axis=0)
    mods = modulation(cc, mod_w, mod_b).reshape(DEPTH, MOD_ROWS, N_MOD, D)

    w1_gu, w1_down = ffn1_w_gu.astype(BF16), ffn1_w_down
    w2_gu, w2_down = ffn2_w_gu.astype(BF16), ffn2_w_down
    hg_w_in = hgrn_w_in.astype(BF16)
    norm_g3 = norm_g.reshape(DEPTH * N_SUB, 1, D)
    final_g2 = final_g.reshape(1, D)

    hs = jnp.concatenate([x.reshape(n_lat, D), ctx.reshape(n_ctx, D)], axis=0)
    for i in range(DEPTH):
        kind, j = i % N_MIXERS, i // N_MIXERS
        last = i == DEPTH - 1
        ctx_in = not (last and kind == 2)
        hs = ffn_sublayer(hs, mods, norm_g3, w1_gu, w1_down, final_g2, layer=i, s=0,
                          rows=rows_all if ctx_in else n_lat, tm=tm, tf=tf)
        if kind == 0:
            hs = hgrn_layer(hs, mods, norm_g3, hg_w_in, lb[0, j], lb[1, j], hgrn_g_norm[j],
                            hgrn_w_out[j].astype(BF16), layer=i, j=j, batch=B, with_ctx_out=not last)
        elif kind == 1:
            hs = mla_layer(hs, mods, norm_g3, mla_weights(mla_w_dqkv[j], mla_w_uq[j], mla_w_ukv[j]),
                           mla_q_norm[j].reshape(1, -1), mla_kv_norm[j].reshape(1, -1),
                           mla_w_o[j].astype(BF16), rope_cos, rope_sin, layer=i, batch=B, with_ctx_out=not last)
        else:
            hs = fnet_layer(hs, mods, norm_g3, fnet_w_out[j].astype(BF16), layer=i, batch=B, with_ctx_out=not last)
        hs = ffn_sublayer(hs, mods, norm_g3, w2_gu, w2_down, final_g2, layer=i, s=2,
                          rows=n_lat if last else rows_all, tm=tm, tf=tf, final=last)
    return hs.reshape(B, T, D)
```
